```python
import math
import jax, jax.numpy as jnp
from jax import lax
import numpy as np

D_MODEL = 1024
BATCH = 4
SEQ = 8192
DEPTH = 2

BLOCK = 128
BRANCH_W = D_MODEL // 2
N_BRANCH = 4
RET_HEADS = 4
RET_DK = BRANCH_W // RET_HEADS
RET_DV = BRANCH_W // RET_HEADS
ROPE_THETA = 10000.0
LRU_HEADS = 8
LRU_HD = BRANCH_W // LRU_HEADS
CONV_W = 4
LRU_C = 8.0
SB_HEADS = 4
SB_HD = BRANCH_W // SB_HEADS
SG_GROUPS = 4
SG_GD = BRANCH_W // SG_GROUPS
D_FF = 4 * D_MODEL
ALPHA = (2 * DEPTH) ** 0.25
BETA = (8 * DEPTH) ** -0.25
LN_EPS = 1e-5

SPLIT_SIZES = (BRANCH_W,) * 9 + (2 * BRANCH_W, N_BRANCH * D_MODEL)
D_IN = int(sum(SPLIT_SIZES))
SPLIT_POINTS = tuple(int(s) for s in np.cumsum(SPLIT_SIZES)[:-1])

kernel_name = "hybrid_retention_rglru_stickbreak_gmlp"


def layer_norm(x, g, b):
    xf = x.astype(jnp.float32)
    mu = jnp.mean(xf, axis=-1, keepdims=True)
    var = jnp.mean(jnp.square(xf - mu), axis=-1, keepdims=True)
    y = (xf - mu) * lax.rsqrt(var + LN_EPS)
    return (y * g.astype(jnp.float32) + b.astype(jnp.float32)).astype(x.dtype)


def rotary(x, pos):
    half = x.shape[-1] // 2
    inv_freq = ROPE_THETA ** (-jnp.arange(half, dtype=jnp.float32) / half)
    ang = pos[:, None] * inv_freq[None, :]
    cos = jnp.cos(ang)[None, :, None, :]
    sin = jnp.sin(ang)[None, :, None, :]
    x1, x2 = x[..., :half], x[..., half:]
    return jnp.concatenate([x1 * cos - x2 * sin, x2 * cos + x1 * sin], axis=-1)


def retention(q, k, v, g):
    dtype = q.dtype
    B, S, _ = q.shape
    N = S // BLOCK
    pos = jnp.arange(S, dtype=jnp.float32)
    qh = rotary(q.astype(jnp.float32).reshape(B, S, RET_HEADS, RET_DK), pos)
    kh = rotary(k.astype(jnp.float32).reshape(B, S, RET_HEADS, RET_DK), pos) * (RET_DK ** -0.5)
    vh = v.astype(jnp.float32).reshape(B, S, RET_HEADS, RET_DV)

    def chunk(t):
        return t.reshape(B, N, BLOCK, RET_HEADS, -1).transpose(0, 3, 1, 2, 4)

    qc, kc, vc = chunk(qh), chunk(kh), chunk(vh)
    log_g = jnp.log1p(-(2.0 ** (-5.0 - jnp.arange(RET_HEADS, dtype=jnp.float32))))
    idx = jnp.arange(BLOCK, dtype=jnp.float32)
    diff = idx[:, None] - idx[None, :]
    decay = jnp.where(diff >= 0, jnp.exp(log_g[:, None, None] * jnp.maximum(diff, 0.0)), 0.0)
    scores = jnp.einsum('bhncd,bhnmd->bhncm', qc, kc) * decay[None, :, None]
    inner = jnp.einsum('bhncm,bhnme->bhnce', scores, vc)
    k_dec = jnp.exp(log_g[:, None] * (BLOCK - 1.0 - idx)[None, :])
    kv = jnp.einsum('bhnmd,bhnme->nbhde', kc * k_dec[None, :, None, :, None], vc)
    chunk_decay = jnp.exp(log_g * BLOCK)[None, :, None, None]

    def step(state, kv_n):
        return chunk_decay * state + kv_n, state

    _, prev = lax.scan(step, jnp.zeros((B, RET_HEADS, RET_DK, RET_DV), jnp.float32), kv)
    q_dec = jnp.exp(log_g[:, None] * (idx + 1.0)[None, :])
    cross = jnp.einsum('bhncd,nbhde->bhnce', qc * q_dec[None, :, None, :, None], prev)
    y = (inner + cross).transpose(0, 2, 3, 1, 4).reshape(B, S, RET_HEADS, RET_DV)
    mu = jnp.mean(y, axis=-1, keepdims=True)
    var = jnp.mean(jnp.square(y - mu), axis=-1, keepdims=True)
    y = ((y - mu) * lax.rsqrt(var + LN_EPS)).reshape(B, S, BRANCH_W)
    return (jax.nn.silu(g.astype(jnp.float32)) * y).astype(dtype)


def rg_lru_branch(xb, gate_in, conv_w, conv_b, w_a, b_a, w_x, b_x, lam):
    dtype = xb.dtype
    B, S, W = xb.shape
    x = xb.astype(jnp.float32)
    rhs = conv_w.astype(jnp.float32).reshape(CONV_W, 1, W)
    xc = lax.conv_general_dilated(x, rhs, window_strides=(1,), padding=[(CONV_W - 1, 0)],
                                  dimension_numbers=('NWC', 'WIO', 'NWC'),
                                  feature_group_count=W) + conv_b.astype(jnp.float32)
    xh = xc.reshape(B, S, LRU_HEADS, LRU_HD)
    r = jax.nn.sigmoid(jnp.einsum('bshi,hij->bshj', xh, w_a.astype(jnp.float32)).reshape(B, S, W)
                       + b_a.astype(jnp.float32))
    i = jax.nn.sigmoid(jnp.einsum('bshi,hij->bshj', xh, w_x.astype(jnp.float32)).reshape(B, S, W)
                       + b_x.astype(jnp.float32))
    log_a = -LRU_C * r * jax.nn.softplus(-lam.astype(jnp.float32))
    a = jnp.exp(log_a)
    u = jnp.sqrt(-jnp.expm1(2.0 * log_a)) * (i * xc)

    def combine(c1, c2):
        a1, b1 = c1
        a2, b2 = c2
        return a1 * a2, a2 * b1 + b2

    _, h = lax.associative_scan(combine, (a, u), axis=1)
    return (h * jax.nn.gelu(gate_in.astype(jnp.float32))).astype(dtype)


def stick_breaking(q, k, v):
    dtype = q.dtype
    B, S, _ = q.shape
    N = S // BLOCK

    def heads(t):
        return t.astype(jnp.float32).reshape(B, S, SB_HEADS, SB_HD).transpose(0, 2, 1, 3)

    qh = heads(q) * (SB_HD ** -0.5)
    kh, vh = heads(k), heads(v)
    qb = qh.reshape(B, SB_HEADS, N, BLOCK, SB_HD).transpose(2, 0, 1, 3, 4)
    key_pos = jnp.arange(S)

    def one_block(args):
        q_blk, n = args
        z = jnp.einsum('bhid,bhjd->bhij', q_blk, kh)
        q_pos = n * BLOCK + jnp.arange(BLOCK)
        mask = key_pos[None, :] < q_pos[:, None]
        log_1m = jnp.where(mask, jax.nn.log_sigmoid(-z), 0.0)
        after = lax.cumsum(log_1m, axis=3, reverse=True) - log_1m
        w = jnp.where(mask, jnp.exp(jax.nn.log_sigmoid(z) + after), 0.0)
        return jnp.einsum('bhij,bhjd->bhid', w, vh)

    out = lax.map(one_block, (qb, jnp.arange(N)))
    return out.transpose(1, 0, 3, 2, 4).reshape(B, S, BRANCH_W).astype(dtype)


def spatial_gating(uv, ln_g, ln_b, w_s, b_s):
    dtype = uv.dtype
    B, S, _ = uv.shape
    N = S // BLOCK
    zz = jax.nn.gelu(uv.astype(jnp.float32))
    u, v = zz[..., :BRANCH_W], zz[..., BRANCH_W:]
    v = layer_norm(v, ln_g, ln_b).reshape(B, N, BLOCK, SG_GROUPS, SG_GD)
    tri = jnp.tril(jnp.ones((BLOCK, BLOCK), jnp.float32))
    w = w_s.astype(jnp.float32) * tri[None]
    s = jnp.einsum('gij,bnjgc->bnigc', w, v) + b_s.astype(jnp.float32).T[None, None, :, :, None]
    return (u * s.reshape(B, S, BRANCH_W)).astype(dtype)


def hybrid_layer(x, w_in, b_in, conv_w, conv_b, lru_wa, lru_ba, lru_wx, lru_bx, lru_lambda,
                 sg_ln_g, sg_ln_b, sg_ws, sg_bs, w_branch, w_out, ln1_g, ln1_b,
                 w1, b1, w2, b2, ln2_g, ln2_b):
    B, S, D = x.shape
    z = x @ w_in + b_in
    (rq, rk, rv, rg, lx, lg, sq, sk, sv, suv, gl) = jnp.split(z, SPLIT_POINTS, axis=-1)
    y_a = retention(rq, rk, rv, rg)
    y_b = rg_lru_branch(lx, lg, conv_w, conv_b, lru_wa, lru_ba, lru_wx, lru_bx, lru_lambda)
    y_c = stick_breaking(sq, sk, sv)
    y_d = spatial_gating(suv, sg_ln_g, sg_ln_b, sg_ws, sg_bs)
    ys = jnp.stack([y_a, y_b, y_c, y_d], axis=2)
    proj = jnp.einsum('bsnw,nwd->bsnd', ys, w_branch)
    gates = jax.nn.sigmoid(gl.astype(jnp.float32)).reshape(B, S, N_BRANCH, D)
    merged = jnp.sum(gates * proj.astype(jnp.float32), axis=2).astype(x.dtype)
    x = layer_norm(ALPHA * x + merged @ w_out, ln1_g, ln1_b)
    hid = jnp.square(jax.nn.relu(x @ w1 + b1))
    x = layer_norm(ALPHA * x + (hid @ w2 + b2), ln2_g, ln2_b)
    return x


def setup_inputs(seed: int = 0) -> dict:
    key = jax.random.key(seed)
    ks = jax.random.split(key, 24)
    L = DEPTH
    f32 = jnp.float32

    def nrm(k, shape, scale):
        return jax.random.normal(k, shape, f32) * scale

    u = jax.random.uniform(ks[9], (L, BRANCH_W), f32, 0.9, 0.999)
    s = u ** (1.0 / LRU_C)
    return {
        "x": nrm(ks[0], (BATCH, SEQ, D_MODEL), 1.0),
        "w_in": nrm(ks[1], (L, D_MODEL, D_IN), D_MODEL ** -0.5),
        "b_in": nrm(ks[2], (L, D_IN), 0.01),
        "conv_w": nrm(ks[3], (L, CONV_W, BRANCH_W), CONV_W ** -0.5),
        "conv_b": nrm(ks[4], (L, BRANCH_W), 0.01),
        "lru_wa": nrm(ks[5], (L, LRU_HEADS, LRU_HD, LRU_HD), LRU_HD ** -0.5),
        "lru_ba": nrm(ks[6], (L, BRANCH_W), 0.01),
        "lru_wx": nrm(ks[7], (L, LRU_HEADS, LRU_HD, LRU_HD), LRU_HD ** -0.5),
        "lru_bx": nrm(ks[8], (L, BRANCH_W), 0.01),
        "lru_lambda": jnp.log(s) - jnp.log1p(-s),
        "sg_ln_g": 1.0 + nrm(ks[10], (L, BRANCH_W), 0.01),
        "sg_ln_b": nrm(ks[11], (L, BRANCH_W), 0.01),
        "sg_ws": nrm(ks[12], (L, SG_GROUPS, BLOCK, BLOCK), BLOCK ** -0.5),
        "sg_bs": 1.0 + nrm(ks[13], (L, SG_GROUPS, BLOCK), 0.01),
        "w_branch": nrm(ks[14], (L, N_BRANCH, BRANCH_W, D_MODEL), BRANCH_W ** -0.5),
        "w_out": nrm(ks[15], (L, D_MODEL, D_MODEL), D_MODEL ** -0.5 * BETA),
        "ln1_g": 1.0 + nrm(ks[16], (L, D_MODEL), 0.01),
        "ln1_b": nrm(ks[17], (L, D_MODEL), 0.01),
        "w1": nrm(ks[18], (L, D_MODEL, D_FF), D_MODEL ** -0.5),
        "b1": nrm(ks[19], (L, D_FF), 0.01),
        "w2": nrm(ks[20], (L, D_FF, D_MODEL), D_FF ** -0.5 * BETA),
        "b2": nrm(ks[21], (L, D_MODEL), 0.01),
        "ln2_g": 1.0 + nrm(ks[22], (L, D_MODEL), 0.01),
        "ln2_b": nrm(ks[23], (L, D_MODEL), 0.01),
    }


def reference(x, w_in, b_in, conv_w, conv_b, lru_wa, lru_ba, lru_wx, lru_bx, lru_lambda,
              sg_ln_g, sg_ln_b, sg_ws, sg_bs, w_branch, w_out, ln1_g, ln1_b,
              w1, b1, w2, b2, ln2_g, ln2_b):
    for l in range(DEPTH):
        x = hybrid_layer(x, w_in[l], b_in[l], conv_w[l], conv_b[l], lru_wa[l], lru_ba[l],
                         lru_wx[l], lru_bx[l], lru_lambda[l], sg_ln_g[l], sg_ln_b[l],
                         sg_ws[l], sg_bs[l], w_branch[l], w_out[l], ln1_g[l], ln1_b[l],
                         w1[l], b1[l], w2[l], b2[l], ln2_g[l], ln2_b[l])
    return x
```

```python
import functools
import math

import jax
import jax.numpy as jnp
from jax import lax
from jax.experimental import pallas as pl
from jax.experimental.pallas import tpu as pltpu

F32 = jnp.float32
BF16 = jnp.bfloat16

D_MODEL = 1024
DEPTH = 2
BLOCK = 128
BRANCH_W = D_MODEL // 2
N_BRANCH = 4
RET_HEADS = 4
RET_DK = BRANCH_W // RET_HEADS
ROPE_THETA = 10000.0
LRU_HEADS = 8
LRU_HD = BRANCH_W // LRU_HEADS
CONV_W = 4
LRU_C = 8.0
SB_HEADS = 4
SB_HD = BRANCH_W // SB_HEADS
SG_GROUPS = 4
D_FF = 4 * D_MODEL
ALPHA = (2 * DEPTH) ** 0.25
LN_EPS = 1e-5

ZB_RET_V, ZB_SB_Q, ZB_SB_K, ZB_SB_V = 0, 1, 2, 3
ZB_COLS = 4 * BRANCH_W
ZF_RET_Q, ZF_RET_K, ZF_RET_G, ZF_LRU_X, ZF_LRU_G, ZF_SG_U, ZF_SG_V = 8, 9, 10, 11, 12, 13, 14
ZF_COLS = N_BRANCH * D_MODEL + 7 * BRANCH_W

VMEM_LIMIT = 56 * 1024 * 1024

PROJ_TM, PROJ_TN = 1024, 512
RET_ROWS = 2 * BLOCK
LRU_ROWS = 256
SG_ROWS = 2 * BLOCK
SB_TQ = 256
MERGE_TM = 256
MLP_TM = 512
MLP_FC = 1024


def _params(*sem):
    return pltpu.CompilerParams(dimension_semantics=sem, vmem_limit_bytes=VMEM_LIMIT)


def _dot(a, b):
    return jnp.dot(a, b, preferred_element_type=F32)


def _dot_nt(a, b):
    return lax.dot_general(a, b, (((1,), (1,)), ((), ())), preferred_element_type=F32)


def _layer_norm(h, g, b):
    mu = jnp.mean(h, axis=-1, keepdims=True)
    d = h - mu
    var = jnp.mean(d * d, axis=-1, keepdims=True)
    return d * lax.rsqrt(var + LN_EPS) * g + b


def _proj_kernel(x_ref, w_ref, b_ref, s_ref, o_ref, xb_ref):
    @pl.when(pl.program_id(1) == 0)
    def _():
        xb_ref[...] = x_ref[...].astype(BF16)

    acc = _dot(xb_ref[...], w_ref[...]) + b_ref[...]
    o_ref[...] = (acc * s_ref[...]).astype(o_ref.dtype)


def _project(x2, w, b, s, out_dtype):
    t, d = x2.shape
    n = w.shape[1]
    return pl.pallas_call(
        _proj_kernel,
        grid=(t // PROJ_TM, n // PROJ_TN),
        in_specs=[
            pl.BlockSpec((PROJ_TM, d), lambda i, j: (i, 0)),
            pl.BlockSpec((d, PROJ_TN), lambda i, j: (0, j)),
            pl.BlockSpec((1, PROJ_TN), lambda i, j: (0, j)),
            pl.BlockSpec((1, PROJ_TN), lambda i, j: (0, j)),
        ],
        out_specs=pl.BlockSpec((PROJ_TM, PROJ_TN), lambda i, j: (i, j)),
        out_shape=jax.ShapeDtypeStruct((t, n), out_dtype),
        scratch_shapes=[pltpu.VMEM((PROJ_TM, d), BF16)],
        compiler_params=_params("arbitrary", "arbitrary"),
        name="in_proj",
    )(x2, w, b, s)


def _ret_log_g(h):
    return math.log1p(-(2.0 ** (-5.0 - h)))


def _retention_kernel(q_ref, k_ref, g_ref, v_ref, cos_ref, sin_ref, o_ref,
                      state_ref, dmat_ref, qdec_ref, kdec_ref):
    @pl.when(pl.program_id(1) == 0)
    def _():
        state_ref[...] = jnp.zeros_like(state_ref)
        row = lax.broadcasted_iota(jnp.int32, (BLOCK, BLOCK), 0).astype(F32)
        col = lax.broadcasted_iota(jnp.int32, (BLOCK, BLOCK), 1).astype(F32)
        diff = row - col
        for h in range(RET_HEADS):
            lg = _ret_log_g(h)
            dmat_ref[h] = jnp.where(diff >= 0, jnp.exp(lg * jnp.maximum(diff, 0.0)), 0.0)
            qdec_ref[h] = jnp.exp(lg * (row + 1.0))
            kdec_ref[h] = jnp.exp(lg * (BLOCK - 1.0 - row))

    for c in range(RET_ROWS // BLOCK):
        rows = slice(c * BLOCK, (c + 1) * BLOCK)
        cs = cos_ref[rows, :]
        sn = sin_ref[rows, :]
        for h in range(RET_HEADS):
            cols = slice(h * RET_DK, (h + 1) * RET_DK)
            q = q_ref[rows, cols]
            k = k_ref[rows, cols]
            v = v_ref[rows, cols]
            qr = q * cs + pltpu.roll(q, RET_DK // 2, 1) * sn
            kr = (k * cs + pltpu.roll(k, RET_DK // 2, 1) * sn) * (RET_DK ** -0.5)
            scores = _dot_nt(qr.astype(BF16), kr.astype(BF16)) * dmat_ref[h]
            inner = _dot(scores.astype(BF16), v)
            st = state_ref[h]
            cross = _dot((qr * qdec_ref[h]).astype(BF16), st.astype(BF16))
            kd_t = jnp.transpose(kr * kdec_ref[h]).astype(BF16)
            state_ref[h] = math.exp(_ret_log_g(h) * BLOCK) * st + _dot(kd_t, v)
            y = inner + cross
            mu = jnp.mean(y, axis=-1, keepdims=True)
            d = y - mu
            var = jnp.mean(d * d, axis=-1, keepdims=True)
            yn = d * lax.rsqrt(var + LN_EPS)
            o_ref[rows, cols] = (jax.nn.silu(g_ref[rows, cols]) * yn).astype(o_ref.dtype)


def _retention(zf, zb, cos_t, sin_t, batch, seq):
    nblk = seq // RET_ROWS
    row_map = lambda b, n: b * nblk + n
    return pl.pallas_call(
        _retention_kernel,
        grid=(batch, nblk),
        in_specs=[
            pl.BlockSpec((RET_ROWS, BRANCH_W), lambda b, n: (row_map(b, n), ZF_RET_Q)),
            pl.BlockSpec((RET_ROWS, BRANCH_W), lambda b, n: (row_map(b, n), ZF_RET_K)),
            pl.BlockSpec((RET_ROWS, BRANCH_W), lambda b, n: (row_map(b, n), ZF_RET_G)),
            pl.BlockSpec((RET_ROWS, BRANCH_W), lambda b, n: (row_map(b, n), ZB_RET_V)),
            pl.BlockSpec((RET_ROWS, RET_DK), lambda b, n: (n, 0)),
            pl.BlockSpec((RET_ROWS, RET_DK), lambda b, n: (n, 0)),
        ],
        out_specs=pl.BlockSpec((RET_ROWS, BRANCH_W), lambda b, n: (row_map(b, n), 0)),
        out_shape=jax.ShapeDtypeStruct((batch * seq, BRANCH_W), BF16),
        scratch_shapes=[pltpu.VMEM((RET_HEADS, RET_DK, RET_DK), F32)] * 4,
        compiler_params=_params("arbitrary", "arbitrary"),
        name="retention",
    )(zf, zf, zf, zb, cos_t, sin_t)


def _lru_kernel(x_ref, gate_ref, cw_ref, cb_ref, wa_ref, ba_ref, wx_ref, bx_ref, lam_ref,
                o_ref, xbuf_ref, h_ref):
    rows = LRU_ROWS

    @pl.when(pl.program_id(1) == 0)
    def _():
        xbuf_ref[0:8, :] = jnp.zeros((8, BRANCH_W), F32)
        h_ref[...] = jnp.zeros_like(h_ref)

    xbuf_ref[8:8 + rows, :] = x_ref[...]
    xc = cb_ref[...]
    for tap in range(CONV_W):
        off = 8 - (CONV_W - 1) + tap
        xc = xc + cw_ref[tap:tap + 1, :] * xbuf_ref[off:off + rows, :]
    xbuf_ref[0:8, :] = xbuf_ref[rows:rows + 8, :]

    xcb = xc.astype(BF16)
    half = BRANCH_W // 2
    pre_a = jnp.concatenate([_dot(xcb[:, :half], wa_ref[0]), _dot(xcb[:, half:], wa_ref[1])], axis=1)
    pre_x = jnp.concatenate([_dot(xcb[:, :half], wx_ref[0]), _dot(xcb[:, half:], wx_ref[1])], axis=1)
    r = jax.nn.sigmoid(pre_a + ba_ref[...])
    i = jax.nn.sigmoid(pre_x + bx_ref[...])
    nlam = -lam_ref[...]
    softplus = jnp.maximum(nlam, 0.0) + jnp.log1p(jnp.exp(-jnp.abs(nlam)))
    log_a = -LRU_C * r * softplus
    a_all = jnp.exp(log_a)
    u_all = jnp.sqrt(jnp.tanh(-log_a) * (1.0 + a_all * a_all)) * (i * xc)

    row = lax.broadcasted_iota(jnp.int32, (rows, 128), 0)
    for s in range(BRANCH_W // 128):
        cols = slice(s * 128, (s + 1) * 128)
        a = a_all[:, cols]
        b = u_all[:, cols]
        shift = 1
        while shift < rows:
            keep = row >= shift
            a_sh = jnp.where(keep, pltpu.roll(a, shift, 0), 1.0)
            b_sh = jnp.where(keep, pltpu.roll(b, shift, 0), 0.0)
            b = a * b_sh + b
            a = a * a_sh
            shift *= 2
        h = a * h_ref[0:1, cols] + b
        h_ref[:, cols] = jnp.broadcast_to(h[rows - 1:rows, :], (8, 128))
        o_ref[:, cols] = (h * jax.nn.gelu(gate_ref[:, cols])).astype(o_ref.dtype)


def _rg_lru(zf, conv_w, conv_b, wa_bd, ba, wx_bd, bx, lam, batch, seq):
    nblk = seq // LRU_ROWS
    row_map = lambda b, n: b * nblk + n
    vec = pl.BlockSpec((1, BRANCH_W), lambda b, n: (0, 0))
    mat = pl.BlockSpec((2, BRANCH_W // 2, BRANCH_W // 2), lambda b, n: (0, 0, 0))
    return pl.pallas_call(
        _lru_kernel,
        grid=(batch, nblk),
        in_specs=[
            pl.BlockSpec((LRU_ROWS, BRANCH_W), lambda b, n: (row_map(b, n), ZF_LRU_X)),
            pl.BlockSpec((LRU_ROWS, BRANCH_W), lambda b, n: (row_map(b, n), ZF_LRU_G)),
            pl.BlockSpec((CONV_W, BRANCH_W), lambda b, n: (0, 0)),
            vec, mat, vec, mat, vec, vec,
        ],
        out_specs=pl.BlockSpec((LRU_ROWS, BRANCH_W), lambda b, n: (row_map(b, n), 0)),
        out_shape=jax.ShapeDtypeStruct((batch * seq, BRANCH_W), BF16),
        scratch_shapes=[pltpu.VMEM((LRU_ROWS + 8, BRANCH_W), F32), pltpu.VMEM((8, BRANCH_W), F32)],
        compiler_params=_params("arbitrary", "arbitrary"),
        name="rg_lru",
    )(zf, zf, conv_w, conv_b, wa_bd, ba, wx_bd, bx, lam)


def _sb_block(q, kb, vb, tri, carry, mask):
    z = _dot_nt(q, kb)
    lg = jnp.log(1.0 + jnp.exp(-jnp.abs(z)))
    log_1m = -jnp.maximum(z, 0.0) - lg
    log_p = jnp.minimum(z, 0.0) - lg
    if mask is not None:
        log_1m = jnp.where(mask, log_1m, 0.0)
    hi = log_1m.astype(BF16)
    lo = (log_1m - hi.astype(F32)).astype(BF16)
    after = _dot(hi, tri) + _dot(lo, tri)
    w = jnp.exp(log_p + after + jnp.concatenate([carry] * (SB_TQ // 128), axis=1))
    if mask is not None:
        w = jnp.where(mask, w, 0.0)
    pv = _dot(w.astype(BF16), vb)
    return pv, carry + jnp.sum(log_1m, axis=1, keepdims=True)


def _sb_kernel(q_ref, k_ref, v_ref, o_ref, acc_ref, carry_ref):
    i = pl.program_id(2)
    t = SB_TQ
    row = lax.broadcasted_iota(jnp.int32, (t, t), 0)
    col = lax.broadcasted_iota(jnp.int32, (t, t), 1)
    tri = (row > col).astype(BF16)
    q = q_ref[...]

    start = pl.multiple_of(i * t, t)
    pv, carry = _sb_block(q, k_ref[pl.ds(start, t), :], v_ref[pl.ds(start, t), :], tri,
                          jnp.zeros((t, 128), F32), col < row)
    acc_ref[...] = pv
    carry_ref[...] = carry

    def body(step, _):
        off = pl.multiple_of((i - 1 - step) * t, t)
        pv, carry = _sb_block(q, k_ref[pl.ds(off, t), :], v_ref[pl.ds(off, t), :], tri,
                              carry_ref[...], None)
        acc_ref[...] += pv
        carry_ref[...] = carry
        return 0

    lax.fori_loop(0, i, body, 0)
    o_ref[...] = acc_ref[...].astype(o_ref.dtype)


def _stick_breaking(zb, batch, seq):
    nq = seq // SB_TQ
    zb3 = zb.reshape(batch, seq, ZB_COLS)
    hb = BRANCH_W // SB_HD
    return pl.pallas_call(
        _sb_kernel,
        grid=(batch, SB_HEADS, nq),
        in_specs=[
            pl.BlockSpec((None, SB_TQ, SB_HD), lambda b, h, i: (b, i, ZB_SB_Q * hb + h)),
            pl.BlockSpec((None, seq, SB_HD), lambda b, h, i: (b, 0, ZB_SB_K * hb + h)),
            pl.BlockSpec((None, seq, SB_HD), lambda b, h, i: (b, 0, ZB_SB_V * hb + h)),
        ],
        out_specs=pl.BlockSpec((None, SB_TQ, SB_HD), lambda b, h, i: (b, i, h)),
        out_shape=jax.ShapeDtypeStruct((batch, seq, BRANCH_W), BF16),
        scratch_shapes=[pltpu.VMEM((SB_TQ, SB_HD), F32), pltpu.VMEM((SB_TQ, 128), F32)],
        compiler_params=_params("arbitrary", "arbitrary", "arbitrary"),
        name="stick_breaking",
    )(zb3, zb3, zb3).reshape(batch * seq, BRANCH_W)


def _sg_kernel(u_ref, v_ref, g_ref, b_ref, ws_ref, bs_ref, o_ref):
    u = jax.nn.gelu(u_ref[...])
    v = _layer_norm(jax.nn.gelu(v_ref[...]), g_ref[...], b_ref[...])
    row = lax.broadcasted_iota(jnp.int32, (BLOCK, BLOCK), 0)
    col = lax.broadcasted_iota(jnp.int32, (BLOCK, BLOCK), 1)
    gd = BRANCH_W // SG_GROUPS
    for g in range(SG_GROUPS):
        w = jnp.where(col <= row, ws_ref[g], 0.0).astype(BF16)
        cols = slice(g * gd, (g + 1) * gd)
        for c in range(SG_ROWS // BLOCK):
            rows = slice(c * BLOCK, (c + 1) * BLOCK)
            s = _dot(w, v[rows, cols].astype(BF16)) + bs_ref[g]
            o_ref[rows, cols] = (u[rows, cols] * s).astype(o_ref.dtype)


def _spatial_gating(zf, ln_g, ln_b, ws, bs_b):
    t = zf.shape[0]
    vec = pl.BlockSpec((1, BRANCH_W), lambda n: (0, 0))
    cube = pl.BlockSpec((SG_GROUPS, BLOCK, BLOCK), lambda n: (0, 0, 0))
    return pl.pallas_call(
        _sg_kernel,
        grid=(t // SG_ROWS,),
        in_specs=[
            pl.BlockSpec((SG_ROWS, BRANCH_W), lambda n: (n, ZF_SG_U)),
            pl.BlockSpec((SG_ROWS, BRANCH_W), lambda n: (n, ZF_SG_V)),
            vec, vec, cube, cube,
        ],
        out_specs=pl.BlockSpec((SG_ROWS, BRANCH_W), lambda n: (n, 0)),
        out_shape=jax.ShapeDtypeStruct((t, BRANCH_W), BF16),
        compiler_params=_params("arbitrary"),
        name="spatial_gating",
    )(zf, zf, ln_g, ln_b, ws, bs_b)


def _merge_kernel(ya_ref, yb_ref, yc_ref, yd_ref, g0_ref, g1_ref, g2_ref, g3_ref, x_ref,
                  wb_ref, wo_ref, lg_ref, lb_ref, o_ref):
    merged = None
    for n, (y_ref, gl_ref) in enumerate(((ya_ref, g0_ref), (yb_ref, g1_ref),
                                         (yc_ref, g2_ref), (yd_ref, g3_ref))):
        term = jax.nn.sigmoid(gl_ref[...]) * _dot(y_ref[...], wb_ref[n])
        merged = term if merged is None else merged + term
    out = _dot(merged.astype(BF16), wo_ref[...])
    o_ref[...] = _layer_norm(ALPHA * x_ref[...] + out, lg_ref[...], lb_ref[...])


def _merge(ys, zf, x2, w_branch, w_out, ln_g, ln_b):
    t = x2.shape[0]
    ysp = pl.BlockSpec((MERGE_TM, BRANCH_W), lambda i: (i, 0))
    gates = [pl.BlockSpec((MERGE_TM, D_MODEL), functools.partial(lambda i, n: (i, n), n=n))
             for n in range(N_BRANCH)]
    vec = pl.BlockSpec((1, D_MODEL), lambda i: (0, 0))
    return pl.pallas_call(
        _merge_kernel,
        grid=(t // MERGE_TM,),
        in_specs=[ysp] * 4 + gates + [
            pl.BlockSpec((MERGE_TM, D_MODEL), lambda i: (i, 0)),
            pl.BlockSpec((N_BRANCH, BRANCH_W, D_MODEL), lambda i: (0, 0, 0)),
            pl.BlockSpec((D_MODEL, D_MODEL), lambda i: (0, 0)),
            vec, vec,
        ],
        out_specs=pl.BlockSpec((MERGE_TM, D_MODEL), lambda i: (i, 0)),
        out_shape=jax.ShapeDtypeStruct((t, D_MODEL), F32),
        compiler_params=_params("arbitrary"),
        name="merge_out",
    )(*ys, zf, zf, zf, zf, x2, w_branch, w_out, ln_g, ln_b)


def _mlp_kernel(x_ref, w1_ref, b1_ref, w2_ref, b2_ref, lg_ref, lb_ref, o_ref):
    x = x_ref[...]
    xb = x.astype(BF16)
    acc = None
    for c in range(D_FF // MLP_FC):
        cols = slice(c * MLP_FC, (c + 1) * MLP_FC)
        hid = jnp.maximum(_dot(xb, w1_ref[:, cols]) + b1_ref[:, cols], 0.0)
        part = _dot((hid * hid).astype(BF16), w2_ref[cols, :])
        acc = part if acc is None else acc + part
    o_ref[...] = _layer_norm(ALPHA * x + (acc + b2_ref[...]), lg_ref[...], lb_ref[...])


def _mlp(x2, w1, b1, w2, b2, ln_g, ln_b):
    t = x2.shape[0]
    vec = pl.BlockSpec((1, D_MODEL), lambda i: (0, 0))
    return pl.pallas_call(
        _mlp_kernel,
        grid=(t // MLP_TM,),
        in_specs=[
            pl.BlockSpec((MLP_TM, D_MODEL), lambda i: (i, 0)),
            pl.BlockSpec((D_MODEL, D_FF), lambda i: (0, 0), pipeline_mode=pl.Buffered(1)),
            pl.BlockSpec((1, D_FF), lambda i: (0, 0)),
            pl.BlockSpec((D_FF, D_MODEL), lambda i: (0, 0), pipeline_mode=pl.Buffered(1)),
            vec, vec, vec,
        ],
        out_specs=pl.BlockSpec((MLP_TM, D_MODEL), lambda i: (i, 0)),
        out_shape=jax.ShapeDtypeStruct((t, D_MODEL), F32),
        compiler_params=_params("arbitrary"),
        name="mlp",
    )(x2, w1, b1, w2, b2, ln_g, ln_b)


def _block_diag(w):
    per = LRU_HEADS // 2
    w4 = w.reshape(2, per, LRU_HD, LRU_HD)
    eye = jnp.eye(per, dtype=w.dtype)
    return jnp.einsum('ghij,hk->ghikj', w4, eye).reshape(2, per * LRU_HD, per * LRU_HD)


def _rope_tables(seq):
    half = RET_DK // 2
    pos = jnp.arange(seq, dtype=F32)
    inv_freq = ROPE_THETA ** (-jnp.arange(half, dtype=F32) / half)
    ang = pos[:, None] * inv_freq[None, :]
    cos, sin = jnp.cos(ang), jnp.sin(ang)
    return jnp.concatenate([cos, cos], axis=-1), jnp.concatenate([-sin, sin], axis=-1)


def _layer(x2, batch, seq, cos_t, sin_t, w_in, b_in, conv_w, conv_b, lru_wa, lru_ba, lru_wx,
           lru_bx, lru_lambda, sg_ln_g, sg_ln_b, sg_ws, sg_bs, w_branch, w_out, ln1_g, ln1_b,
           w1, b1, w2, b2, ln2_g, ln2_b):
    bw = BRANCH_W
    parts = [w_in[:, p * bw:(p + 1) * bw] for p in range(9)]
    bparts = [b_in[p * bw:(p + 1) * bw] for p in range(9)]
    rq, rk, rv, rg, lx, lg, sq, sk, sv = range(9)
    uv0 = 9 * bw
    gl0 = uv0 + 2 * bw
    wb_cols = jnp.concatenate([parts[rv], parts[sq], parts[sk], parts[sv]], axis=1).astype(BF16)
    bb_cols = jnp.concatenate([bparts[rv], bparts[sq], bparts[sk], bparts[sv]])[None, :]
    sb_cols = jnp.concatenate([jnp.ones((bw,), F32), jnp.full((bw,), SB_HD ** -0.5, F32),
                               jnp.ones((2 * bw,), F32)])[None, :]
    wf_cols = jnp.concatenate([w_in[:, gl0:], parts[rq], parts[rk], parts[rg], parts[lx],
                               parts[lg], w_in[:, uv0:gl0]], axis=1).astype(BF16)
    bf_cols = jnp.concatenate([b_in[gl0:], bparts[rq], bparts[rk], bparts[rg], bparts[lx],
                               bparts[lg], b_in[uv0:gl0]])[None, :]
    zb = _project(x2, wb_cols, bb_cols, sb_cols, BF16)
    zf = _project(x2, wf_cols, bf_cols, jnp.ones((1, ZF_COLS), F32), F32)

    y_a = _retention(zf, zb, cos_t, sin_t, batch, seq)
    y_b = _rg_lru(zf, conv_w, conv_b[None, :], _block_diag(lru_wa).astype(BF16), lru_ba[None, :],
                  _block_diag(lru_wx).astype(BF16), lru_bx[None, :], lru_lambda[None, :], batch, seq)
    y_c = _stick_breaking(zb, batch, seq)
    bs_b = jnp.broadcast_to(sg_bs[:, :, None], (SG_GROUPS, BLOCK, BRANCH_W // SG_GROUPS))
    y_d = _spatial_gating(zf, sg_ln_g[None, :], sg_ln_b[None, :], sg_ws, bs_b)

    x1 = _merge((y_a, y_b, y_c, y_d), zf, x2, w_branch.astype(BF16), w_out.astype(BF16),
                ln1_g[None, :], ln1_b[None, :])
    return _mlp(x1, w1.astype(BF16), b1[None, :], w2.astype(BF16), b2[None, :],
                ln2_g[None, :], ln2_b[None, :])


def kernel(x, w_in, b_in, conv_w, conv_b, lru_wa, lru_ba, lru_wx, lru_bx, lru_lambda, sg_ln_g, sg_ln_b, sg_ws, sg_bs, w_branch, w_out, ln1_g, ln1_b, w1, b1, w2, b2, ln2_g, ln2_b):
    batch, seq, d = x.shape
    cos_t, sin_t = _rope_tables(seq)
    x2 = x.reshape(batch * seq, d)
    per_layer = (w_in, b_in, conv_w, conv_b, lru_wa, lru_ba, lru_wx, lru_bx, lru_lambda, sg_ln_g,
                 sg_ln_b, sg_ws, sg_bs, w_branch, w_out, ln1_g, ln1_b, w1, b1, w2, b2, ln2_g, ln2_b)
    for l in range(DEPTH):
        x2 = _layer(x2, batch, seq, cos_t, sin_t, *(p[l] for p in per_layer))
    return x2.reshape(batch, seq, d)
```

```python
import functools
import math

import jax
import jax.numpy as jnp
from jax import lax
from jax.experimental import pallas as pl
from jax.experimental.pallas import tpu as pltpu

F32 = jnp.float32
BF16 = jnp.bfloat16

D_MODEL = 1024
DEPTH = 2
BLOCK = 128
BRANCH_W = D_MODEL // 2
N_BRANCH = 4
RET_HEADS = 4
RET_DK = BRANCH_W // RET_HEADS
ROPE_THETA = 10000.0
LRU_HEADS = 8
LRU_HD = BRANCH_W // LRU_HEADS
CONV_W = 4
LRU_C = 8.0
SB_HEADS = 4
SB_HD = BRANCH_W // SB_HEADS
SG_GROUPS = 4
D_FF = 4 * D_MODEL
ALPHA = (2 * DEPTH) ** 0.25
LN_EPS = 1e-5
LOG2E = 1.4426950408889634

ZB_RET_V, ZB_SB_Q, ZB_SB_K, ZB_SB_V = 0, 1, 2, 3
ZB_COLS = 4 * BRANCH_W
ZF_RET_Q, ZF_RET_K, ZF_RET_G, ZF_LRU_X, ZF_LRU_G, ZF_SG_U, ZF_SG_V = 0, 1, 2, 3, 4, 5, 6
ZF_COLS = 7 * BRANCH_W

VMEM_LIMIT = 56 * 1024 * 1024

PROJ_TM, PROJ_TN = 512, 512
RET_ROWS = 2 * BLOCK
LRU_ROWS = 256
SG_ROWS = 2 * BLOCK
SB_TQ = 256
MERGE_TM = 512
MLP_TM = 512
MLP_FC = 1024


def _params(*sem):
    return pltpu.CompilerParams(dimension_semantics=sem, vmem_limit_bytes=VMEM_LIMIT)


def _dot(a, b):
    return jnp.dot(a, b, preferred_element_type=F32)


def _dot_nt(a, b):
    return lax.dot_general(a, b, (((1,), (1,)), ((), ())), preferred_element_type=F32)


def _layer_norm(h, g, b):
    mu = jnp.mean(h, axis=-1, keepdims=True)
    d = h - mu
    var = jnp.mean(d * d, axis=-1, keepdims=True)
    return d * lax.rsqrt(var + LN_EPS) * g + b


def _proj_kernel(x_ref, w_ref, b_ref, zb_ref, zf_ref):
    xb = x_ref[...].astype(BF16)
    for c in range(ZB_COLS // PROJ_TN):
        cols = slice(c * PROJ_TN, (c + 1) * PROJ_TN)
        acc = _dot(xb, w_ref[:, cols]) + b_ref[:, cols]
        if cols.start // BRANCH_W == ZB_SB_Q:
            acc = acc * (SB_HD ** -0.5)
        zb_ref[:, cols] = acc.astype(BF16)
    for c in range(ZF_COLS // PROJ_TN):
        cols = slice(c * PROJ_TN, (c + 1) * PROJ_TN)
        wcols = slice(ZB_COLS + cols.start, ZB_COLS + cols.stop)
        zf_ref[:, cols] = _dot(xb, w_ref[:, wcols]) + b_ref[:, wcols]


def _project(x2, w, b):
    t, d = x2.shape
    n = ZB_COLS + ZF_COLS
    whole = functools.partial(pl.BlockSpec, pipeline_mode=pl.Buffered(1))
    return pl.pallas_call(
        _proj_kernel,
        grid=(t // PROJ_TM,),
        in_specs=[
            pl.BlockSpec((PROJ_TM, d), lambda i: (i, 0)),
            whole((d, n), lambda i: (0, 0)),
            pl.BlockSpec((1, n), lambda i: (0, 0)),
        ],
        out_specs=[pl.BlockSpec((PROJ_TM, ZB_COLS), lambda i: (i, 0)),
                   pl.BlockSpec((PROJ_TM, ZF_COLS), lambda i: (i, 0))],
        out_shape=[jax.ShapeDtypeStruct((t, ZB_COLS), BF16), jax.ShapeDtypeStruct((t, ZF_COLS), F32)],
        compiler_params=_params("arbitrary"),
        name="in_proj",
    )(x2, w, b)


def _ret_log_g(h):
    return math.log1p(-(2.0 ** (-5.0 - h)))


def _retention_kernel(q_ref, k_ref, g_ref, v_ref, cos_ref, sin_ref, o_ref,
                      state_ref, dmat_ref, qdec_ref, kdec_ref):
    @pl.when(pl.program_id(1) == 0)
    def _():
        state_ref[...] = jnp.zeros_like(state_ref)
        row = lax.broadcasted_iota(jnp.int32, (BLOCK, BLOCK), 0).astype(F32)
        col = lax.broadcasted_iota(jnp.int32, (BLOCK, BLOCK), 1).astype(F32)
        diff = row - col
        for h in range(RET_HEADS):
            lg = _ret_log_g(h)
            dmat_ref[h] = jnp.where(diff >= 0, jnp.exp(lg * jnp.maximum(diff, 0.0)), 0.0)
            qdec_ref[h] = jnp.exp(lg * (row + 1.0))
            kdec_ref[h] = jnp.exp(lg * (BLOCK - 1.0 - row))

    for c in range(RET_ROWS // BLOCK):
        rows = slice(c * BLOCK, (c + 1) * BLOCK)
        cs = cos_ref[rows, :]
        sn = sin_ref[rows, :]
        for h in range(RET_HEADS):
            cols = slice(h * RET_DK, (h + 1) * RET_DK)
            q = q_ref[rows, cols]
            k = k_ref[rows, cols]
            v = v_ref[rows, cols]
            qr = q * cs + pltpu.roll(q, RET_DK // 2, 1) * sn
            kr = (k * cs + pltpu.roll(k, RET_DK // 2, 1) * sn) * (RET_DK ** -0.5)
            scores = _dot_nt(qr.astype(BF16), kr.astype(BF16)) * dmat_ref[h]
            inner = _dot(scores.astype(BF16), v)
            st = state_ref[h]
            cross = _dot((qr * qdec_ref[h]).astype(BF16), st.astype(BF16))
            kd_t = jnp.transpose(kr * kdec_ref[h]).astype(BF16)
            state_ref[h] = math.exp(_ret_log_g(h) * BLOCK) * st + _dot(kd_t, v)
            y = inner + cross
            mu = jnp.mean(y, axis=-1, keepdims=True)
            d = y - mu
            var = jnp.mean(d * d, axis=-1, keepdims=True)
            yn = d * lax.rsqrt(var + LN_EPS)
            o_ref[rows, cols] = (jax.nn.silu(g_ref[rows, cols]) * yn).astype(o_ref.dtype)


def _retention(zf, zb, cos_t, sin_t, batch, seq):
    nblk = seq // RET_ROWS
    row_map = lambda b, n: b * nblk + n
    return pl.pallas_call(
        _retention_kernel,
        grid=(batch, nblk),
        in_specs=[
            pl.BlockSpec((RET_ROWS, BRANCH_W), lambda b, n: (row_map(b, n), ZF_RET_Q)),
            pl.BlockSpec((RET_ROWS, BRANCH_W), lambda b, n: (row_map(b, n), ZF_RET_K)),
            pl.BlockSpec((RET_ROWS, BRANCH_W), lambda b, n: (row_map(b, n), ZF_RET_G)),
            pl.BlockSpec((RET_ROWS, BRANCH_W), lambda b, n: (row_map(b, n), ZB_RET_V)),
            pl.BlockSpec((RET_ROWS, RET_DK), lambda b, n: (n, 0)),
            pl.BlockSpec((RET_ROWS, RET_DK), lambda b, n: (n, 0)),
        ],
        out_specs=pl.BlockSpec((RET_ROWS, BRANCH_W), lambda b, n: (row_map(b, n), 0)),
        out_shape=jax.ShapeDtypeStruct((batch * seq, BRANCH_W), BF16),
        scratch_shapes=[pltpu.VMEM((RET_HEADS, RET_DK, RET_DK), F32)] * 4,
        compiler_params=_params("arbitrary", "arbitrary"),
        name="retention",
    )(zf, zf, zf, zb, cos_t, sin_t)


def _lru_kernel(x_ref, gate_ref, cw_ref, cb_ref, wa_ref, ba_ref, wx_ref, bx_ref, lam_ref,
                o_ref, xbuf_ref, h_ref):
    rows = LRU_ROWS

    @pl.when(pl.program_id(1) == 0)
    def _():
        xbuf_ref[0:8, :] = jnp.zeros((8, BRANCH_W), F32)
        h_ref[...] = jnp.zeros_like(h_ref)

    xbuf_ref[8:8 + rows, :] = x_ref[...]
    xc = cb_ref[...]
    for tap in range(CONV_W):
        off = 8 - (CONV_W - 1) + tap
        xc = xc + cw_ref[tap:tap + 1, :] * xbuf_ref[off:off + rows, :]
    xbuf_ref[0:8, :] = xbuf_ref[rows:rows + 8, :]

    xcb = xc.astype(BF16)
    half = BRANCH_W // 2
    pre_a = jnp.concatenate([_dot(xcb[:, :half], wa_ref[0]), _dot(xcb[:, half:], wa_ref[1])], axis=1)
    pre_x = jnp.concatenate([_dot(xcb[:, :half], wx_ref[0]), _dot(xcb[:, half:], wx_ref[1])], axis=1)
    r = jax.nn.sigmoid(pre_a + ba_ref[...])
    i = jax.nn.sigmoid(pre_x + bx_ref[...])
    nlam = -lam_ref[...]
    softplus = jnp.maximum(nlam, 0.0) + jnp.log1p(jnp.exp(-jnp.abs(nlam)))
    log_a = -LRU_C * r * softplus
    a_all = jnp.exp(log_a)
    u_all = jnp.sqrt(jnp.tanh(-log_a) * (1.0 + a_all * a_all)) * (i * xc)

    row = lax.broadcasted_iota(jnp.int32, (rows, 128), 0)
    for s in range(BRANCH_W // 128):
        cols = slice(s * 128, (s + 1) * 128)
        a = a_all[:, cols]
        b = u_all[:, cols]
        shift = 1
        while shift < rows:
            keep = row >= shift
            a_sh = jnp.where(keep, pltpu.roll(a, shift, 0), 1.0)
            b_sh = jnp.where(keep, pltpu.roll(b, shift, 0), 0.0)
            b = a * b_sh + b
            a = a * a_sh
            shift *= 2
        h = a * h_ref[0:1, cols] + b
        h_ref[:, cols] = jnp.broadcast_to(h[rows - 1:rows, :], (8, 128))
        o_ref[:, cols] = (h * jax.nn.gelu(gate_ref[:, cols])).astype(o_ref.dtype)


def _rg_lru(zf, conv_w, conv_b, wa_bd, ba, wx_bd, bx, lam, batch, seq):
    nblk = seq // LRU_ROWS
    row_map = lambda b, n: b * nblk + n
    vec = pl.BlockSpec((1, BRANCH_W), lambda b, n: (0, 0))
    mat = pl.BlockSpec((2, BRANCH_W // 2, BRANCH_W // 2), lambda b, n: (0, 0, 0))
    return pl.pallas_call(
        _lru_kernel,
        grid=(batch, nblk),
        in_specs=[
            pl.BlockSpec((LRU_ROWS, BRANCH_W), lambda b, n: (row_map(b, n), ZF_LRU_X)),
            pl.BlockSpec((LRU_ROWS, BRANCH_W), lambda b, n: (row_map(b, n), ZF_LRU_G)),
            pl.BlockSpec((CONV_W, BRANCH_W), lambda b, n: (0, 0)),
            vec, mat, vec, mat, vec, vec,
        ],
        out_specs=pl.BlockSpec((LRU_ROWS, BRANCH_W), lambda b, n: (row_map(b, n), 0)),
        out_shape=jax.ShapeDtypeStruct((batch * seq, BRANCH_W), BF16),
        scratch_shapes=[pltpu.VMEM((LRU_ROWS + 8, BRANCH_W), F32), pltpu.VMEM((8, BRANCH_W), F32)],
        compiler_params=_params("arbitrary", "arbitrary"),
        name="rg_lru",
    )(zf, zf, conv_w, conv_b, wa_bd, ba, wx_bd, bx, lam)


def _sb_key_block(q_ref, k_ref, v_ref, acc_ref, carry_ref, tri, off, mask):
    t = SB_TQ
    heads = range(SB_HEADS)
    cols = [slice(h * SB_HD, (h + 1) * SB_HD) for h in heads]
    zs = [_dot_nt(q_ref[:, cols[h]], k_ref[pl.ds(off, t), cols[h]]) for h in heads]
    cums = []
    for h in heads:
        z = zs[h]
        lg = jnp.log(1.0 + jnp.exp2(jnp.abs(z) * (-LOG2E)))
        log_1m = jnp.minimum(-z, 0.0) - lg
        if mask is not None:
            log_1m = jnp.where(mask, log_1m, 0.0)
        hi = log_1m.astype(BF16)
        lo = (log_1m - hi.astype(F32)).astype(BF16)
        cums.append(_dot(hi, tri) + _dot(lo, tri))
    for h in heads:
        if mask is None:
            carry = carry_ref[h]
            w = jnp.exp(zs[h] + cums[h] + jnp.concatenate([carry] * (t // 128), axis=1))
            carry_ref[h] = carry + cums[h][:, 0:1]
            acc_ref[:, cols[h]] += _dot(w.astype(BF16), v_ref[pl.ds(off, t), cols[h]])
        else:
            w = jnp.where(mask, jnp.exp(zs[h] + cums[h]), 0.0)
            carry_ref[h] = jnp.broadcast_to(cums[h][:, 0:1], (t, 128))
            acc_ref[:, cols[h]] = _dot(w.astype(BF16), v_ref[pl.ds(off, t), cols[h]])


def _sb_kernel(q_ref, k_ref, v_ref, o_ref, acc_ref, carry_ref):
    i = pl.program_id(1)
    t = SB_TQ
    row = lax.broadcasted_iota(jnp.int32, (t, t), 0)
    col = lax.broadcasted_iota(jnp.int32, (t, t), 1)
    tri = (row >= col).astype(BF16)

    _sb_key_block(q_ref, k_ref, v_ref, acc_ref, carry_ref, tri, pl.multiple_of(i * t, t), col < row)

    def body(step, _):
        off = pl.multiple_of((i - 1 - step) * t, t)
        _sb_key_block(q_ref, k_ref, v_ref, acc_ref, carry_ref, tri, off, None)
        return 0

    lax.fori_loop(0, i, body, 0)
    o_ref[...] = acc_ref[...].astype(o_ref.dtype)


def _stick_breaking(zb, batch, seq):
    nq = seq // SB_TQ
    zb3 = zb.reshape(batch, seq, ZB_COLS)
    whole = functools.partial(pl.BlockSpec, pipeline_mode=pl.Buffered(1))
    return pl.pallas_call(
        _sb_kernel,
        grid=(batch, nq),
        in_specs=[
            pl.BlockSpec((None, SB_TQ, BRANCH_W), lambda b, i: (b, i, ZB_SB_Q)),
            whole((None, seq, BRANCH_W), lambda b, i: (b, 0, ZB_SB_K)),
            whole((None, seq, BRANCH_W), lambda b, i: (b, 0, ZB_SB_V)),
        ],
        out_specs=pl.BlockSpec((None, SB_TQ, BRANCH_W), lambda b, i: (b, i, 0)),
        out_shape=jax.ShapeDtypeStruct((batch, seq, BRANCH_W), BF16),
        scratch_shapes=[pltpu.VMEM((SB_TQ, BRANCH_W), F32), pltpu.VMEM((SB_HEADS, SB_TQ, 128), F32)],
        compiler_params=_params("arbitrary", "arbitrary"),
        name="stick_breaking",
    )(zb3, zb3, zb3).reshape(batch * seq, BRANCH_W)


def _sg_kernel(u_ref, v_ref, g_ref, b_ref, ws_ref, bs_ref, o_ref):
    u = jax.nn.gelu(u_ref[...])
    v = _layer_norm(jax.nn.gelu(v_ref[...]), g_ref[...], b_ref[...])
    row = lax.broadcasted_iota(jnp.int32, (BLOCK, BLOCK), 0)
    col = lax.broadcasted_iota(jnp.int32, (BLOCK, BLOCK), 1)
    gd = BRANCH_W // SG_GROUPS
    for g in range(SG_GROUPS):
        w = jnp.where(col <= row, ws_ref[g], 0.0).astype(BF16)
        cols = slice(g * gd, (g + 1) * gd)
        for c in range(SG_ROWS // BLOCK):
            rows = slice(c * BLOCK, (c + 1) * BLOCK)
            s = _dot(w, v[rows, cols].astype(BF16)) + bs_ref[g]
            o_ref[rows, cols] = (u[rows, cols] * s).astype(o_ref.dtype)


def _spatial_gating(zf, ln_g, ln_b, ws, bs_b):
    t = zf.shape[0]
    vec = pl.BlockSpec((1, BRANCH_W), lambda n: (0, 0))
    cube = pl.BlockSpec((SG_GROUPS, BLOCK, BLOCK), lambda n: (0, 0, 0))
    return pl.pallas_call(
        _sg_kernel,
        grid=(t // SG_ROWS,),
        in_specs=[
            pl.BlockSpec((SG_ROWS, BRANCH_W), lambda n: (n, ZF_SG_U)),
            pl.BlockSpec((SG_ROWS, BRANCH_W), lambda n: (n, ZF_SG_V)),
            vec, vec, cube, cube,
        ],
        out_specs=pl.BlockSpec((SG_ROWS, BRANCH_W), lambda n: (n, 0)),
        out_shape=jax.ShapeDtypeStruct((t, BRANCH_W), BF16),
        compiler_params=_params("arbitrary"),
        name="spatial_gating",
    )(zf, zf, ln_g, ln_b, ws, bs_b)


def _merge_kernel(ya_ref, yb_ref, yc_ref, yd_ref, x_ref, wg_ref, bg_ref, wb_ref, wo_ref,
                  lg_ref, lb_ref, o_ref):
    x = x_ref[...]
    xb = x.astype(BF16)
    merged = None
    for n, y_ref in enumerate((ya_ref, yb_ref, yc_ref, yd_ref)):
        cols = slice(n * D_MODEL, (n + 1) * D_MODEL)
        gate = jax.nn.sigmoid(_dot(xb, wg_ref[:, cols]) + bg_ref[:, cols])
        term = gate * _dot(y_ref[...], wb_ref[n])
        merged = term if merged is None else merged + term
    out = _dot(merged.astype(BF16), wo_ref[...])
    o_ref[...] = _layer_norm(ALPHA * x + out, lg_ref[...], lb_ref[...])


def _merge(ys, x2, w_gate, b_gate, w_branch, w_out, ln_g, ln_b):
    t = x2.shape[0]
    ysp = pl.BlockSpec((MERGE_TM, BRANCH_W), lambda i: (i, 0))
    vec = pl.BlockSpec((1, D_MODEL), lambda i: (0, 0))
    whole = functools.partial(pl.BlockSpec, pipeline_mode=pl.Buffered(1))
    return pl.pallas_call(
        _merge_kernel,
        grid=(t // MERGE_TM,),
        in_specs=[ysp] * 4 + [
            pl.BlockSpec((MERGE_TM, D_MODEL), lambda i: (i, 0)),
            whole((D_MODEL, N_BRANCH * D_MODEL), lambda i: (0, 0)),
            pl.BlockSpec((1, N_BRANCH * D_MODEL), lambda i: (0, 0)),
            whole((N_BRANCH, BRANCH_W, D_MODEL), lambda i: (0, 0, 0)),
            whole((D_MODEL, D_MODEL), lambda i: (0, 0)),
            vec, vec,
        ],
        out_specs=pl.BlockSpec((MERGE_TM, D_MODEL), lambda i: (i, 0)),
        out_shape=jax.ShapeDtypeStruct((t, D_MODEL), F32),
        compiler_params=_params("arbitrary"),
        name="merge_out",
    )(*ys, x2, w_gate, b_gate, w_branch, w_out, ln_g, ln_b)


def _mlp_kernel(x_ref, w1_ref, b1_ref, w2_ref, b2_ref, lg_ref, lb_ref, o_ref):
    x = x_ref[...]
    xb = x.astype(BF16)
    acc = None
    for c in range(D_FF // MLP_FC):
        cols = slice(c * MLP_FC, (c + 1) * MLP_FC)
        hid = jnp.maximum(_dot(xb, w1_ref[:, cols]) + b1_ref[:, cols], 0.0)
        part = _dot((hid * hid).astype(BF16), w2_ref[cols, :])
        acc = part if acc is None else acc + part
    o_ref[...] = _layer_norm(ALPHA * x + (acc + b2_ref[...]), lg_ref[...], lb_ref[...])


def _mlp(x2, w1, b1, w2, b2, ln_g, ln_b):
    t = x2.shape[0]
    vec = pl.BlockSpec((1, D_MODEL), lambda i: (0, 0))
    return pl.pallas_call(
        _mlp_kernel,
        grid=(t // MLP_TM,),
        in_specs=[
            pl.BlockSpec((MLP_TM, D_MODEL), lambda i: (i, 0)),
            pl.BlockSpec((D_MODEL, D_FF), lambda i: (0, 0), pipeline_mode=pl.Buffered(1)),
            pl.BlockSpec((1, D_FF), lambda i: (0, 0)),
            pl.BlockSpec((D_FF, D_MODEL), lambda i: (0, 0), pipeline_mode=pl.Buffered(1)),
            vec, vec, vec,
        ],
        out_specs=pl.BlockSpec((MLP_TM, D_MODEL), lambda i: (i, 0)),
        out_shape=jax.ShapeDtypeStruct((t, D_MODEL), F32),
        compiler_params=_params("arbitrary"),
        name="mlp",
    )(x2, w1, b1, w2, b2, ln_g, ln_b)


def _block_diag(w):
    per = LRU_HEADS // 2
    w4 = w.reshape(2, per, LRU_HD, LRU_HD)
    eye = jnp.eye(per, dtype=w.dtype)
    return jnp.einsum('ghij,hk->ghikj', w4, eye).reshape(2, per * LRU_HD, per * LRU_HD)


def _rope_tables(seq):
    half = RET_DK // 2
    pos = jnp.arange(seq, dtype=F32)
    inv_freq = ROPE_THETA ** (-jnp.arange(half, dtype=F32) / half)
    ang = pos[:, None] * inv_freq[None, :]
    cos, sin = jnp.cos(ang), jnp.sin(ang)
    return jnp.concatenate([cos, cos], axis=-1), jnp.concatenate([-sin, sin], axis=-1)


def _layer(x2, batch, seq, cos_t, sin_t, w_in, b_in, conv_w, conv_b, lru_wa, lru_ba, lru_wx,
           lru_bx, lru_lambda, sg_ln_g, sg_ln_b, sg_ws, sg_bs, w_branch, w_out, ln1_g, ln1_b,
           w1, b1, w2, b2, ln2_g, ln2_b):
    bw = BRANCH_W
    parts = [w_in[:, p * bw:(p + 1) * bw] for p in range(9)]
    bparts = [b_in[p * bw:(p + 1) * bw] for p in range(9)]
    rq, rk, rv, rg, lx, lg, sq, sk, sv = range(9)
    uv0 = 9 * bw
    gl0 = uv0 + 2 * bw
    order = (rv, sq, sk, sv, rq, rk, rg, lx, lg)
    w_cols = jnp.concatenate([parts[p] for p in order] + [w_in[:, uv0:gl0]], axis=1).astype(BF16)
    b_cols = jnp.concatenate([bparts[p] for p in order] + [b_in[uv0:gl0]])[None, :]
    zb, zf = _project(x2, w_cols, b_cols)

    y_a = _retention(zf, zb, cos_t, sin_t, batch, seq)
    y_b = _rg_lru(zf, conv_w, conv_b[None, :], _block_diag(lru_wa).astype(BF16), lru_ba[None, :],
                  _block_diag(lru_wx).astype(BF16), lru_bx[None, :], lru_lambda[None, :], batch, seq)
    y_c = _stick_breaking(zb, batch, seq)
    bs_b = jnp.broadcast_to(sg_bs[:, :, None], (SG_GROUPS, BLOCK, BRANCH_W // SG_GROUPS))
    y_d = _spatial_gating(zf, sg_ln_g[None, :], sg_ln_b[None, :], sg_ws, bs_b)

    x1 = _merge((y_a, y_b, y_c, y_d), x2, w_in[:, gl0:].astype(BF16), b_in[None, gl0:],
                w_branch.astype(BF16), w_out.astype(BF16), ln1_g[None, :], ln1_b[None, :])
    return _mlp(x1, w1.astype(BF16), b1[None, :], w2.astype(BF16), b2[None, :],
                ln2_g[None, :], ln2_b[None, :])


def kernel(x, w_in, b_in, conv_w, conv_b, lru_wa, lru_ba, lru_wx, lru_bx, lru_lambda, sg_ln_g, sg_ln_b, sg_ws, sg_bs, w_branch, w_out, ln1_g, ln1_b, w1, b1, w2, b2, ln2_g, ln2_b):
    batch, seq, d = x.shape
    cos_t, sin_t = _rope_tables(seq)
    x2 = x.reshape(batch * seq, d)
    per_layer = (w_in, b_in, conv_w, conv_b, lru_wa, lru_ba, lru_wx, lru_bx, lru_lambda, sg_ln_g,
                 sg_ln_b, sg_ws, sg_bs, w_branch, w_out, ln1_g, ln1_b, w1, b1, w2, b2, ln2_g, ln2_b)
    for l in range(DEPTH):
        x2 = _layer(x2, batch, seq, cos_t, sin_t, *(p[l] for p in per_layer))
    return x2.reshape(batch, seq, d)
```

```python
import functools
import math

import jax
import jax.numpy as jnp
from jax import lax
from jax.experimental import pallas as pl
from jax.experimental.pallas import tpu as pltpu

F32 = jnp.float32
BF16 = jnp.bfloat16

D_MODEL = 1024
DEPTH = 2
BLOCK = 128
BRANCH_W = D_MODEL // 2
N_BRANCH = 4
RET_HEADS = 4
RET_DK = BRANCH_W // RET_HEADS
ROPE_THETA = 10000.0
LRU_HEADS = 8
LRU_HD = BRANCH_W // LRU_HEADS
CONV_W = 4
LRU_C = 8.0
SB_HEADS = 4
SB_HD = BRANCH_W // SB_HEADS
SG_GROUPS = 4
D_FF = 4 * D_MODEL
ALPHA = (2 * DEPTH) ** 0.25
LN_EPS = 1e-5
LOG2E = 1.4426950408889634
SB_EXP_ZERO = -105.0

ZB_RET_V, ZB_SB_Q, ZB_SB_K, ZB_SB_V = 0, 1, 2, 3
ZB_COLS = 4 * BRANCH_W
ZF_RET_Q, ZF_RET_K, ZF_RET_G, ZF_LRU_X, ZF_LRU_G, ZF_SG_U, ZF_SG_V = 0, 1, 2, 3, 4, 5, 6
ZF_COLS = 7 * BRANCH_W

VMEM_LIMIT = 56 * 1024 * 1024

PROJ_TM, PROJ_TN = 512, 512
RET_ROWS = 2 * BLOCK
LRU_ROWS = 256
SG_ROWS = 2 * BLOCK
SB_TQ = 256
MERGE_TM = 512
MLP_TM = 512
MLP_FC = 1024


def _params(*sem):
    return pltpu.CompilerParams(dimension_semantics=sem, vmem_limit_bytes=VMEM_LIMIT)


def _dot(a, b):
    return jnp.dot(a, b, preferred_element_type=F32)


def _dot_nt(a, b):
    return lax.dot_general(a, b, (((1,), (1,)), ((), ())), preferred_element_type=F32)


def _layer_norm(h, g, b):
    mu = jnp.mean(h, axis=-1, keepdims=True)
    d = h - mu
    var = jnp.mean(d * d, axis=-1, keepdims=True)
    return d * lax.rsqrt(var + LN_EPS) * g + b


def _proj_kernel(x_ref, w_ref, b_ref, zb_ref, zf_ref):
    xb = x_ref[...].astype(BF16)
    for c in range(ZB_COLS // PROJ_TN):
        cols = slice(c * PROJ_TN, (c + 1) * PROJ_TN)
        acc = _dot(xb, w_ref[:, cols]) + b_ref[:, cols]
        if cols.start // BRANCH_W == ZB_SB_Q:
            acc = acc * (SB_HD ** -0.5)
        zb_ref[:, cols] = acc.astype(BF16)
    for c in range(ZF_COLS // PROJ_TN):
        cols = slice(c * PROJ_TN, (c + 1) * PROJ_TN)
        wcols = slice(ZB_COLS + cols.start, ZB_COLS + cols.stop)
        zf_ref[:, cols] = _dot(xb, w_ref[:, wcols]) + b_ref[:, wcols]


def _project(x2, w, b):
    t, d = x2.shape
    n = ZB_COLS + ZF_COLS
    whole = functools.partial(pl.BlockSpec, pipeline_mode=pl.Buffered(1))
    return pl.pallas_call(
        _proj_kernel,
        grid=(t // PROJ_TM,),
        in_specs=[
            pl.BlockSpec((PROJ_TM, d), lambda i: (i, 0)),
            whole((d, n), lambda i: (0, 0)),
            pl.BlockSpec((1, n), lambda i: (0, 0)),
        ],
        out_specs=[pl.BlockSpec((PROJ_TM, ZB_COLS), lambda i: (i, 0)),
                   pl.BlockSpec((PROJ_TM, ZF_COLS), lambda i: (i, 0))],
        out_shape=[jax.ShapeDtypeStruct((t, ZB_COLS), BF16), jax.ShapeDtypeStruct((t, ZF_COLS), F32)],
        compiler_params=_params("arbitrary"),
        name="in_proj",
    )(x2, w, b)


def _ret_log_g(h):
    return math.log1p(-(2.0 ** (-5.0 - h)))


def _retention_kernel(q_ref, k_ref, g_ref, v_ref, cos_ref, sin_ref, o_ref,
                      state_ref, dmat_ref, qdec_ref, kdec_ref):
    @pl.when(pl.program_id(1) == 0)
    def _():
        state_ref[...] = jnp.zeros_like(state_ref)
        row = lax.broadcasted_iota(jnp.int32, (BLOCK, BLOCK), 0).astype(F32)
        col = lax.broadcasted_iota(jnp.int32, (BLOCK, BLOCK), 1).astype(F32)
        diff = row - col
        for h in range(RET_HEADS):
            lg = _ret_log_g(h)
            dmat_ref[h] = jnp.where(diff >= 0, jnp.exp(lg * jnp.maximum(diff, 0.0)), 0.0)
            qdec_ref[h] = jnp.exp(lg * (row + 1.0))
            kdec_ref[h] = jnp.exp(lg * (BLOCK - 1.0 - row))

    for c in range(RET_ROWS // BLOCK):
        rows = slice(c * BLOCK, (c + 1) * BLOCK)
        cs = cos_ref[rows, :]
        sn = sin_ref[rows, :]
        for h in range(RET_HEADS):
            cols = slice(h * RET_DK, (h + 1) * RET_DK)
            q = q_ref[rows, cols]
            k = k_ref[rows, cols]
            v = v_ref[rows, cols]
            qr = q * cs + pltpu.roll(q, RET_DK // 2, 1) * sn
            kr = (k * cs + pltpu.roll(k, RET_DK // 2, 1) * sn) * (RET_DK ** -0.5)
            scores = _dot_nt(qr.astype(BF16), kr.astype(BF16)) * dmat_ref[h]
            inner = _dot(scores.astype(BF16), v)
            st = state_ref[h]
            cross = _dot((qr * qdec_ref[h]).astype(BF16), st.astype(BF16))
            kd_t = jnp.transpose(kr * kdec_ref[h]).astype(BF16)
            state_ref[h] = math.exp(_ret_log_g(h) * BLOCK) * st + _dot(kd_t, v)
            y = inner + cross
            mu = jnp.mean(y, axis=-1, keepdims=True)
            d = y - mu
            var = jnp.mean(d * d, axis=-1, keepdims=True)
            yn = d * lax.rsqrt(var + LN_EPS)
            o_ref[rows, cols] = (jax.nn.silu(g_ref[rows, cols]) * yn).astype(o_ref.dtype)


def _retention(zf, zb, cos_t, sin_t, batch, seq):
    nblk = seq // RET_ROWS
    row_map = lambda b, n: b * nblk + n
    return pl.pallas_call(
        _retention_kernel,
        grid=(batch, nblk),
        in_specs=[
            pl.BlockSpec((RET_ROWS, BRANCH_W), lambda b, n: (row_map(b, n), ZF_RET_Q)),
            pl.BlockSpec((RET_ROWS, BRANCH_W), lambda b, n: (row_map(b, n), ZF_RET_K)),
            pl.BlockSpec((RET_ROWS, BRANCH_W), lambda b, n: (row_map(b, n), ZF_RET_G)),
            pl.BlockSpec((RET_ROWS, BRANCH_W), lambda b, n: (row_map(b, n), ZB_RET_V)),
            pl.BlockSpec((RET_ROWS, RET_DK), lambda b, n: (n, 0)),
            pl.BlockSpec((RET_ROWS, RET_DK), lambda b, n: (n, 0)),
        ],
        out_specs=pl.BlockSpec((RET_ROWS, BRANCH_W), lambda b, n: (row_map(b, n), 0)),
        out_shape=jax.ShapeDtypeStruct((batch * seq, BRANCH_W), BF16),
        scratch_shapes=[pltpu.VMEM((RET_HEADS, RET_DK, RET_DK), F32)] * 4,
        compiler_params=_params("arbitrary", "arbitrary"),
        name="retention",
    )(zf, zf, zf, zb, cos_t, sin_t)


def _lru_kernel(x_ref, gate_ref, cw_ref, cb_ref, wa_ref, ba_ref, wx_ref, bx_ref, lam_ref,
                o_ref, xbuf_ref, h_ref):
    rows = LRU_ROWS

    @pl.when(pl.program_id(1) == 0)
    def _():
        xbuf_ref[0:8, :] = jnp.zeros((8, BRANCH_W), F32)
        h_ref[...] = jnp.zeros_like(h_ref)

    xbuf_ref[8:8 + rows, :] = x_ref[...]
    xc = cb_ref[...]
    for tap in range(CONV_W):
        off = 8 - (CONV_W - 1) + tap
        xc = xc + cw_ref[tap:tap + 1, :] * xbuf_ref[off:off + rows, :]
    xbuf_ref[0:8, :] = xbuf_ref[rows:rows + 8, :]

    xcb = xc.astype(BF16)
    half = BRANCH_W // 2
    pre_a = jnp.concatenate([_dot(xcb[:, :half], wa_ref[0]), _dot(xcb[:, half:], wa_ref[1])], axis=1)
    pre_x = jnp.concatenate([_dot(xcb[:, :half], wx_ref[0]), _dot(xcb[:, half:], wx_ref[1])], axis=1)
    r = jax.nn.sigmoid(pre_a + ba_ref[...])
    i = jax.nn.sigmoid(pre_x + bx_ref[...])
    nlam = -lam_ref[...]
    softplus = jnp.maximum(nlam, 0.0) + jnp.log1p(jnp.exp(-jnp.abs(nlam)))
    log_a = -LRU_C * r * softplus
    a_all = jnp.exp(log_a)
    u_all = jnp.sqrt(jnp.tanh(-log_a) * (1.0 + a_all * a_all)) * (i * xc)

    row = lax.broadcasted_iota(jnp.int32, (rows, 128), 0)
    for s in range(BRANCH_W // 128):
        cols = slice(s * 128, (s + 1) * 128)
        a = a_all[:, cols]
        b = u_all[:, cols]
        shift = 1
        while shift < rows:
            keep = row >= shift
            a_sh = jnp.where(keep, pltpu.roll(a, shift, 0), 1.0)
            b_sh = jnp.where(keep, pltpu.roll(b, shift, 0), 0.0)
            b = a * b_sh + b
            a = a * a_sh
            shift *= 2
        h = a * h_ref[0:1, cols] + b
        h_ref[:, cols] = jnp.broadcast_to(h[rows - 1:rows, :], (8, 128))
        o_ref[:, cols] = (h * jax.nn.gelu(gate_ref[:, cols])).astype(o_ref.dtype)


def _rg_lru(zf, conv_w, conv_b, wa_bd, ba, wx_bd, bx, lam, batch, seq):
    nblk = seq // LRU_ROWS
    row_map = lambda b, n: b * nblk + n
    vec = pl.BlockSpec((1, BRANCH_W), lambda b, n: (0, 0))
    mat = pl.BlockSpec((2, BRANCH_W // 2, BRANCH_W // 2), lambda b, n: (0, 0, 0))
    return pl.pallas_call(
        _lru_kernel,
        grid=(batch, nblk),
        in_specs=[
            pl.BlockSpec((LRU_ROWS, BRANCH_W), lambda b, n: (row_map(b, n), ZF_LRU_X)),
            pl.BlockSpec((LRU_ROWS, BRANCH_W), lambda b, n: (row_map(b, n), ZF_LRU_G)),
            pl.BlockSpec((CONV_W, BRANCH_W), lambda b, n: (0, 0)),
            vec, mat, vec, mat, vec, vec,
        ],
        out_specs=pl.BlockSpec((LRU_ROWS, BRANCH_W), lambda b, n: (row_map(b, n), 0)),
        out_shape=jax.ShapeDtypeStruct((batch * seq, BRANCH_W), BF16),
        scratch_shapes=[pltpu.VMEM((LRU_ROWS + 8, BRANCH_W), F32), pltpu.VMEM((8, BRANCH_W), F32)],
        compiler_params=_params("arbitrary", "arbitrary"),
        name="rg_lru",
    )(zf, zf, conv_w, conv_b, wa_bd, ba, wx_bd, bx, lam)


def _sb_key_block(q_ref, k_ref, v_ref, acc_ref, carry_ref, tri, off, mask):
    t = SB_TQ
    heads = range(SB_HEADS)
    cols = [slice(h * SB_HD, (h + 1) * SB_HD) for h in heads]
    zs = [_dot_nt(q_ref[:, cols[h]], k_ref[pl.ds(off, t), cols[h]]) for h in heads]
    cums = []
    for h in heads:
        z = zs[h]
        lg = jnp.log(1.0 + jnp.exp2(jnp.abs(z) * (-LOG2E)))
        log_1m = jnp.minimum(-z, 0.0) - lg
        if mask is not None:
            log_1m = jnp.where(mask, log_1m, 0.0)
        hi = log_1m.astype(BF16)
        lo = (log_1m - hi.astype(F32)).astype(BF16)
        cums.append(_dot(hi, tri) + _dot(lo, tri))
    for h in heads:
        if mask is None:
            carry = carry_ref[h]
            w = jnp.exp(zs[h] + cums[h] + jnp.concatenate([carry] * (t // 128), axis=1))
            carry_ref[h] = carry + cums[h][:, 0:1]
            acc_ref[:, cols[h]] += _dot(w.astype(BF16), v_ref[pl.ds(off, t), cols[h]])
        else:
            w = jnp.where(mask, jnp.exp(zs[h] + cums[h]), 0.0)
            carry_ref[h] = jnp.broadcast_to(cums[h][:, 0:1], (t, 128))
            acc_ref[:, cols[h]] = _dot(w.astype(BF16), v_ref[pl.ds(off, t), cols[h]])


def _sb_kernel(q_ref, k_ref, v_ref, o_ref, acc_ref, carry_ref):
    i = pl.program_id(1)
    t = SB_TQ
    row = lax.broadcasted_iota(jnp.int32, (t, t), 0)
    col = lax.broadcasted_iota(jnp.int32, (t, t), 1)
    tri = (row >= col).astype(BF16)

    _sb_key_block(q_ref, k_ref, v_ref, acc_ref, carry_ref, tri, pl.multiple_of(i * t, t), col < row)

    def any_weight_left():
        m = carry_ref[0]
        for h in range(1, SB_HEADS):
            m = jnp.maximum(m, carry_ref[h])
        return jnp.max(m) > SB_EXP_ZERO

    def cond(state):
        step, live = state
        return jnp.logical_and(step < i, live)

    def body(state):
        step, _ = state
        off = pl.multiple_of((i - 1 - step) * t, t)
        _sb_key_block(q_ref, k_ref, v_ref, acc_ref, carry_ref, tri, off, None)
        return step + 1, any_weight_left()

    lax.while_loop(cond, body, (jnp.int32(0), any_weight_left()))
    o_ref[...] = acc_ref[...].astype(o_ref.dtype)


def _stick_breaking(zb, batch, seq):
    nq = seq // SB_TQ
    zb3 = zb.reshape(batch, seq, ZB_COLS)
    whole = functools.partial(pl.BlockSpec, pipeline_mode=pl.Buffered(1))
    return pl.pallas_call(
        _sb_kernel,
        grid=(batch, nq),
        in_specs=[
            pl.BlockSpec((None, SB_TQ, BRANCH_W), lambda b, i: (b, i, ZB_SB_Q)),
            whole((None, seq, BRANCH_W), lambda b, i: (b, 0, ZB_SB_K)),
            whole((None, seq, BRANCH_W), lambda b, i: (b, 0, ZB_SB_V)),
        ],
        out_specs=pl.BlockSpec((None, SB_TQ, BRANCH_W), lambda b, i: (b, i, 0)),
        out_shape=jax.ShapeDtypeStruct((batch, seq, BRANCH_W), BF16),
        scratch_shapes=[pltpu.VMEM((SB_TQ, BRANCH_W), F32), pltpu.VMEM((SB_HEADS, SB_TQ, 128), F32)],
        compiler_params=_params("arbitrary", "arbitrary"),
        name="stick_breaking",
    )(zb3, zb3, zb3).reshape(batch * seq, BRANCH_W)


def _sg_kernel(u_ref, v_ref, g_ref, b_ref, ws_ref, bs_ref, o_ref):
    u = jax.nn.gelu(u_ref[...])
    v = _layer_norm(jax.nn.gelu(v_ref[...]), g_ref[...], b_ref[...])
    row = lax.broadcasted_iota(jnp.int32, (BLOCK, BLOCK), 0)
    col = lax.broadcasted_iota(jnp.int32, (BLOCK, BLOCK), 1)
    gd = BRANCH_W // SG_GROUPS
    for g in range(SG_GROUPS):
        w = jnp.where(col <= row, ws_ref[g], 0.0).astype(BF16)
        cols = slice(g * gd, (g + 1) * gd)
        for c in range(SG_ROWS // BLOCK):
            rows = slice(c * BLOCK, (c + 1) * BLOCK)
            s = _dot(w, v[rows, cols].astype(BF16)) + bs_ref[g]
            o_ref[rows, cols] = (u[rows, cols] * s).astype(o_ref.dtype)


def _spatial_gating(zf, ln_g, ln_b, ws, bs_b):
    t = zf.shape[0]
    vec = pl.BlockSpec((1, BRANCH_W), lambda n: (0, 0))
    cube = pl.BlockSpec((SG_GROUPS, BLOCK, BLOCK), lambda n: (0, 0, 0))
    return pl.pallas_call(
        _sg_kernel,
        grid=(t // SG_ROWS,),
        in_specs=[
            pl.BlockSpec((SG_ROWS, BRANCH_W), lambda n: (n, ZF_SG_U)),
            pl.BlockSpec((SG_ROWS, BRANCH_W), lambda n: (n, ZF_SG_V)),
            vec, vec, cube, cube,
        ],
        out_specs=pl.BlockSpec((SG_ROWS, BRANCH_W), lambda n: (n, 0)),
        out_shape=jax.ShapeDtypeStruct((t, BRANCH_W), BF16),
        compiler_params=_params("arbitrary"),
        name="spatial_gating",
    )(zf, zf, ln_g, ln_b, ws, bs_b)


def _merge_kernel(ya_ref, yb_ref, yc_ref, yd_ref, x_ref, wg_ref, bg_ref, wb_ref, wo_ref,
                  lg_ref, lb_ref, o_ref):
    x = x_ref[...]
    xb = x.astype(BF16)
    merged = None
    for n, y_ref in enumerate((ya_ref, yb_ref, yc_ref, yd_ref)):
        cols = slice(n * D_MODEL, (n + 1) * D_MODEL)
        gate = jax.nn.sigmoid(_dot(xb, wg_ref[:, cols]) + bg_ref[:, cols])
        term = gate * _dot(y_ref[...], wb_ref[n])
        merged = term if merged is None else merged + term
    out = _dot(merged.astype(BF16), wo_ref[...])
    o_ref[...] = _layer_norm(ALPHA * x + out, lg_ref[...], lb_ref[...])


def _merge(ys, x2, w_gate, b_gate, w_branch, w_out, ln_g, ln_b):
    t = x2.shape[0]
    ysp = pl.BlockSpec((MERGE_TM, BRANCH_W), lambda i: (i, 0))
    vec = pl.BlockSpec((1, D_MODEL), lambda i: (0, 0))
    whole = functools.partial(pl.BlockSpec, pipeline_mode=pl.Buffered(1))
    return pl.pallas_call(
        _merge_kernel,
        grid=(t // MERGE_TM,),
        in_specs=[ysp] * 4 + [
            pl.BlockSpec((MERGE_TM, D_MODEL), lambda i: (i, 0)),
            whole((D_MODEL, N_BRANCH * D_MODEL), lambda i: (0, 0)),
            pl.BlockSpec((1, N_BRANCH * D_MODEL), lambda i: (0, 0)),
            whole((N_BRANCH, BRANCH_W, D_MODEL), lambda i: (0, 0, 0)),
            whole((D_MODEL, D_MODEL), lambda i: (0, 0)),
            vec, vec,
        ],
        out_specs=pl.BlockSpec((MERGE_TM, D_MODEL), lambda i: (i, 0)),
        out_shape=jax.ShapeDtypeStruct((t, D_MODEL), F32),
        compiler_params=_params("arbitrary"),
        name="merge_out",
    )(*ys, x2, w_gate, b_gate, w_branch, w_out, ln_g, ln_b)


def _mlp_kernel(x_ref, w1_ref, b1_ref, w2_ref, b2_ref, lg_ref, lb_ref, o_ref):
    x = x_ref[...]
    xb = x.astype(BF16)
    acc = None
    for c in range(D_FF // MLP_FC):
        cols = slice(c * MLP_FC, (c + 1) * MLP_FC)
        hid = jnp.maximum(_dot(xb, w1_ref[:, cols]) + b1_ref[:, cols], 0.0)
        part = _dot((hid * hid).astype(BF16), w2_ref[cols, :])
        acc = part if acc is None else acc + part
    o_ref[...] = _layer_norm(ALPHA * x + (acc + b2_ref[...]), lg_ref[...], lb_ref[...])


def _mlp(x2, w1, b1, w2, b2, ln_g, ln_b):
    t = x2.shape[0]
    vec = pl.BlockSpec((1, D_MODEL), lambda i: (0, 0))
    return pl.pallas_call(
        _mlp_kernel,
        grid=(t // MLP_TM,),
        in_specs=[
            pl.BlockSpec((MLP_TM, D_MODEL), lambda i: (i, 0)),
            pl.BlockSpec((D_MODEL, D_FF), lambda i: (0, 0), pipeline_mode=pl.Buffered(1)),
            pl.BlockSpec((1, D_FF), lambda i: (0, 0)),
            pl.BlockSpec((D_FF, D_MODEL), lambda i: (0, 0), pipeline_mode=pl.Buffered(1)),
            vec, vec, vec,
        ],
        out_specs=pl.BlockSpec((MLP_TM, D_MODEL), lambda i: (i, 0)),
        out_shape=jax.ShapeDtypeStruct((t, D_MODEL), F32),
        compiler_params=_params("arbitrary"),
        name="mlp",
    )(x2, w1, b1, w2, b2, ln_g, ln_b)


def _block_diag(w):
    per = LRU_HEADS // 2
    w4 = w.reshape(2, per, LRU_HD, LRU_HD)
    eye = jnp.eye(per, dtype=w.dtype)
    return jnp.einsum('ghij,hk->ghikj', w4, eye).reshape(2, per * LRU_HD, per * LRU_HD)


def _rope_tables(seq):
    half = RET_DK // 2
    pos = jnp.arange(seq, dtype=F32)
    inv_freq = ROPE_THETA ** (-jnp.arange(half, dtype=F32) / half)
    ang = pos[:, None] * inv_freq[None, :]
    cos, sin = jnp.cos(ang), jnp.sin(ang)
    return jnp.concatenate([cos, cos], axis=-1), jnp.concatenate([-sin, sin], axis=-1)


def _layer(x2, batch, seq, cos_t, sin_t, w_in, b_in, conv_w, conv_b, lru_wa, lru_ba, lru_wx,
           lru_bx, lru_lambda, sg_ln_g, sg_ln_b, sg_ws, sg_bs, w_branch, w_out, ln1_g, ln1_b,
           w1, b1, w2, b2, ln2_g, ln2_b):
    bw = BRANCH_W
    parts = [w_in[:, p * bw:(p + 1) * bw] for p in range(9)]
    bparts = [b_in[p * bw:(p + 1) * bw] for p in range(9)]
    rq, rk, rv, rg, lx, lg, sq, sk, sv = range(9)
    uv0 = 9 * bw
    gl0 = uv0 + 2 * bw
    order = (rv, sq, sk, sv, rq, rk, rg, lx, lg)
    w_cols = jnp.concatenate([parts[p] for p in order] + [w_in[:, uv0:gl0]], axis=1).astype(BF16)
    b_cols = jnp.concatenate([bparts[p] for p in order] + [b_in[uv0:gl0]])[None, :]
    zb, zf = _project(x2, w_cols, b_cols)

    y_a = _retention(zf, zb, cos_t, sin_t, batch, seq)
    y_b = _rg_lru(zf, conv_w, conv_b[None, :], _block_diag(lru_wa).astype(BF16), lru_ba[None, :],
                  _block_diag(lru_wx).astype(BF16), lru_bx[None, :], lru_lambda[None, :], batch, seq)
    y_c = _stick_breaking(zb, batch, seq)
    bs_b = jnp.broadcast_to(sg_bs[:, :, None], (SG_GROUPS, BLOCK, BRANCH_W // SG_GROUPS))
    y_d = _spatial_gating(zf, sg_ln_g[None, :], sg_ln_b[None, :], sg_ws, bs_b)

    x1 = _merge((y_a, y_b, y_c, y_d), x2, w_in[:, gl0:].astype(BF16), b_in[None, gl0:],
                w_branch.astype(BF16), w_out.astype(BF16), ln1_g[None, :], ln1_b[None, :])
    return _mlp(x1, w1.astype(BF16), b1[None, :], w2.astype(BF16), b2[None, :],
                ln2_g[None, :], ln2_b[None, :])


def kernel(x, w_in, b_in, conv_w, conv_b, lru_wa, lru_ba, lru_wx, lru_bx, lru_lambda, sg_ln_g, sg_ln_b, sg_ws, sg_bs, w_branch, w_out, ln1_g, ln1_b, w1, b1, w2, b2, ln2_g, ln2_b):
    batch, seq, d = x.shape
    cos_t, sin_t = _rope_tables(seq)
    x2 = x.reshape(batch * seq, d)
    per_layer = (w_in, b_in, conv_w, conv_b, lru_wa, lru_ba, lru_wx, lru_bx, lru_lambda, sg_ln_g,
                 sg_ln_b, sg_ws, sg_bs, w_branch, w_out, ln1_g, ln1_b, w1, b1, w2, b2, ln2_g, ln2_b)
    for l in range(DEPTH):
        x2 = _layer(x2, batch, seq, cos_t, sin_t, *(p[l] for p in per_layer))
    return x2.reshape(batch, seq, d)
```

```python
import functools
import math

import jax
import jax.numpy as jnp
from jax import lax
from jax.experimental import pallas as pl
from jax.experimental.pallas import tpu as pltpu

F32 = jnp.float32
BF16 = jnp.bfloat16

D_MODEL = 1024
DEPTH = 2
BLOCK = 128
BRANCH_W = D_MODEL // 2
N_BRANCH = 4
RET_HEADS = 4
RET_DK = BRANCH_W // RET_HEADS
ROPE_THETA = 10000.0
LRU_HEADS = 8
LRU_HD = BRANCH_W // LRU_HEADS
CONV_W = 4
LRU_C = 8.0
SB_HEADS = 4
SB_HD = BRANCH_W // SB_HEADS
SG_GROUPS = 4
D_FF = 4 * D_MODEL
ALPHA = (2 * DEPTH) ** 0.25
LN_EPS = 1e-5
LOG2E = 1.4426950408889634
SB_EXP_ZERO = -105.0

W_RET_Q, W_RET_K, W_RET_G, W_RET_V, W_LRU_X, W_LRU_G, W_SG_U, W_SG_V, W_SB_Q, W_SB_K, W_SB_V = range(11)
W_COLS = 11 * BRANCH_W
ZF_LRU_X, ZF_LRU_G, ZF_SG_U, ZF_SG_V = 0, 1, 2, 3
ZF_COLS = 4 * BRANCH_W
ZB_SB_Q, ZB_SB_K, ZB_SB_V = 0, 1, 2
ZB_COLS = 3 * BRANCH_W

VMEM_LIMIT = 56 * 1024 * 1024

PROJ_TM = 512
LRU_ROWS = 256
SB_TQ = 256
MERGE_TM = 512
MLP_TM = 512
MLP_FC = 1024


def _params(*sem):
    return pltpu.CompilerParams(dimension_semantics=sem, vmem_limit_bytes=VMEM_LIMIT)


def _dot(a, b):
    return jnp.dot(a, b, preferred_element_type=F32)


def _dot_nt(a, b):
    return lax.dot_general(a, b, (((1,), (1,)), ((), ())), preferred_element_type=F32)


def _layer_norm(h, g, b):
    mu = jnp.mean(h, axis=-1, keepdims=True)
    d = h - mu
    var = jnp.mean(d * d, axis=-1, keepdims=True)
    return d * lax.rsqrt(var + LN_EPS) * g + b


def _ret_log_g(h):
    return math.log1p(-(2.0 ** (-5.0 - h)))


def _ret_chunk(rows, zq_ref, zk_ref, zg_ref, zv_ref, cos_ref, sin_ref, ya_ref,
               state_ref, dmat_ref, qdec_ref, kdec_ref):
    heads = range(RET_HEADS)
    cols = [slice(h * RET_DK, (h + 1) * RET_DK) for h in heads]
    cs = cos_ref[rows, :]
    sn = sin_ref[rows, :]
    qr, kr = [], []
    for h in heads:
        q = zq_ref[rows, cols[h]]
        k = zk_ref[rows, cols[h]]
        qr.append(q * cs + pltpu.roll(q, RET_DK // 2, 1) * sn)
        kr.append((k * cs + pltpu.roll(k, RET_DK // 2, 1) * sn) * (RET_DK ** -0.5))
    scores = [_dot_nt(qr[h].astype(BF16), kr[h].astype(BF16)) * dmat_ref[h] for h in heads]
    cross = [_dot((qr[h] * qdec_ref[h]).astype(BF16), state_ref[h].astype(BF16)) for h in heads]
    kd_t = [jnp.transpose(kr[h] * kdec_ref[h]).astype(BF16) for h in heads]
    for h in heads:
        v = zv_ref[rows, cols[h]]
        y = _dot(scores[h].astype(BF16), v) + cross[h]
        state_ref[h] = math.exp(_ret_log_g(h) * BLOCK) * state_ref[h] + _dot(kd_t[h], v)
        mu = jnp.mean(y, axis=-1, keepdims=True)
        d = y - mu
        var = jnp.mean(d * d, axis=-1, keepdims=True)
        yn = d * lax.rsqrt(var + LN_EPS)
        ya_ref[rows, cols[h]] = (jax.nn.silu(zg_ref[rows, cols[h]]) * yn).astype(ya_ref.dtype)


def _proj_kernel(x_ref, w_ref, b_ref, cos_ref, sin_ref, zb_ref, zf_ref, ya_ref,
                 zq_ref, zk_ref, zg_ref, zv_ref, state_ref, dmat_ref, qdec_ref, kdec_ref):
    @pl.when(pl.program_id(1) == 0)
    def _():
        state_ref[...] = jnp.zeros_like(state_ref)
        row = lax.broadcasted_iota(jnp.int32, (BLOCK, BLOCK), 0).astype(F32)
        col = lax.broadcasted_iota(jnp.int32, (BLOCK, BLOCK), 1).astype(F32)
        diff = row - col
        for h in range(RET_HEADS):
            lg = _ret_log_g(h)
            dmat_ref[h] = jnp.where(diff >= 0, jnp.exp(lg * jnp.maximum(diff, 0.0)), 0.0)
            qdec_ref[h] = jnp.exp(lg * (row + 1.0))
            kdec_ref[h] = jnp.exp(lg * (BLOCK - 1.0 - row))

    xb = x_ref[...].astype(BF16)

    def mm(part):
        cols = slice(part * BRANCH_W, (part + 1) * BRANCH_W)
        return _dot(xb, w_ref[:, cols]) + b_ref[:, cols]

    zq_ref[...] = mm(W_RET_Q)
    zk_ref[...] = mm(W_RET_K)
    zg_ref[...] = mm(W_RET_G)
    zv_ref[...] = mm(W_RET_V).astype(BF16)

    def emit(part):
        if part in (W_SB_Q, W_SB_K, W_SB_V):
            acc = mm(part)
            if part == W_SB_Q:
                acc = acc * (SB_HD ** -0.5)
            o = (part - W_SB_Q) * BRANCH_W
            zb_ref[:, o:o + BRANCH_W] = acc.astype(BF16)
        else:
            o = (part - W_LRU_X) * BRANCH_W
            zf_ref[:, o:o + BRANCH_W] = mm(part)

    rest = [W_LRU_X, W_LRU_G, W_SG_U, W_SG_V, W_SB_Q, W_SB_K, W_SB_V]
    nchunk = PROJ_TM // BLOCK
    per = -(-len(rest) // nchunk)
    for c in range(nchunk):
        _ret_chunk(slice(c * BLOCK, (c + 1) * BLOCK), zq_ref, zk_ref, zg_ref, zv_ref, cos_ref, sin_ref,
                   ya_ref, state_ref, dmat_ref, qdec_ref, kdec_ref)
        for part in rest[c * per:(c + 1) * per]:
            emit(part)


def _project(x2, w, b, cos_t, sin_t, batch, seq):
    t, d = x2.shape
    nblk = seq // PROJ_TM
    row_map = lambda bi, n: (bi * nblk + n, 0)
    whole = functools.partial(pl.BlockSpec, pipeline_mode=pl.Buffered(1))
    tile = lambda width: pl.BlockSpec((PROJ_TM, width), row_map)
    return pl.pallas_call(
        _proj_kernel,
        grid=(batch, nblk),
        in_specs=[
            tile(d),
            whole((d, W_COLS), lambda bi, n: (0, 0)),
            pl.BlockSpec((1, W_COLS), lambda bi, n: (0, 0)),
            pl.BlockSpec((PROJ_TM, RET_DK), lambda bi, n: (n, 0)),
            pl.BlockSpec((PROJ_TM, RET_DK), lambda bi, n: (n, 0)),
        ],
        out_specs=[tile(ZB_COLS), tile(ZF_COLS), tile(BRANCH_W)],
        out_shape=[jax.ShapeDtypeStruct((t, ZB_COLS), BF16), jax.ShapeDtypeStruct((t, ZF_COLS), F32),
                   jax.ShapeDtypeStruct((t, BRANCH_W), BF16)],
        scratch_shapes=[pltpu.VMEM((PROJ_TM, BRANCH_W), F32)] * 3 + [pltpu.VMEM((PROJ_TM, BRANCH_W), BF16)]
        + [pltpu.VMEM((RET_HEADS, RET_DK, RET_DK), F32)] * 4,
        compiler_params=_params("arbitrary", "arbitrary"),
        name="in_proj_retention",
    )(x2, w, b, cos_t, sin_t)


def _lru_kernel(x_ref, gate_ref, cw_ref, cb_ref, wa_ref, ba_ref, wx_ref, bx_ref, lam_ref,
                o_ref, xbuf_ref, h_ref):
    rows = LRU_ROWS

    @pl.when(pl.program_id(1) == 0)
    def _():
        xbuf_ref[0:8, :] = jnp.zeros((8, BRANCH_W), F32)
        h_ref[...] = jnp.zeros_like(h_ref)

    xbuf_ref[8:8 + rows, :] = x_ref[...]
    xc = cb_ref[...]
    for tap in range(CONV_W):
        off = 8 - (CONV_W - 1) + tap
        xc = xc + cw_ref[tap:tap + 1, :] * xbuf_ref[off:off + rows, :]
    xbuf_ref[0:8, :] = xbuf_ref[rows:rows + 8, :]

    xcb = xc.astype(BF16)
    half = BRANCH_W // 2
    pre_a = jnp.concatenate([_dot(xcb[:, :half], wa_ref[0]), _dot(xcb[:, half:], wa_ref[1])], axis=1)
    pre_x = jnp.concatenate([_dot(xcb[:, :half], wx_ref[0]), _dot(xcb[:, half:], wx_ref[1])], axis=1)
    r = jax.nn.sigmoid(pre_a + ba_ref[...])
    i = jax.nn.sigmoid(pre_x + bx_ref[...])
    nlam = -lam_ref[...]
    softplus = jnp.maximum(nlam, 0.0) + jnp.log1p(jnp.exp(-jnp.abs(nlam)))
    log_a = -LRU_C * r * softplus
    a_all = jnp.exp(log_a)
    u_all = jnp.sqrt(jnp.tanh(-log_a) * (1.0 + a_all * a_all)) * (i * xc)

    row = lax.broadcasted_iota(jnp.int32, (rows, 128), 0)
    for s in range(BRANCH_W // 128):
        cols = slice(s * 128, (s + 1) * 128)
        a = a_all[:, cols]
        b = u_all[:, cols]
        shift = 1
        while shift < rows:
            keep = row >= shift
            a_sh = jnp.where(keep, pltpu.roll(a, shift, 0), 1.0)
            b_sh = jnp.where(keep, pltpu.roll(b, shift, 0), 0.0)
            b = a * b_sh + b
            a = a * a_sh
            shift *= 2
        h = a * h_ref[0:1, cols] + b
        h_ref[:, cols] = jnp.broadcast_to(h[rows - 1:rows, :], (8, 128))
        o_ref[:, cols] = (h * jax.nn.gelu(gate_ref[:, cols])).astype(o_ref.dtype)


def _rg_lru(zf, conv_w, conv_b, wa_bd, ba, wx_bd, bx, lam, batch, seq):
    nblk = seq // LRU_ROWS
    row_map = lambda b, n: b * nblk + n
    vec = pl.BlockSpec((1, BRANCH_W), lambda b, n: (0, 0))
    mat = pl.BlockSpec((2, BRANCH_W // 2, BRANCH_W // 2), lambda b, n: (0, 0, 0))
    return pl.pallas_call(
        _lru_kernel,
        grid=(batch, nblk),
        in_specs=[
            pl.BlockSpec((LRU_ROWS, BRANCH_W), lambda b, n: (row_map(b, n), ZF_LRU_X)),
            pl.BlockSpec((LRU_ROWS, BRANCH_W), lambda b, n: (row_map(b, n), ZF_LRU_G)),
            pl.BlockSpec((CONV_W, BRANCH_W), lambda b, n: (0, 0)),
            vec, mat, vec, mat, vec, vec,
        ],
        out_specs=pl.BlockSpec((LRU_ROWS, BRANCH_W), lambda b, n: (row_map(b, n), 0)),
        out_shape=jax.ShapeDtypeStruct((batch * seq, BRANCH_W), BF16),
        scratch_shapes=[pltpu.VMEM((LRU_ROWS + 8, BRANCH_W), F32), pltpu.VMEM((8, BRANCH_W), F32)],
        compiler_params=_params("arbitrary", "arbitrary"),
        name="rg_lru",
    )(zf, zf, conv_w, conv_b, wa_bd, ba, wx_bd, bx, lam)


def _sb_key_block(q_ref, k_ref, v_ref, acc_ref, carry_ref, tri, off, mask):
    t = SB_TQ
    heads = range(SB_HEADS)
    cols = [slice(h * SB_HD, (h + 1) * SB_HD) for h in heads]
    zs = [_dot_nt(q_ref[:, cols[h]], k_ref[pl.ds(off, t), cols[h]]) for h in heads]
    cums = []
    for h in heads:
        z = zs[h]
        lg = jnp.log(1.0 + jnp.exp2(jnp.abs(z) * (-LOG2E)))
        log_1m = jnp.minimum(-z, 0.0) - lg
        if mask is not None:
            log_1m = jnp.where(mask, log_1m, 0.0)
        hi = log_1m.astype(BF16)
        lo = (log_1m - hi.astype(F32)).astype(BF16)
        cums.append(_dot(hi, tri) + _dot(lo, tri))
    for h in heads:
        if mask is None:
            carry = carry_ref[h]
            w = jnp.exp(zs[h] + cums[h] + jnp.concatenate([carry] * (t // 128), axis=1))
            carry_ref[h] = carry + cums[h][:, 0:1]
            acc_ref[:, cols[h]] += _dot(w.astype(BF16), v_ref[pl.ds(off, t), cols[h]])
        else:
            w = jnp.where(mask, jnp.exp(zs[h] + cums[h]), 0.0)
            carry_ref[h] = jnp.broadcast_to(cums[h][:, 0:1], (t, 128))
            acc_ref[:, cols[h]] = _dot(w.astype(BF16), v_ref[pl.ds(off, t), cols[h]])


def _sb_kernel(q_ref, k_ref, v_ref, o_ref, acc_ref, carry_ref):
    i = pl.program_id(1)
    t = SB_TQ
    row = lax.broadcasted_iota(jnp.int32, (t, t), 0)
    col = lax.broadcasted_iota(jnp.int32, (t, t), 1)
    tri = (row >= col).astype(BF16)

    _sb_key_block(q_ref, k_ref, v_ref, acc_ref, carry_ref, tri, pl.multiple_of(i * t, t), col < row)

    def any_weight_left():
        m = carry_ref[0]
        for h in range(1, SB_HEADS):
            m = jnp.maximum(m, carry_ref[h])
        return jnp.max(m) > SB_EXP_ZERO

    def cond(state):
        step, live = state
        return jnp.logical_and(step < i, live)

    def body(state):
        step, _ = state
        off = pl.multiple_of((i - 1 - step) * t, t)
        _sb_key_block(q_ref, k_ref, v_ref, acc_ref, carry_ref, tri, off, None)
        return step + 1, any_weight_left()

    lax.while_loop(cond, body, (jnp.int32(0), any_weight_left()))
    o_ref[...] = acc_ref[...].astype(o_ref.dtype)


def _stick_breaking(zb, batch, seq):
    nq = seq // SB_TQ
    zb3 = zb.reshape(batch, seq, ZB_COLS)
    whole = functools.partial(pl.BlockSpec, pipeline_mode=pl.Buffered(1))
    return pl.pallas_call(
        _sb_kernel,
        grid=(batch, nq),
        in_specs=[
            pl.BlockSpec((None, SB_TQ, BRANCH_W), lambda b, i: (b, i, ZB_SB_Q)),
            whole((None, seq, BRANCH_W), lambda b, i: (b, 0, ZB_SB_K)),
            whole((None, seq, BRANCH_W), lambda b, i: (b, 0, ZB_SB_V)),
        ],
        out_specs=pl.BlockSpec((None, SB_TQ, BRANCH_W), lambda b, i: (b, i, 0)),
        out_shape=jax.ShapeDtypeStruct((batch, seq, BRANCH_W), BF16),
        scratch_shapes=[pltpu.VMEM((SB_TQ, BRANCH_W), F32), pltpu.VMEM((SB_HEADS, SB_TQ, 128), F32)],
        compiler_params=_params("arbitrary", "arbitrary"),
        name="stick_breaking",
    )(zb3, zb3, zb3).reshape(batch * seq, BRANCH_W)


def _sg_chunk(rows, u_ref, v_ref, g_ref, b_ref, ws_ref, bs_ref, o_ref):
    u = jax.nn.gelu(u_ref[rows, :])
    v = _layer_norm(jax.nn.gelu(v_ref[rows, :]), g_ref[...], b_ref[...]).astype(BF16)
    row = lax.broadcasted_iota(jnp.int32, (BLOCK, BLOCK), 0)
    col = lax.broadcasted_iota(jnp.int32, (BLOCK, BLOCK), 1)
    gd = BRANCH_W // SG_GROUPS
    for g in range(SG_GROUPS):
        w = jnp.where(col <= row, ws_ref[g], 0.0).astype(BF16)
        cols = slice(g * gd, (g + 1) * gd)
        s = _dot(w, v[:, cols]) + bs_ref[g]
        o_ref[rows, cols] = (u[:, cols] * s).astype(o_ref.dtype)


def _merge_kernel(ya_ref, yb_ref, yc_ref, u_ref, v_ref, x_ref, sg_g_ref, sg_b_ref, ws_ref, bs_ref,
                  wg_ref, bg_ref, wb_ref, wo_ref, lg_ref, lb_ref, o_ref, yd_ref):
    xb = x_ref[...].astype(BF16)
    nchunk = MERGE_TM // BLOCK
    sg_after = {0: range(0, nchunk // 2), 1: range(nchunk // 2, nchunk - 1), 2: range(nchunk - 1, nchunk)}
    merged = None
    for n, y_ref in enumerate((ya_ref, yb_ref, yc_ref, yd_ref)):
        cols = slice(n * D_MODEL, (n + 1) * D_MODEL)
        logits = _dot(xb, wg_ref[:, cols]) + bg_ref[:, cols]
        for c in sg_after.get(n, ()):
            _sg_chunk(slice(c * BLOCK, (c + 1) * BLOCK), u_ref, v_ref, sg_g_ref, sg_b_ref, ws_ref, bs_ref,
                      yd_ref)
        term = jax.nn.sigmoid(logits) * _dot(y_ref[...], wb_ref[n])
        merged = term if merged is None else merged + term
    mb = merged.astype(BF16)
    half = MERGE_TM // 2
    outs = [_dot(mb[r * half:(r + 1) * half, :], wo_ref[...]) for r in range(2)]
    for r in range(2):
        rows = slice(r * half, (r + 1) * half)
        o_ref[rows, :] = _layer_norm(ALPHA * x_ref[rows, :] + outs[r], lg_ref[...], lb_ref[...])


def _merge(ys, zf, x2, sg_ln_g, sg_ln_b, ws, bs_b, w_gate, b_gate, w_branch, w_out, ln_g, ln_b):
    t = x2.shape[0]
    ysp = pl.BlockSpec((MERGE_TM, BRANCH_W), lambda i: (i, 0))
    vec = pl.BlockSpec((1, D_MODEL), lambda i: (0, 0))
    hvec = pl.BlockSpec((1, BRANCH_W), lambda i: (0, 0))
    cube = pl.BlockSpec((SG_GROUPS, BLOCK, BLOCK), lambda i: (0, 0, 0))
    whole = functools.partial(pl.BlockSpec, pipeline_mode=pl.Buffered(1))
    return pl.pallas_call(
        _merge_kernel,
        grid=(t // MERGE_TM,),
        in_specs=[ysp] * 3 + [
            pl.BlockSpec((MERGE_TM, BRANCH_W), lambda i: (i, ZF_SG_U)),
            pl.BlockSpec((MERGE_TM, BRANCH_W), lambda i: (i, ZF_SG_V)),
            pl.BlockSpec((MERGE_TM, D_MODEL), lambda i: (i, 0)),
            hvec, hvec, cube, cube,
            whole((D_MODEL, N_BRANCH * D_MODEL), lambda i: (0, 0)),
            pl.BlockSpec((1, N_BRANCH * D_MODEL), lambda i: (0, 0)),
            whole((N_BRANCH, BRANCH_W, D_MODEL), lambda i: (0, 0, 0)),
            whole((D_MODEL, D_MODEL), lambda i: (0, 0)),
            vec, vec,
        ],
        out_specs=pl.BlockSpec((MERGE_TM, D_MODEL), lambda i: (i, 0)),
        out_shape=jax.ShapeDtypeStruct((t, D_MODEL), F32),
        scratch_shapes=[pltpu.VMEM((MERGE_TM, BRANCH_W), BF16)],
        compiler_params=_params("arbitrary"),
        name="sg_merge_out",
    )(*ys, zf, zf, x2, sg_ln_g, sg_ln_b, ws, bs_b, w_gate, b_gate, w_branch, w_out, ln_g, ln_b)


def _mlp_kernel(x_ref, w1_ref, b1_ref, w2_ref, b2_ref, lg_ref, lb_ref, o_ref):
    x = x_ref[...]
    xb = x.astype(BF16)
    acc = None
    for c in range(D_FF // MLP_FC):
        cols = slice(c * MLP_FC, (c + 1) * MLP_FC)
        hid = jnp.maximum(_dot(xb, w1_ref[:, cols]) + b1_ref[:, cols], 0.0)
        part = _dot((hid * hid).astype(BF16), w2_ref[cols, :])
        acc = part if acc is None else acc + part
    o_ref[...] = _layer_norm(ALPHA * x + (acc + b2_ref[...]), lg_ref[...], lb_ref[...])


def _mlp(x2, w1, b1, w2, b2, ln_g, ln_b):
    t = x2.shape[0]
    vec = pl.BlockSpec((1, D_MODEL), lambda i: (0, 0))
    return pl.pallas_call(
        _mlp_kernel,
        grid=(t // MLP_TM,),
        in_specs=[
            pl.BlockSpec((MLP_TM, D_MODEL), lambda i: (i, 0)),
            pl.BlockSpec((D_MODEL, D_FF), lambda i: (0, 0), pipeline_mode=pl.Buffered(1)),
            pl.BlockSpec((1, D_FF), lambda i: (0, 0)),
            pl.BlockSpec((D_FF, D_MODEL), lambda i: (0, 0), pipeline_mode=pl.Buffered(1)),
            vec, vec, vec,
        ],
        out_specs=pl.BlockSpec((MLP_TM, D_MODEL), lambda i: (i, 0)),
        out_shape=jax.ShapeDtypeStruct((t, D_MODEL), F32),
        compiler_params=_params("arbitrary"),
        name="mlp",
    )(x2, w1, b1, w2, b2, ln_g, ln_b)


def _block_diag(w):
    per = LRU_HEADS // 2
    w4 = w.reshape(2, per, LRU_HD, LRU_HD)
    eye = jnp.eye(per, dtype=w.dtype)
    return jnp.einsum('ghij,hk->ghikj', w4, eye).reshape(2, per * LRU_HD, per * LRU_HD)


def _rope_tables(seq):
    half = RET_DK // 2
    pos = jnp.arange(seq, dtype=F32)
    inv_freq = ROPE_THETA ** (-jnp.arange(half, dtype=F32) / half)
    ang = pos[:, None] * inv_freq[None, :]
    cos, sin = jnp.cos(ang), jnp.sin(ang)
    return jnp.concatenate([cos, cos], axis=-1), jnp.concatenate([-sin, sin], axis=-1)


def _layer(x2, batch, seq, cos_t, sin_t, w_in, b_in, conv_w, conv_b, lru_wa, lru_ba, lru_wx,
           lru_bx, lru_lambda, sg_ln_g, sg_ln_b, sg_ws, sg_bs, w_branch, w_out, ln1_g, ln1_b,
           w1, b1, w2, b2, ln2_g, ln2_b):
    bw = BRANCH_W
    parts = [w_in[:, p * bw:(p + 1) * bw] for p in range(9)]
    bparts = [b_in[p * bw:(p + 1) * bw] for p in range(9)]
    rq, rk, rv, rg, lx, lg, sq, sk, sv = range(9)
    uv0 = 9 * bw
    gl0 = uv0 + 2 * bw
    order = (rq, rk, rg, rv, lx, lg)
    w_cols = jnp.concatenate([parts[p] for p in order] + [w_in[:, uv0:gl0]] + [parts[p] for p in (sq, sk, sv)],
                             axis=1).astype(BF16)
    b_cols = jnp.concatenate([bparts[p] for p in order] + [b_in[uv0:gl0]]
                             + [bparts[p] for p in (sq, sk, sv)])[None, :]
    zb, zf, y_a = _project(x2, w_cols, b_cols, cos_t, sin_t, batch, seq)

    y_b = _rg_lru(zf, conv_w, conv_b[None, :], _block_diag(lru_wa).astype(BF16), lru_ba[None, :],
                  _block_diag(lru_wx).astype(BF16), lru_bx[None, :], lru_lambda[None, :], batch, seq)
    y_c = _stick_breaking(zb, batch, seq)
    bs_b = jnp.broadcast_to(sg_bs[:, :, None], (SG_GROUPS, BLOCK, BRANCH_W // SG_GROUPS))

    x1 = _merge((y_a, y_b, y_c), zf, x2, sg_ln_g[None, :], sg_ln_b[None, :], sg_ws, bs_b,
                w_in[:, gl0:].astype(BF16), b_in[None, gl0:], w_branch.astype(BF16), w_out.astype(BF16),
                ln1_g[None, :], ln1_b[None, :])
    return _mlp(x1, w1.astype(BF16), b1[None, :], w2.astype(BF16), b2[None, :],
                ln2_g[None, :], ln2_b[None, :])


def kernel(x, w_in, b_in, conv_w, conv_b, lru_wa, lru_ba, lru_wx, lru_bx, lru_lambda, sg_ln_g, sg_ln_b, sg_ws, sg_bs, w_branch, w_out, ln1_g, ln1_b, w1, b1, w2, b2, ln2_g, ln2_b):
    batch, seq, d = x.shape
    cos_t, sin_t = _rope_tables(seq)
    x2 = x.reshape(batch * seq, d)
    per_layer = (w_in, b_in, conv_w, conv_b, lru_wa, lru_ba, lru_wx, lru_bx, lru_lambda, sg_ln_g,
                 sg_ln_b, sg_ws, sg_bs, w_branch, w_out, ln1_g, ln1_b, w1, b1, w2, b2, ln2_g, ln2_b)
    for l in range(DEPTH):
        x2 = _layer(x2, batch, seq, cos_t, sin_t, *(p[l] for p in per_layer))
    return x2.reshape(batch, seq, d)
```

```python
import functools
import math

import jax
import jax.numpy as jnp
from jax import lax
from jax.experimental import pallas as pl
from jax.experimental.pallas import tpu as pltpu

F32 = jnp.float32
BF16 = jnp.bfloat16

D_MODEL = 1024
DEPTH = 2
BLOCK = 128
BRANCH_W = D_MODEL // 2
N_BRANCH = 4
RET_HEADS = 4
RET_DK = BRANCH_W // RET_HEADS
ROPE_THETA = 10000.0
LRU_HEADS = 8
LRU_HD = BRANCH_W // LRU_HEADS
CONV_W = 4
LRU_C = 8.0
SB_HEADS = 4
SB_HD = BRANCH_W // SB_HEADS
SG_GROUPS = 4
D_FF = 4 * D_MODEL
ALPHA = (2 * DEPTH) ** 0.25
LN_EPS = 1e-5
LOG2E = 1.4426950408889634
SB_EXP_ZERO = -105.0

W_RET_Q, W_RET_K, W_RET_V, W_RET_G, W_LRU_X, W_LRU_G, W_SB_Q, W_SB_K, W_SB_V, W_SG_U, W_SG_V = range(11)
W_COLS = 11 * BRANCH_W
ZF_SG_U, ZF_SG_V = 0, 1
ZF_COLS = 2 * BRANCH_W
ZB_SB_Q, ZB_SB_K, ZB_SB_V = 0, 1, 2
ZB_COLS = 3 * BRANCH_W

VMEM_LIMIT = 56 * 1024 * 1024

PROJ_TM = 512
LRU_ROWS = 128
SB_TQ = 256
MERGE_TM = 512
MLP_TM = 512
MLP_FC = 1024


def _params(*sem):
    return pltpu.CompilerParams(dimension_semantics=sem, vmem_limit_bytes=VMEM_LIMIT)


def _dot(a, b):
    return jnp.dot(a, b, preferred_element_type=F32)


def _dot_nt(a, b):
    return lax.dot_general(a, b, (((1,), (1,)), ((), ())), preferred_element_type=F32)


def _gelu(x):
    k1 = -2.0 * math.sqrt(2.0 / math.pi) * LOG2E
    return x / (1.0 + jnp.exp2(x * (k1 + (k1 * 0.044715) * (x * x))))


def _layer_norm(h, g, b):
    mu = jnp.mean(h, axis=-1, keepdims=True)
    d = h - mu
    var = jnp.mean(d * d, axis=-1, keepdims=True)
    return d * lax.rsqrt(var + LN_EPS) * g + b


def _ret_log_g(h):
    return math.log1p(-(2.0 ** (-5.0 - h)))


def _ret_chunk(rows, zq_ref, zk_ref, zg_ref, zv_ref, cos_ref, sin_ref, ya_ref,
               state_ref, dmat_ref, qdec_ref, kdec_ref):
    heads = range(RET_HEADS)
    cols = [slice(h * RET_DK, (h + 1) * RET_DK) for h in heads]
    cs = cos_ref[rows, :]
    sn = sin_ref[rows, :]
    qr, kr = [], []
    for h in heads:
        q = zq_ref[rows, cols[h]]
        k = zk_ref[rows, cols[h]]
        qr.append(q * cs + pltpu.roll(q, RET_DK // 2, 1) * sn)
        kr.append((k * cs + pltpu.roll(k, RET_DK // 2, 1) * sn) * (RET_DK ** -0.5))
    scores = [_dot_nt(qr[h].astype(BF16), kr[h].astype(BF16)) * dmat_ref[h] for h in heads]
    cross = [_dot((qr[h] * qdec_ref[h]).astype(BF16), state_ref[h].astype(BF16)) for h in heads]
    kd_t = [jnp.transpose(kr[h] * kdec_ref[h]).astype(BF16) for h in heads]
    for h in heads:
        v = zv_ref[rows, cols[h]]
        y = _dot(scores[h].astype(BF16), v) + cross[h]
        state_ref[h] = math.exp(_ret_log_g(h) * BLOCK) * state_ref[h] + _dot(kd_t[h], v)
        mu = jnp.mean(y, axis=-1, keepdims=True)
        d = y - mu
        var = jnp.mean(d * d, axis=-1, keepdims=True)
        yn = d * lax.rsqrt(var + LN_EPS)
        ya_ref[rows, cols[h]] = (jax.nn.silu(zg_ref[rows, cols[h]]) * yn).astype(ya_ref.dtype)


def _proj_kernel(x_ref, w_ref, b_ref, cos_ref, sin_ref, cw_ref, cb_ref, wa_ref, ba_ref, wx_ref, bx_ref,
                 lam_ref, zb_ref, zf_ref, ya_ref, yb_ref,
                 zq_ref, zk_ref, zg_ref, zv_ref, state_ref, dmat_ref, qdec_ref, kdec_ref,
                 xbuf_ref, zlg_ref, h_ref):
    @pl.when(pl.program_id(1) == 0)
    def _():
        xbuf_ref[0:8, :] = jnp.zeros((8, BRANCH_W), F32)
        h_ref[...] = jnp.zeros_like(h_ref)
        state_ref[...] = jnp.zeros_like(state_ref)
        row = lax.broadcasted_iota(jnp.int32, (BLOCK, BLOCK), 0).astype(F32)
        col = lax.broadcasted_iota(jnp.int32, (BLOCK, BLOCK), 1).astype(F32)
        diff = row - col
        for h in range(RET_HEADS):
            lg = _ret_log_g(h)
            dmat_ref[h] = jnp.where(diff >= 0, jnp.exp(lg * jnp.maximum(diff, 0.0)), 0.0)
            qdec_ref[h] = jnp.exp(lg * (row + 1.0))
            kdec_ref[h] = jnp.exp(lg * (BLOCK - 1.0 - row))

    xb = x_ref[...].astype(BF16)

    half = BRANCH_W // 2

    def project(part, dst_ref, row0, col0, scale=None):
        def piece(hf):
            cols = slice(part * BRANCH_W + hf * half, part * BRANCH_W + (hf + 1) * half)
            acc = _dot(xb, w_ref[:, cols]) + b_ref[:, cols]
            if scale is not None:
                acc = acc * scale
            dst_ref[row0:row0 + PROJ_TM, col0 + hf * half:col0 + (hf + 1) * half] = acc.astype(dst_ref.dtype)
        return [functools.partial(piece, 0), functools.partial(piece, 1)]

    def ret(c):
        _ret_chunk(slice(c * BLOCK, (c + 1) * BLOCK), zq_ref, zk_ref, zg_ref, zv_ref, cos_ref, sin_ref,
                   ya_ref, state_ref, dmat_ref, qdec_ref, kdec_ref)

    def lru(blk, hf):
        _lru_piece(blk * LRU_ROWS, hf, xbuf_ref, cw_ref, cb_ref, wa_ref, ba_ref, wx_ref, bx_ref, lam_ref,
                   zlg_ref, h_ref, yb_ref)

    for piece in project(W_LRU_X, xbuf_ref, 8, 0) + project(W_LRU_G, zlg_ref, 0, 0):
        piece()
    matmuls = (project(W_RET_Q, zq_ref, 0, 0) + project(W_RET_K, zk_ref, 0, 0)
               + project(W_RET_G, zg_ref, 0, 0) + project(W_RET_V, zv_ref, 0, 0)
               + project(W_SG_U, zf_ref, 0, ZF_SG_U * BRANCH_W) + project(W_SG_V, zf_ref, 0, ZF_SG_V * BRANCH_W)
               + project(W_SB_Q, zb_ref, 0, ZB_SB_Q * BRANCH_W, SB_HD ** -0.5)
               + project(W_SB_K, zb_ref, 0, ZB_SB_K * BRANCH_W) + project(W_SB_V, zb_ref, 0, ZB_SB_V * BRANCH_W))
    nblk = PROJ_TM // LRU_ROWS
    lru_pieces = [functools.partial(lru, blk, hf) for blk in range(nblk) for hf in range(2)]
    ret_pieces = [functools.partial(ret, c) for c in range(PROJ_TM // BLOCK)]
    branch = lru_pieces[:6] + [ret_pieces[0], lru_pieces[6], ret_pieces[1], lru_pieces[7]] + ret_pieces[2:]
    issued = 0
    for n, piece in enumerate(branch):
        piece()
        upto = min(len(matmuls), (3 * (n + 1) + 1) // 2)
        for m in matmuls[issued:upto]:
            m()
        issued = upto
    for m in matmuls[issued:]:
        m()
    xbuf_ref[0:8, :] = xbuf_ref[PROJ_TM:PROJ_TM + 8, :]


def _project(x2, w, b, cos_t, sin_t, conv_w, conv_b, wa_bd, ba, wx_bd, bx, lam, batch, seq):
    t, d = x2.shape
    nblk = seq // PROJ_TM
    row_map = lambda bi, n: (bi * nblk + n, 0)
    whole = functools.partial(pl.BlockSpec, pipeline_mode=pl.Buffered(1))
    tile = lambda width: pl.BlockSpec((PROJ_TM, width), row_map)
    vec = pl.BlockSpec((1, BRANCH_W), lambda bi, n: (0, 0))
    mat = pl.BlockSpec((2, BRANCH_W // 2, BRANCH_W // 2), lambda bi, n: (0, 0, 0))
    tile_f32 = pltpu.VMEM((PROJ_TM, BRANCH_W), F32)
    return pl.pallas_call(
        _proj_kernel,
        grid=(batch, nblk),
        in_specs=[
            tile(d),
            whole((d, W_COLS), lambda bi, n: (0, 0)),
            pl.BlockSpec((1, W_COLS), lambda bi, n: (0, 0)),
            pl.BlockSpec((PROJ_TM, RET_DK), lambda bi, n: (n, 0)),
            pl.BlockSpec((PROJ_TM, RET_DK), lambda bi, n: (n, 0)),
            pl.BlockSpec((CONV_W, BRANCH_W), lambda bi, n: (0, 0)),
            vec, mat, vec, mat, vec, vec,
        ],
        out_specs=[tile(ZB_COLS), tile(ZF_COLS), tile(BRANCH_W), tile(BRANCH_W)],
        out_shape=[jax.ShapeDtypeStruct((t, ZB_COLS), BF16), jax.ShapeDtypeStruct((t, ZF_COLS), F32),
                   jax.ShapeDtypeStruct((t, BRANCH_W), BF16), jax.ShapeDtypeStruct((t, BRANCH_W), BF16)],
        scratch_shapes=[tile_f32] * 3 + [pltpu.VMEM((PROJ_TM, BRANCH_W), BF16)]
        + [pltpu.VMEM((RET_HEADS, RET_DK, RET_DK), F32)] * 4
        + [pltpu.VMEM((PROJ_TM + 8, BRANCH_W), F32), tile_f32, pltpu.VMEM((8, BRANCH_W), F32)],
        compiler_params=_params("arbitrary", "arbitrary"),
        name="in_proj_retention_lru",
    )(x2, w, b, cos_t, sin_t, conv_w, conv_b, wa_bd, ba, wx_bd, bx, lam)


def _lru_piece(r0, hf, xbuf_ref, cw_ref, cb_ref, wa_ref, ba_ref, wx_ref, bx_ref, lam_ref, gate_ref,
               h_ref, o_ref):
    rows = LRU_ROWS
    half = BRANCH_W // 2
    lanes = slice(hf * half, (hf + 1) * half)
    xc = cb_ref[:, lanes]
    for tap in range(CONV_W):
        off = 8 + r0 - (CONV_W - 1) + tap
        xc = xc + cw_ref[tap:tap + 1, lanes] * xbuf_ref[off:off + rows, lanes]
    xcb = xc.astype(BF16)
    r = jax.nn.sigmoid(_dot(xcb, wa_ref[hf]) + ba_ref[:, lanes])
    i = jax.nn.sigmoid(_dot(xcb, wx_ref[hf]) + bx_ref[:, lanes])
    nlam = -lam_ref[:, lanes]
    softplus = jnp.maximum(nlam, 0.0) + jnp.log1p(jnp.exp(-jnp.abs(nlam)))
    log_a = -LRU_C * r * softplus
    a_all = jnp.exp(log_a)
    u_all = jnp.sqrt(jnp.tanh(-log_a) * (1.0 + a_all * a_all)) * (i * xc)

    sub = lax.broadcasted_iota(jnp.int32, (rows, 128), 0) & 7
    for s in range(half // 128):
        cols = slice(hf * half + s * 128, hf * half + (s + 1) * 128)
        a = a_all[:, s * 128:(s + 1) * 128]
        b = u_all[:, s * 128:(s + 1) * 128]
        for shift in (1, 2, 4):
            keep = sub >= shift
            a_sh = jnp.where(keep, pltpu.roll(a, shift, 0), 1.0)
            b_sh = jnp.where(keep, pltpu.roll(b, shift, 0), 0.0)
            b = a * b_sh + b
            a = a * a_sh
        hb = h_ref[:, cols]
        groups = []
        for g in range(rows // 8):
            hg = a[8 * g:8 * g + 8, :] * hb + b[8 * g:8 * g + 8, :]
            groups.append(hg)
            hb = jnp.broadcast_to(hg[7:8, :], (8, 128))
        h_ref[:, cols] = hb
        h = jnp.concatenate(groups, axis=0)
        o_ref[r0:r0 + rows, cols] = (h * _gelu(gate_ref[r0:r0 + rows, cols])).astype(o_ref.dtype)


def _sb_key_block(q_ref, k_ref, v_ref, acc_ref, carry_ref, tri, off, mask):
    t = SB_TQ
    heads = range(SB_HEADS)
    cols = [slice(h * SB_HD, (h + 1) * SB_HD) for h in heads]
    zs = [_dot_nt(q_ref[:, cols[h]], k_ref[pl.ds(off, t), cols[h]]) for h in heads]
    cums = []
    for h in heads:
        z = zs[h]
        lg = jnp.log(1.0 + jnp.exp2(jnp.abs(z) * (-LOG2E)))
        log_1m = jnp.minimum(-z, 0.0) - lg
        if mask is not None:
            log_1m = jnp.where(mask, log_1m, 0.0)
        hi = log_1m.astype(BF16)
        lo = (log_1m - hi.astype(F32)).astype(BF16)
        cums.append(_dot(hi, tri) + _dot(lo, tri))
    for h in heads:
        if mask is None:
            carry = carry_ref[h]
            w = jnp.exp(zs[h] + cums[h] + jnp.concatenate([carry] * (t // 128), axis=1))
            carry_ref[h] = carry + cums[h][:, 0:1]
            acc_ref[:, cols[h]] += _dot(w.astype(BF16), v_ref[pl.ds(off, t), cols[h]])
        else:
            w = jnp.where(mask, jnp.exp(zs[h] + cums[h]), 0.0)
            carry_ref[h] = jnp.broadcast_to(cums[h][:, 0:1], (t, 128))
            acc_ref[:, cols[h]] = _dot(w.astype(BF16), v_ref[pl.ds(off, t), cols[h]])


def _sb_kernel(q_ref, k_ref, v_ref, o_ref, acc_ref, carry_ref):
    i = pl.program_id(1)
    t = SB_TQ
    row = lax.broadcasted_iota(jnp.int32, (t, t), 0)
    col = lax.broadcasted_iota(jnp.int32, (t, t), 1)
    tri = (row >= col).astype(BF16)

    _sb_key_block(q_ref, k_ref, v_ref, acc_ref, carry_ref, tri, pl.multiple_of(i * t, t), col < row)

    def any_weight_left():
        m = carry_ref[0]
        for h in range(1, SB_HEADS):
            m = jnp.maximum(m, carry_ref[h])
        return jnp.max(m) > SB_EXP_ZERO

    def cond(state):
        step, live = state
        return jnp.logical_and(step < i, live)

    def body(state):
        step, _ = state
        off = pl.multiple_of((i - 1 - step) * t, t)
        _sb_key_block(q_ref, k_ref, v_ref, acc_ref, carry_ref, tri, off, None)
        return step + 1, any_weight_left()

    lax.while_loop(cond, body, (jnp.int32(0), any_weight_left()))
    o_ref[...] = acc_ref[...].astype(o_ref.dtype)


def _stick_breaking(zb, batch, seq):
    nq = seq // SB_TQ
    zb3 = zb.reshape(batch, seq, ZB_COLS)
    whole = functools.partial(pl.BlockSpec, pipeline_mode=pl.Buffered(1))
    return pl.pallas_call(
        _sb_kernel,
        grid=(batch, nq),
        in_specs=[
            pl.BlockSpec((None, SB_TQ, BRANCH_W), lambda b, i: (b, i, ZB_SB_Q)),
            whole((None, seq, BRANCH_W), lambda b, i: (b, 0, ZB_SB_K)),
            whole((None, seq, BRANCH_W), lambda b, i: (b, 0, ZB_SB_V)),
        ],
        out_specs=pl.BlockSpec((None, SB_TQ, BRANCH_W), lambda b, i: (b, i, 0)),
        out_shape=jax.ShapeDtypeStruct((batch, seq, BRANCH_W), BF16),
        scratch_shapes=[pltpu.VMEM((SB_TQ, BRANCH_W), F32), pltpu.VMEM((SB_HEADS, SB_TQ, 128), F32)],
        compiler_params=_params("arbitrary", "arbitrary"),
        name="stick_breaking",
    )(zb3, zb3, zb3).reshape(batch * seq, BRANCH_W)


def _sg_chunk(rows, u_ref, v_ref, g_ref, b_ref, ws_ref, bs_ref, o_ref):
    u = _gelu(u_ref[rows, :])
    v = _layer_norm(_gelu(v_ref[rows, :]), g_ref[...], b_ref[...]).astype(BF16)
    row = lax.broadcasted_iota(jnp.int32, (BLOCK, BLOCK), 0)
    col = lax.broadcasted_iota(jnp.int32, (BLOCK, BLOCK), 1)
    gd = BRANCH_W // SG_GROUPS
    for g in range(SG_GROUPS):
        w = jnp.where(col <= row, ws_ref[g], 0.0).astype(BF16)
        cols = slice(g * gd, (g + 1) * gd)
        s = _dot(w, v[:, cols]) + bs_ref[g]
        o_ref[rows, cols] = (u[:, cols] * s).astype(o_ref.dtype)


def _merge_kernel(ya_ref, yb_ref, yc_ref, u_ref, v_ref, x_ref, sg_g_ref, sg_b_ref, ws_ref, bs_ref,
                  wg_ref, bg_ref, wb_ref, wo_ref, lg_ref, lb_ref, o_ref, yd_ref):
    xb = x_ref[...].astype(BF16)
    nchunk = MERGE_TM // BLOCK
    sg_after = {0: range(0, nchunk // 2), 1: range(nchunk // 2, nchunk - 1), 2: range(nchunk - 1, nchunk)}
    merged = None
    for n, y_ref in enumerate((ya_ref, yb_ref, yc_ref, yd_ref)):
        cols = slice(n * D_MODEL, (n + 1) * D_MODEL)
        logits = _dot(xb, wg_ref[:, cols]) + bg_ref[:, cols]
        for c in sg_after.get(n, ()):
            _sg_chunk(slice(c * BLOCK, (c + 1) * BLOCK), u_ref, v_ref, sg_g_ref, sg_b_ref, ws_ref, bs_ref,
                      yd_ref)
        term = jax.nn.sigmoid(logits) * _dot(y_ref[...], wb_ref[n])
        merged = term if merged is None else merged + term
    mb = merged.astype(BF16)
    half = MERGE_TM // 2
    outs = [_dot(mb[r * half:(r + 1) * half, :], wo_ref[...]) for r in range(2)]
    for r in range(2):
        rows = slice(r * half, (r + 1) * half)
        o_ref[rows, :] = _layer_norm(ALPHA * x_ref[rows, :] + outs[r], lg_ref[...], lb_ref[...])


def _merge(ys, zf, x2, sg_ln_g, sg_ln_b, ws, bs_b, w_gate, b_gate, w_branch, w_out, ln_g, ln_b):
    t = x2.shape[0]
    ysp = pl.BlockSpec((MERGE_TM, BRANCH_W), lambda i: (i, 0))
    vec = pl.BlockSpec((1, D_MODEL), lambda i: (0, 0))
    hvec = pl.BlockSpec((1, BRANCH_W), lambda i: (0, 0))
    cube = pl.BlockSpec((SG_GROUPS, BLOCK, BLOCK), lambda i: (0, 0, 0))
    whole = functools.partial(pl.BlockSpec, pipeline_mode=pl.Buffered(1))
    return pl.pallas_call(
        _merge_kernel,
        grid=(t // MERGE_TM,),
        in_specs=[ysp] * 3 + [
            pl.BlockSpec((MERGE_TM, BRANCH_W), lambda i: (i, ZF_SG_U)),
            pl.BlockSpec((MERGE_TM, BRANCH_W), lambda i: (i, ZF_SG_V)),
            pl.BlockSpec((MERGE_TM, D_MODEL), lambda i: (i, 0)),
            hvec, hvec, cube, cube,
            whole((D_MODEL, N_BRANCH * D_MODEL), lambda i: (0, 0)),
            pl.BlockSpec((1, N_BRANCH * D_MODEL), lambda i: (0, 0)),
            whole((N_BRANCH, BRANCH_W, D_MODEL), lambda i: (0, 0, 0)),
            whole((D_MODEL, D_MODEL), lambda i: (0, 0)),
            vec, vec,
        ],
        out_specs=pl.BlockSpec((MERGE_TM, D_MODEL), lambda i: (i, 0)),
        out_shape=jax.ShapeDtypeStruct((t, D_MODEL), F32),
        scratch_shapes=[pltpu.VMEM((MERGE_TM, BRANCH_W), BF16)],
        compiler_params=_params("arbitrary"),
        name="sg_merge_out",
    )(*ys, zf, zf, x2, sg_ln_g, sg_ln_b, ws, bs_b, w_gate, b_gate, w_branch, w_out, ln_g, ln_b)


def _mlp_kernel(x_ref, w1_ref, b1_ref, w2_ref, b2_ref, lg_ref, lb_ref, o_ref):
    x = x_ref[...]
    xb = x.astype(BF16)
    acc = None
    for c in range(D_FF // MLP_FC):
        cols = slice(c * MLP_FC, (c + 1) * MLP_FC)
        hid = jnp.maximum(_dot(xb, w1_ref[:, cols]) + b1_ref[:, cols], 0.0)
        part = _dot((hid * hid).astype(BF16), w2_ref[cols, :])
        acc = part if acc is None else acc + part
    o_ref[...] = _layer_norm(ALPHA * x + (acc + b2_ref[...]), lg_ref[...], lb_ref[...])


def _mlp(x2, w1, b1, w2, b2, ln_g, ln_b):
    t = x2.shape[0]
    vec = pl.BlockSpec((1, D_MODEL), lambda i: (0, 0))
    return pl.pallas_call(
        _mlp_kernel,
        grid=(t // MLP_TM,),
        in_specs=[
            pl.BlockSpec((MLP_TM, D_MODEL), lambda i: (i, 0)),
            pl.BlockSpec((D_MODEL, D_FF), lambda i: (0, 0), pipeline_mode=pl.Buffered(1)),
            pl.BlockSpec((1, D_FF), lambda i: (0, 0)),
            pl.BlockSpec((D_FF, D_MODEL), lambda i: (0, 0), pipeline_mode=pl.Buffered(1)),
            vec, vec, vec,
        ],
        out_specs=pl.BlockSpec((MLP_TM, D_MODEL), lambda i: (i, 0)),
        out_shape=jax.ShapeDtypeStruct((t, D_MODEL), F32),
        compiler_params=_params("arbitrary"),
        name="mlp",
    )(x2, w1, b1, w2, b2, ln_g, ln_b)


def _block_diag(w):
    per = LRU_HEADS // 2
    w4 = w.reshape(2, per, LRU_HD, LRU_HD)
    eye = jnp.eye(per, dtype=w.dtype)
    return jnp.einsum('ghij,hk->ghikj', w4, eye).reshape(2, per * LRU_HD, per * LRU_HD)


def _rope_tables(seq):
    half = RET_DK // 2
    pos = jnp.arange(seq, dtype=F32)
    inv_freq = ROPE_THETA ** (-jnp.arange(half, dtype=F32) / half)
    ang = pos[:, None] * inv_freq[None, :]
    cos, sin = jnp.cos(ang), jnp.sin(ang)
    return jnp.concatenate([cos, cos], axis=-1), jnp.concatenate([-sin, sin], axis=-1)


def _layer(x2, batch, seq, cos_t, sin_t, w_in, b_in, conv_w, conv_b, lru_wa, lru_ba, lru_wx,
           lru_bx, lru_lambda, sg_ln_g, sg_ln_b, sg_ws, sg_bs, w_branch, w_out, ln1_g, ln1_b,
           w1, b1, w2, b2, ln2_g, ln2_b):
    gl0 = W_COLS
    zb, zf, y_a, y_b = _project(
        x2, w_in[:, :gl0].astype(BF16), b_in[None, :gl0], cos_t, sin_t, conv_w, conv_b[None, :],
        _block_diag(lru_wa).astype(BF16), lru_ba[None, :], _block_diag(lru_wx).astype(BF16),
        lru_bx[None, :], lru_lambda[None, :], batch, seq)
    y_c = _stick_breaking(zb, batch, seq)
    bs_b = jnp.broadcast_to(sg_bs[:, :, None], (SG_GROUPS, BLOCK, BRANCH_W // SG_GROUPS))

    x1 = _merge((y_a, y_b, y_c), zf, x2, sg_ln_g[None, :], sg_ln_b[None, :], sg_ws, bs_b,
                w_in[:, gl0:].astype(BF16), b_in[None, gl0:], w_branch.astype(BF16), w_out.astype(BF16),
                ln1_g[None, :], ln1_b[None, :])
    return _mlp(x1, w1.astype(BF16), b1[None, :], w2.astype(BF16), b2[None, :],
                ln2_g[None, :], ln2_b[None, :])


def kernel(x, w_in, b_in, conv_w, conv_b, lru_wa, lru_ba, lru_wx, lru_bx, lru_lambda, sg_ln_g, sg_ln_b, sg_ws, sg_bs, w_branch, w_out, ln1_g, ln1_b, w1, b1, w2, b2, ln2_g, ln2_b):
    batch, seq, d = x.shape
    cos_t, sin_t = _rope_tables(seq)
    x2 = x.reshape(batch * seq, d)
    per_layer = (w_in, b_in, conv_w, conv_b, lru_wa, lru_ba, lru_wx, lru_bx, lru_lambda, sg_ln_g,
                 sg_ln_b, sg_ws, sg_bs, w_branch, w_out, ln1_g, ln1_b, w1, b1, w2, b2, ln2_g, ln2_b)
    for l in range(DEPTH):
        x2 = _layer(x2, batch, seq, cos_t, sin_t, *(p[l] for p in per_layer))
    return x2.reshape(batch, seq, d)
```

```python
import functools
import math

import jax
import jax.numpy as jnp
from jax import lax
from jax.experimental import pallas as pl
from jax.experimental.pallas import tpu as pltpu

F32 = jnp.float32
BF16 = jnp.bfloat16

D_MODEL = 1024
DEPTH = 2
BLOCK = 128
BRANCH_W = D_MODEL // 2
N_BRANCH = 4
RET_HEADS = 4
RET_DK = BRANCH_W // RET_HEADS
ROPE_THETA = 10000.0
LRU_HEADS = 8
LRU_HD = BRANCH_W // LRU_HEADS
CONV_W = 4
LRU_C = 8.0
SB_HEADS = 4
SB_HD = BRANCH_W // SB_HEADS
SG_GROUPS = 4
D_FF = 4 * D_MODEL
ALPHA = (2 * DEPTH) ** 0.25
LN_EPS = 1e-5
LOG2E = 1.4426950408889634
SB_EXP_ZERO = -105.0

W_RET_Q, W_RET_K, W_RET_V, W_RET_G, W_LRU_X, W_LRU_G, W_SB_Q, W_SB_K, W_SB_V, W_SG_U, W_SG_V = range(11)
W_COLS = 11 * BRANCH_W
ZF_SG_U, ZF_SG_V = 0, 1
ZF_COLS = 2 * BRANCH_W
ZB_SB_Q, ZB_SB_K, ZB_SB_V = 0, 1, 2
ZB_COLS = 3 * BRANCH_W

VMEM_LIMIT = 56 * 1024 * 1024

PROJ_TM = 512
LRU_ROWS = 128
SB_TQ = 256
MERGE_TM = 512
MLP_TM = 1024
MLP_SUB = 512
MLP_FC = 1024


def _params(*sem):
    return pltpu.CompilerParams(dimension_semantics=sem, vmem_limit_bytes=VMEM_LIMIT)


def _dot(a, b):
    return jnp.dot(a, b, preferred_element_type=F32)


def _dot_nt(a, b):
    return lax.dot_general(a, b, (((1,), (1,)), ((), ())), preferred_element_type=F32)


def _gelu(x):
    k1 = -2.0 * math.sqrt(2.0 / math.pi) * LOG2E
    return x / (1.0 + jnp.exp2(x * (k1 + (k1 * 0.044715) * (x * x))))


def _layer_norm(h, g, b):
    mu = jnp.mean(h, axis=-1, keepdims=True)
    d = h - mu
    var = jnp.mean(d * d, axis=-1, keepdims=True)
    return d * lax.rsqrt(var + LN_EPS) * g + b


def _ret_log_g(h):
    return math.log1p(-(2.0 ** (-5.0 - h)))


def _ret_chunk(rows, zq_ref, zk_ref, zg_ref, zv_ref, cos_ref, sin_ref, ya_ref,
               state_ref, dmat_ref, qdec_ref, kdec_ref):
    heads = range(RET_HEADS)
    cols = [slice(h * RET_DK, (h + 1) * RET_DK) for h in heads]
    cs = cos_ref[rows, :]
    sn = sin_ref[rows, :]
    qr, kr = [], []
    for h in heads:
        q = zq_ref[rows, cols[h]]
        k = zk_ref[rows, cols[h]]
        qr.append(q * cs + pltpu.roll(q, RET_DK // 2, 1) * sn)
        kr.append((k * cs + pltpu.roll(k, RET_DK // 2, 1) * sn) * (RET_DK ** -0.5))
    scores = [_dot_nt(qr[h].astype(BF16), kr[h].astype(BF16)) * dmat_ref[h] for h in heads]
    cross = [_dot((qr[h] * qdec_ref[h]).astype(BF16), state_ref[h].astype(BF16)) for h in heads]
    kd_t = [jnp.transpose(kr[h] * kdec_ref[h]).astype(BF16) for h in heads]
    for h in heads:
        v = zv_ref[rows, cols[h]]
        y = _dot(scores[h].astype(BF16), v) + cross[h]
        state_ref[h] = math.exp(_ret_log_g(h) * BLOCK) * state_ref[h] + _dot(kd_t[h], v)
        mu = jnp.mean(y, axis=-1, keepdims=True)
        d = y - mu
        var = jnp.mean(d * d, axis=-1, keepdims=True)
        yn = d * lax.rsqrt(var + LN_EPS)
        ya_ref[rows, cols[h]] = (jax.nn.silu(zg_ref[rows, cols[h]]) * yn).astype(ya_ref.dtype)


def _proj_kernel(x_ref, w_ref, b_ref, cos_ref, sin_ref, cw_ref, cb_ref, wa_ref, ba_ref, wx_ref, bx_ref,
                 lam_ref, zb_ref, zf_ref, ya_ref, yb_ref,
                 zq_ref, zk_ref, zg_ref, zv_ref, state_ref, dmat_ref, qdec_ref, kdec_ref,
                 xbuf_ref, zlg_ref, h_ref):
    @pl.when(pl.program_id(1) == 0)
    def _():
        xbuf_ref[0:8, :] = jnp.zeros((8, BRANCH_W), F32)
        h_ref[...] = jnp.zeros_like(h_ref)
        state_ref[...] = jnp.zeros_like(state_ref)
        row = lax.broadcasted_iota(jnp.int32, (BLOCK, BLOCK), 0).astype(F32)
        col = lax.broadcasted_iota(jnp.int32, (BLOCK, BLOCK), 1).astype(F32)
        diff = row - col
        for h in range(RET_HEADS):
            lg = _ret_log_g(h)
            dmat_ref[h] = jnp.where(diff >= 0, jnp.exp(lg * jnp.maximum(diff, 0.0)), 0.0)
            qdec_ref[h] = jnp.exp(lg * (row + 1.0))
            kdec_ref[h] = jnp.exp(lg * (BLOCK - 1.0 - row))

    xb = x_ref[...].astype(BF16)

    half = BRANCH_W // 2

    def project(part, dst_ref, row0, col0, scale=None):
        def piece(hf):
            cols = slice(part * BRANCH_W + hf * half, part * BRANCH_W + (hf + 1) * half)
            acc = _dot(xb, w_ref[:, cols]) + b_ref[:, cols]
            if scale is not None:
                acc = acc * scale
            dst_ref[row0:row0 + PROJ_TM, col0 + hf * half:col0 + (hf + 1) * half] = acc.astype(dst_ref.dtype)
        return [functools.partial(piece, 0), functools.partial(piece, 1)]

    def ret(c):
        _ret_chunk(slice(c * BLOCK, (c + 1) * BLOCK), zq_ref, zk_ref, zg_ref, zv_ref, cos_ref, sin_ref,
                   ya_ref, state_ref, dmat_ref, qdec_ref, kdec_ref)

    def lru(blk, hf):
        _lru_piece(blk * LRU_ROWS, hf, xbuf_ref, cw_ref, cb_ref, wa_ref, ba_ref, wx_ref, bx_ref, lam_ref,
                   zlg_ref, h_ref, yb_ref)

    for piece in project(W_LRU_X, xbuf_ref, 8, 0) + project(W_LRU_G, zlg_ref, 0, 0):
        piece()
    matmuls = (project(W_RET_Q, zq_ref, 0, 0) + project(W_RET_K, zk_ref, 0, 0)
               + project(W_RET_G, zg_ref, 0, 0) + project(W_RET_V, zv_ref, 0, 0)
               + project(W_SG_U, zf_ref, 0, ZF_SG_U * BRANCH_W) + project(W_SG_V, zf_ref, 0, ZF_SG_V * BRANCH_W)
               + project(W_SB_Q, zb_ref, 0, ZB_SB_Q * BRANCH_W, SB_HD ** -0.5)
               + project(W_SB_K, zb_ref, 0, ZB_SB_K * BRANCH_W) + project(W_SB_V, zb_ref, 0, ZB_SB_V * BRANCH_W))
    nblk = PROJ_TM // LRU_ROWS
    lru_pieces = [functools.partial(lru, blk, hf) for blk in range(nblk) for hf in range(2)]
    ret_pieces = [functools.partial(ret, c) for c in range(PROJ_TM // BLOCK)]
    branch = lru_pieces[:6] + [ret_pieces[0], lru_pieces[6], ret_pieces[1], lru_pieces[7]] + ret_pieces[2:]
    issued = 0
    for n, piece in enumerate(branch):
        piece()
        upto = min(len(matmuls), (3 * (n + 1) + 1) // 2)
        for m in matmuls[issued:upto]:
            m()
        issued = upto
    for m in matmuls[issued:]:
        m()
    xbuf_ref[0:8, :] = xbuf_ref[PROJ_TM:PROJ_TM + 8, :]


def _project(x2, w, b, cos_t, sin_t, conv_w, conv_b, wa_bd, ba, wx_bd, bx, lam, batch, seq):
    t, d = x2.shape
    nblk = seq // PROJ_TM
    row_map = lambda bi, n: (bi * nblk + n, 0)
    whole = functools.partial(pl.BlockSpec, pipeline_mode=pl.Buffered(1))
    tile = lambda width: pl.BlockSpec((PROJ_TM, width), row_map)
    vec = pl.BlockSpec((1, BRANCH_W), lambda bi, n: (0, 0))
    mat = pl.BlockSpec((2, BRANCH_W // 2, BRANCH_W // 2), lambda bi, n: (0, 0, 0))
    tile_f32 = pltpu.VMEM((PROJ_TM, BRANCH_W), F32)
    return pl.pallas_call(
        _proj_kernel,
        grid=(batch, nblk),
        in_specs=[
            tile(d),
            whole((d, W_COLS), lambda bi, n: (0, 0)),
            pl.BlockSpec((1, W_COLS), lambda bi, n: (0, 0)),
            pl.BlockSpec((PROJ_TM, RET_DK), lambda bi, n: (n, 0)),
            pl.BlockSpec((PROJ_TM, RET_DK), lambda bi, n: (n, 0)),
            pl.BlockSpec((CONV_W, BRANCH_W), lambda bi, n: (0, 0)),
            vec, mat, vec, mat, vec, vec,
        ],
        out_specs=[tile(ZB_COLS), tile(ZF_COLS), tile(BRANCH_W), tile(BRANCH_W)],
        out_shape=[jax.ShapeDtypeStruct((t, ZB_COLS), BF16), jax.ShapeDtypeStruct((t, ZF_COLS), F32),
                   jax.ShapeDtypeStruct((t, BRANCH_W), BF16), jax.ShapeDtypeStruct((t, BRANCH_W), BF16)],
        scratch_shapes=[tile_f32] * 3 + [pltpu.VMEM((PROJ_TM, BRANCH_W), BF16)]
        + [pltpu.VMEM((RET_HEADS, RET_DK, RET_DK), F32)] * 4
        + [pltpu.VMEM((PROJ_TM + 8, BRANCH_W), F32), tile_f32, pltpu.VMEM((8, BRANCH_W), F32)],
        compiler_params=_params("arbitrary", "arbitrary"),
        name="in_proj_retention_lru",
    )(x2, w, b, cos_t, sin_t, conv_w, conv_b, wa_bd, ba, wx_bd, bx, lam)


def _lru_piece(r0, hf, xbuf_ref, cw_ref, cb_ref, wa_ref, ba_ref, wx_ref, bx_ref, lam_ref, gate_ref,
               h_ref, o_ref):
    rows = LRU_ROWS
    half = BRANCH_W // 2
    lanes = slice(hf * half, (hf + 1) * half)
    xc = cb_ref[:, lanes]
    for tap in range(CONV_W):
        off = 8 + r0 - (CONV_W - 1) + tap
        xc = xc + cw_ref[tap:tap + 1, lanes] * xbuf_ref[off:off + rows, lanes]
    xcb = xc.astype(BF16)
    r = jax.nn.sigmoid(_dot(xcb, wa_ref[hf]) + ba_ref[:, lanes])
    i = jax.nn.sigmoid(_dot(xcb, wx_ref[hf]) + bx_ref[:, lanes])
    nlam = -lam_ref[:, lanes]
    softplus = jnp.maximum(nlam, 0.0) + jnp.log1p(jnp.exp(-jnp.abs(nlam)))
    log_a = -LRU_C * r * softplus
    a_all = jnp.exp(log_a)
    u_all = jnp.sqrt(jnp.tanh(-log_a) * (1.0 + a_all * a_all)) * (i * xc)

    sub = lax.broadcasted_iota(jnp.int32, (rows, 128), 0) & 7
    for s in range(half // 128):
        cols = slice(hf * half + s * 128, hf * half + (s + 1) * 128)
        a = a_all[:, s * 128:(s + 1) * 128]
        b = u_all[:, s * 128:(s + 1) * 128]
        for shift in (1, 2, 4):
            keep = sub >= shift
            a_sh = jnp.where(keep, pltpu.roll(a, shift, 0), 1.0)
            b_sh = jnp.where(keep, pltpu.roll(b, shift, 0), 0.0)
            b = a * b_sh + b
            a = a * a_sh
        hb = h_ref[:, cols]
        groups = []
        for g in range(rows // 8):
            hg = a[8 * g:8 * g + 8, :] * hb + b[8 * g:8 * g + 8, :]
            groups.append(hg)
            hb = jnp.broadcast_to(hg[7:8, :], (8, 128))
        h_ref[:, cols] = hb
        h = jnp.concatenate(groups, axis=0)
        o_ref[r0:r0 + rows, cols] = (h * _gelu(gate_ref[r0:r0 + rows, cols])).astype(o_ref.dtype)


def _sb_key_blocks(q_ref, k_ref, v_ref, acc_ref, carry_ref, tri, offs, diag_mask):
    t = SB_TQ
    heads = range(SB_HEADS)
    blocks = range(len(offs))
    cols = [slice(h * SB_HD, (h + 1) * SB_HD) for h in heads]
    zs = [[_dot_nt(q_ref[:, cols[h]], k_ref[pl.ds(offs[b], t), cols[h]]) for b in blocks] for h in heads]
    cums = [[None] * len(offs) for _ in heads]
    for h in heads:
        for b in blocks:
            z = zs[h][b]
            lg = jnp.log(1.0 + jnp.exp2(jnp.abs(z) * (-LOG2E)))
            log_1m = jnp.minimum(-z, 0.0) - lg
            if b == 0 and diag_mask is not None:
                log_1m = jnp.where(diag_mask, log_1m, 0.0)
            hi = log_1m.astype(BF16)
            lo = (log_1m - hi.astype(F32)).astype(BF16)
            cums[h][b] = _dot(hi, tri) + _dot(lo, tri)
    for h in heads:
        carry = None if diag_mask is not None else carry_ref[h]
        pv = None
        for b in blocks:
            e = zs[h][b] + cums[h][b]
            if carry is not None:
                e = e + jnp.concatenate([carry] * (t // 128), axis=1)
            w = jnp.exp(e)
            if b == 0 and diag_mask is not None:
                w = jnp.where(diag_mask, w, 0.0)
            d = _dot(w.astype(BF16), v_ref[pl.ds(offs[b], t), cols[h]])
            pv = d if pv is None else pv + d
            block_sum = cums[h][b][:, 0:1]
            carry = jnp.broadcast_to(block_sum, (t, 128)) if carry is None else carry + block_sum
        carry_ref[h] = carry
        if diag_mask is not None:
            acc_ref[:, cols[h]] = pv
        else:
            acc_ref[:, cols[h]] += pv


def _sb_kernel(q_ref, k_ref, v_ref, o_ref, acc_ref, carry_ref):
    i = pl.program_id(1)
    t = SB_TQ
    row = lax.broadcasted_iota(jnp.int32, (t, t), 0)
    col = lax.broadcasted_iota(jnp.int32, (t, t), 1)
    tri = (row >= col).astype(BF16)
    diag = pl.multiple_of(i * t, t)

    @pl.when(i == 0)
    def _():
        _sb_key_blocks(q_ref, k_ref, v_ref, acc_ref, carry_ref, tri, [diag], col < row)

    @pl.when(i > 0)
    def _():
        prev = pl.multiple_of((i - 1) * t, t)
        _sb_key_blocks(q_ref, k_ref, v_ref, acc_ref, carry_ref, tri, [diag, prev], col < row)

    def any_weight_left():
        m = carry_ref[0]
        for h in range(1, SB_HEADS):
            m = jnp.maximum(m, carry_ref[h])
        return jnp.max(m) > SB_EXP_ZERO

    def cond(state):
        step, live = state
        return jnp.logical_and(step < i, live)

    def body(state):
        step, _ = state
        off = pl.multiple_of((i - 1 - step) * t, t)
        _sb_key_blocks(q_ref, k_ref, v_ref, acc_ref, carry_ref, tri, [off], None)
        return step + 1, any_weight_left()

    lax.while_loop(cond, body, (jnp.int32(1), any_weight_left()))
    o_ref[...] = acc_ref[...].astype(o_ref.dtype)


def _stick_breaking(zb, batch, seq):
    nq = seq // SB_TQ
    zb3 = zb.reshape(batch, seq, ZB_COLS)
    whole = functools.partial(pl.BlockSpec, pipeline_mode=pl.Buffered(1))
    return pl.pallas_call(
        _sb_kernel,
        grid=(batch, nq),
        in_specs=[
            pl.BlockSpec((None, SB_TQ, BRANCH_W), lambda b, i: (b, i, ZB_SB_Q)),
            whole((None, seq, BRANCH_W), lambda b, i: (b, 0, ZB_SB_K)),
            whole((None, seq, BRANCH_W), lambda b, i: (b, 0, ZB_SB_V)),
        ],
        out_specs=pl.BlockSpec((None, SB_TQ, BRANCH_W), lambda b, i: (b, i, 0)),
        out_shape=jax.ShapeDtypeStruct((batch, seq, BRANCH_W), BF16),
        scratch_shapes=[pltpu.VMEM((SB_TQ, BRANCH_W), F32), pltpu.VMEM((SB_HEADS, SB_TQ, 128), F32)],
        compiler_params=_params("arbitrary", "arbitrary"),
        name="stick_breaking",
    )(zb3, zb3, zb3).reshape(batch * seq, BRANCH_W)


def _sg_chunk(rows, u_ref, v_ref, g_ref, b_ref, ws_ref, bs_ref, o_ref):
    u = _gelu(u_ref[rows, :])
    v = _layer_norm(_gelu(v_ref[rows, :]), g_ref[...], b_ref[...]).astype(BF16)
    row = lax.broadcasted_iota(jnp.int32, (BLOCK, BLOCK), 0)
    col = lax.broadcasted_iota(jnp.int32, (BLOCK, BLOCK), 1)
    gd = BRANCH_W // SG_GROUPS
    for g in range(SG_GROUPS):
        w = jnp.where(col <= row, ws_ref[g], 0.0).astype(BF16)
        cols = slice(g * gd, (g + 1) * gd)
        s = _dot(w, v[:, cols]) + bs_ref[g]
        o_ref[rows, cols] = (u[:, cols] * s).astype(o_ref.dtype)


def _merge_kernel(ya_ref, yb_ref, yc_ref, u_ref, v_ref, x_ref, sg_g_ref, sg_b_ref, ws_ref, bs_ref,
                  wg_ref, bg_ref, wb_ref, wo_ref, lg_ref, lb_ref, o_ref, yd_ref):
    xb = x_ref[...].astype(BF16)
    nchunk = MERGE_TM // BLOCK
    sg_after = {0: range(0, nchunk // 2), 1: range(nchunk // 2, nchunk - 1), 2: range(nchunk - 1, nchunk)}
    merged = None
    for n, y_ref in enumerate((ya_ref, yb_ref, yc_ref, yd_ref)):
        cols = slice(n * D_MODEL, (n + 1) * D_MODEL)
        logits = _dot(xb, wg_ref[:, cols]) + bg_ref[:, cols]
        for c in sg_after.get(n, ()):
            _sg_chunk(slice(c * BLOCK, (c + 1) * BLOCK), u_ref, v_ref, sg_g_ref, sg_b_ref, ws_ref, bs_ref,
                      yd_ref)
        term = jax.nn.sigmoid(logits) * _dot(y_ref[...], wb_ref[n])
        merged = term if merged is None else merged + term
    mb = merged.astype(BF16)
    half = MERGE_TM // 2
    outs = [_dot(mb[r * half:(r + 1) * half, :], wo_ref[...]) for r in range(2)]
    for r in range(2):
        rows = slice(r * half, (r + 1) * half)
        o_ref[rows, :] = _layer_norm(ALPHA * x_ref[rows, :] + outs[r], lg_ref[...], lb_ref[...])


def _merge(ys, zf, x2, sg_ln_g, sg_ln_b, ws, bs_b, w_gate, b_gate, w_branch, w_out, ln_g, ln_b):
    t = x2.shape[0]
    ysp = pl.BlockSpec((MERGE_TM, BRANCH_W), lambda i: (i, 0))
    vec = pl.BlockSpec((1, D_MODEL), lambda i: (0, 0))
    hvec = pl.BlockSpec((1, BRANCH_W), lambda i: (0, 0))
    cube = pl.BlockSpec((SG_GROUPS, BLOCK, BLOCK), lambda i: (0, 0, 0))
    whole = functools.partial(pl.BlockSpec, pipeline_mode=pl.Buffered(1))
    return pl.pallas_call(
        _merge_kernel,
        grid=(t // MERGE_TM,),
        in_specs=[ysp] * 3 + [
            pl.BlockSpec((MERGE_TM, BRANCH_W), lambda i: (i, ZF_SG_U)),
            pl.BlockSpec((MERGE_TM, BRANCH_W), lambda i: (i, ZF_SG_V)),
            pl.BlockSpec((MERGE_TM, D_MODEL), lambda i: (i, 0)),
            hvec, hvec, cube, cube,
            whole((D_MODEL, N_BRANCH * D_MODEL), lambda i: (0, 0)),
            pl.BlockSpec((1, N_BRANCH * D_MODEL), lambda i: (0, 0)),
            whole((N_BRANCH, BRANCH_W, D_MODEL), lambda i: (0, 0, 0)),
            whole((D_MODEL, D_MODEL), lambda i: (0, 0)),
            vec, vec,
        ],
        out_specs=pl.BlockSpec((MERGE_TM, D_MODEL), lambda i: (i, 0)),
        out_shape=jax.ShapeDtypeStruct((t, D_MODEL), F32),
        scratch_shapes=[pltpu.VMEM((MERGE_TM, BRANCH_W), BF16)],
        compiler_params=_params("arbitrary"),
        name="sg_merge_out",
    )(*ys, zf, zf, x2, sg_ln_g, sg_ln_b, ws, bs_b, w_gate, b_gate, w_branch, w_out, ln_g, ln_b)


def _mlp_kernel(x_ref, w1_ref, b1_ref, w2_ref, b2_ref, lg_ref, lb_ref, o_ref):
    def finish(rows, acc):
        o_ref[rows, :] = _layer_norm(ALPHA * x_ref[rows, :] + (acc + b2_ref[...]), lg_ref[...], lb_ref[...])

    done = []
    for r in range(MLP_TM // MLP_SUB):
        rows = slice(r * MLP_SUB, (r + 1) * MLP_SUB)
        xb = x_ref[rows, :].astype(BF16)
        acc = None
        for c in range(D_FF // MLP_FC):
            cols = slice(c * MLP_FC, (c + 1) * MLP_FC)
            hid = jnp.maximum(_dot(xb, w1_ref[:, cols]) + b1_ref[:, cols], 0.0)
            part = _dot((hid * hid).astype(BF16), w2_ref[cols, :])
            acc = part if acc is None else acc + part
        done.append((rows, acc))
    for rows, acc in done:
        finish(rows, acc)


def _mlp(x2, w1, b1, w2, b2, ln_g, ln_b):
    t = x2.shape[0]
    vec = pl.BlockSpec((1, D_MODEL), lambda i: (0, 0))
    return pl.pallas_call(
        _mlp_kernel,
        grid=(t // MLP_TM,),
        in_specs=[
            pl.BlockSpec((MLP_TM, D_MODEL), lambda i: (i, 0)),
            pl.BlockSpec((D_MODEL, D_FF), lambda i: (0, 0), pipeline_mode=pl.Buffered(1)),
            pl.BlockSpec((1, D_FF), lambda i: (0, 0)),
            pl.BlockSpec((D_FF, D_MODEL), lambda i: (0, 0), pipeline_mode=pl.Buffered(1)),
            vec, vec, vec,
        ],
        out_specs=pl.BlockSpec((MLP_TM, D_MODEL), lambda i: (i, 0)),
        out_shape=jax.ShapeDtypeStruct((t, D_MODEL), F32),
        compiler_params=_params("arbitrary"),
        name="mlp",
    )(x2, w1, b1, w2, b2, ln_g, ln_b)


def _block_diag(w):
    per = LRU_HEADS // 2
    w4 = w.reshape(2, per, LRU_HD, LRU_HD)
    eye = jnp.eye(per, dtype=w.dtype)
    return jnp.einsum('ghij,hk->ghikj', w4, eye).reshape(2, per * LRU_HD, per * LRU_HD)


def _rope_tables(seq):
    half = RET_DK // 2
    pos = jnp.arange(seq, dtype=F32)
    inv_freq = ROPE_THETA ** (-jnp.arange(half, dtype=F32) / half)
    ang = pos[:, None] * inv_freq[None, :]
    cos, sin = jnp.cos(ang), jnp.sin(ang)
    return jnp.concatenate([cos, cos], axis=-1), jnp.concatenate([-sin, sin], axis=-1)


def _layer(x2, batch, seq, cos_t, sin_t, w_in, b_in, conv_w, conv_b, lru_wa, lru_ba, lru_wx,
           lru_bx, lru_lambda, sg_ln_g, sg_ln_b, sg_ws, sg_bs, w_branch, w_out, ln1_g, ln1_b,
           w1, b1, w2, b2, ln2_g, ln2_b):
    gl0 = W_COLS
    zb, zf, y_a, y_b = _project(
        x2, w_in[:, :gl0].astype(BF16), b_in[None, :gl0], cos_t, sin_t, conv_w, conv_b[None, :],
        _block_diag(lru_wa).astype(BF16), lru_ba[None, :], _block_diag(lru_wx).astype(BF16),
        lru_bx[None, :], lru_lambda[None, :], batch, seq)
    y_c = _stick_breaking(zb, batch, seq)
    bs_b = jnp.broadcast_to(sg_bs[:, :, None], (SG_GROUPS, BLOCK, BRANCH_W // SG_GROUPS))

    x1 = _merge((y_a, y_b, y_c), zf, x2, sg_ln_g[None, :], sg_ln_b[None, :], sg_ws, bs_b,
                w_in[:, gl0:].astype(BF16), b_in[None, gl0:], w_branch.astype(BF16), w_out.astype(BF16),
                ln1_g[None, :], ln1_b[None, :])
    return _mlp(x1, w1.astype(BF16), b1[None, :], w2.astype(BF16), b2[None, :],
                ln2_g[None, :], ln2_b[None, :])


def kernel(x, w_in, b_in, conv_w, conv_b, lru_wa, lru_ba, lru_wx, lru_bx, lru_lambda, sg_ln_g, sg_ln_b, sg_ws, sg_bs, w_branch, w_out, ln1_g, ln1_b, w1, b1, w2, b2, ln2_g, ln2_b):
    batch, seq, d = x.shape
    cos_t, sin_t = _rope_tables(seq)
    x2 = x.reshape(batch * seq, d)
    per_layer = (w_in, b_in, conv_w, conv_b, lru_wa, lru_ba, lru_wx, lru_bx, lru_lambda, sg_ln_g,
                 sg_ln_b, sg_ws, sg_bs, w_branch, w_out, ln1_g, ln1_b, w1, b1, w2, b2, ln2_g, ln2_b)
    for l in range(DEPTH):
        x2 = _layer(x2, batch, seq, cos_t, sin_t, *(p[l] for p in per_layer))
    return x2.reshape(batch, seq, d)
```

```python
import functools
import math

import jax
import jax.numpy as jnp
from jax import lax
from jax.experimental import pallas as pl
from jax.experimental.pallas import tpu as pltpu

F32 = jnp.float32
BF16 = jnp.bfloat16

D_MODEL = 1024
DEPTH = 2
BLOCK = 128
BRANCH_W = D_MODEL // 2
N_BRANCH = 4
RET_HEADS = 4
RET_DK = BRANCH_W // RET_HEADS
ROPE_THETA = 10000.0
LRU_HEADS = 8
LRU_HD = BRANCH_W // LRU_HEADS
CONV_W = 4
LRU_C = 8.0
SB_HEADS = 4
SB_HD = BRANCH_W // SB_HEADS
SG_GROUPS = 4
D_FF = 4 * D_MODEL
ALPHA = (2 * DEPTH) ** 0.25
LN_EPS = 1e-5
LOG2E = 1.4426950408889634
SB_EXP_ZERO = -105.0

W_RET_Q, W_RET_K, W_RET_V, W_RET_G, W_LRU_X, W_LRU_G, W_SB_Q, W_SB_K, W_SB_V, W_SG_U, W_SG_V = range(11)
W_COLS = 11 * BRANCH_W
ZF_SG_U, ZF_SG_V = 0, 1
ZF_COLS = 2 * BRANCH_W
ZB_SB_Q, ZB_SB_K, ZB_SB_V = 0, 1, 2
ZB_COLS = 3 * BRANCH_W

VMEM_LIMIT = 56 * 1024 * 1024

PROJ_TM = 512
LRU_ROWS = 128
SB_TQ = 256
MERGE_TM = 512
MLP_TM = 1024
MLP_SUB = 512
MLP_FC = 1024


def _params(*sem):
    return pltpu.CompilerParams(dimension_semantics=sem, vmem_limit_bytes=VMEM_LIMIT)


def _dot(a, b):
    return jnp.dot(a, b, preferred_element_type=F32)


def _dot_nt(a, b):
    return lax.dot_general(a, b, (((1,), (1,)), ((), ())), preferred_element_type=F32)


def _gelu(x):
    k1 = -2.0 * math.sqrt(2.0 / math.pi) * LOG2E
    return x / (1.0 + jnp.exp2(x * (k1 + (k1 * 0.044715) * (x * x))))


def _layer_norm(h, g, b):
    mu = jnp.mean(h, axis=-1, keepdims=True)
    d = h - mu
    var = jnp.mean(d * d, axis=-1, keepdims=True)
    return d * lax.rsqrt(var + LN_EPS) * g + b


def _ret_log_g(h):
    return math.log1p(-(2.0 ** (-5.0 - h)))


def _ret_chunk(rows, zq_ref, zk_ref, zg_ref, zv_ref, cos_ref, sin_ref, ya_ref,
               state_ref, dmat_ref, qdec_ref, kdec_ref):
    heads = range(RET_HEADS)
    cols = [slice(h * RET_DK, (h + 1) * RET_DK) for h in heads]
    cs = cos_ref[rows, :]
    sn = sin_ref[rows, :]
    qr, kr = [], []
    for h in heads:
        q = zq_ref[rows, cols[h]]
        k = zk_ref[rows, cols[h]]
        qr.append(q * cs + pltpu.roll(q, RET_DK // 2, 1) * sn)
        kr.append((k * cs + pltpu.roll(k, RET_DK // 2, 1) * sn) * (RET_DK ** -0.5))
    scores = [_dot_nt(qr[h].astype(BF16), kr[h].astype(BF16)) * dmat_ref[h] for h in heads]
    cross = [_dot((qr[h] * qdec_ref[h]).astype(BF16), state_ref[h].astype(BF16)) for h in heads]
    kd_t = [jnp.transpose(kr[h] * kdec_ref[h]).astype(BF16) for h in heads]
    for h in heads:
        v = zv_ref[rows, cols[h]]
        y = _dot(scores[h].astype(BF16), v) + cross[h]
        state_ref[h] = math.exp(_ret_log_g(h) * BLOCK) * state_ref[h] + _dot(kd_t[h], v)
        mu = jnp.mean(y, axis=-1, keepdims=True)
        d = y - mu
        var = jnp.mean(d * d, axis=-1, keepdims=True)
        yn = d * lax.rsqrt(var + LN_EPS)
        ya_ref[rows, cols[h]] = (jax.nn.silu(zg_ref[rows, cols[h]]) * yn).astype(ya_ref.dtype)


def _proj_kernel(x_ref, w_ref, b_ref, cos_ref, sin_ref, cw_ref, cb_ref, wa_ref, ba_ref, wx_ref, bx_ref,
                 lam_ref, zb_ref, zf_ref, ya_ref, yb_ref,
                 zq_ref, zk_ref, zg_ref, zv_ref, state_ref, dmat_ref, qdec_ref, kdec_ref,
                 zlx_ref, zlg_ref, tail_ref, h_ref):
    @pl.when(pl.program_id(1) == 0)
    def _():
        tail_ref[...] = jnp.zeros_like(tail_ref)
        h_ref[...] = jnp.zeros_like(h_ref)
        state_ref[...] = jnp.zeros_like(state_ref)
        row = lax.broadcasted_iota(jnp.int32, (BLOCK, BLOCK), 0).astype(F32)
        col = lax.broadcasted_iota(jnp.int32, (BLOCK, BLOCK), 1).astype(F32)
        diff = row - col
        for h in range(RET_HEADS):
            lg = _ret_log_g(h)
            dmat_ref[h] = jnp.where(diff >= 0, jnp.exp(lg * jnp.maximum(diff, 0.0)), 0.0)
            qdec_ref[h] = jnp.exp(lg * (row + 1.0))
            kdec_ref[h] = jnp.exp(lg * (BLOCK - 1.0 - row))

    xb = x_ref[...].astype(BF16)

    half = BRANCH_W // 2

    def project(part, dst_ref, row0, col0, scale=None):
        def piece(hf):
            cols = slice(part * BRANCH_W + hf * half, part * BRANCH_W + (hf + 1) * half)
            acc = _dot(xb, w_ref[:, cols]) + b_ref[:, cols]
            if scale is not None:
                acc = acc * scale
            dst_ref[row0:row0 + PROJ_TM, col0 + hf * half:col0 + (hf + 1) * half] = acc.astype(dst_ref.dtype)
        return [functools.partial(piece, 0), functools.partial(piece, 1)]

    def ret(c):
        _ret_chunk(slice(c * BLOCK, (c + 1) * BLOCK), zq_ref, zk_ref, zg_ref, zv_ref, cos_ref, sin_ref,
                   ya_ref, state_ref, dmat_ref, qdec_ref, kdec_ref)

    def lru(blk, hf):
        _lru_piece(blk * LRU_ROWS, hf, zlx_ref, cw_ref, cb_ref, wa_ref, ba_ref, wx_ref, bx_ref, lam_ref,
                   zlg_ref, tail_ref, h_ref, yb_ref)

    for piece in project(W_LRU_X, zlx_ref, 0, 0) + project(W_LRU_G, zlg_ref, 0, 0):
        piece()
    matmuls = (project(W_RET_Q, zq_ref, 0, 0) + project(W_RET_K, zk_ref, 0, 0)
               + project(W_RET_G, zg_ref, 0, 0) + project(W_RET_V, zv_ref, 0, 0)
               + project(W_SG_U, zf_ref, 0, ZF_SG_U * BRANCH_W) + project(W_SG_V, zf_ref, 0, ZF_SG_V * BRANCH_W)
               + project(W_SB_Q, zb_ref, 0, ZB_SB_Q * BRANCH_W, SB_HD ** -0.5)
               + project(W_SB_K, zb_ref, 0, ZB_SB_K * BRANCH_W) + project(W_SB_V, zb_ref, 0, ZB_SB_V * BRANCH_W))
    nblk = PROJ_TM // LRU_ROWS
    lru_pieces = [functools.partial(lru, blk, hf) for blk in range(nblk) for hf in range(2)]
    ret_pieces = [functools.partial(ret, c) for c in range(PROJ_TM // BLOCK)]
    branch = lru_pieces[:6] + [ret_pieces[0], lru_pieces[6], ret_pieces[1], lru_pieces[7]] + ret_pieces[2:]
    issued = 0
    for n, piece in enumerate(branch):
        piece()
        upto = min(len(matmuls), (3 * (n + 1) + 1) // 2)
        for m in matmuls[issued:upto]:
            m()
        issued = upto
    for m in matmuls[issued:]:
        m()


def _project(x2, w, b, cos_t, sin_t, conv_w, conv_b, wa_bd, ba, wx_bd, bx, lam, batch, seq):
    t, d = x2.shape
    nblk = seq // PROJ_TM
    row_map = lambda bi, n: (bi * nblk + n, 0)
    whole = functools.partial(pl.BlockSpec, pipeline_mode=pl.Buffered(1))
    tile = lambda width: pl.BlockSpec((PROJ_TM, width), row_map)
    vec = pl.BlockSpec((1, BRANCH_W), lambda bi, n: (0, 0))
    mat = pl.BlockSpec((2, BRANCH_W // 2, BRANCH_W // 2), lambda bi, n: (0, 0, 0))
    tile_f32 = pltpu.VMEM((PROJ_TM, BRANCH_W), F32)
    return pl.pallas_call(
        _proj_kernel,
        grid=(batch, nblk),
        in_specs=[
            tile(d),
            whole((d, W_COLS), lambda bi, n: (0, 0)),
            pl.BlockSpec((1, W_COLS), lambda bi, n: (0, 0)),
            pl.BlockSpec((PROJ_TM, RET_DK), lambda bi, n: (n, 0)),
            pl.BlockSpec((PROJ_TM, RET_DK), lambda bi, n: (n, 0)),
            pl.BlockSpec((CONV_W, BRANCH_W), lambda bi, n: (0, 0)),
            vec, mat, vec, mat, vec, vec,
        ],
        out_specs=[tile(ZB_COLS), tile(ZF_COLS), tile(BRANCH_W), tile(BRANCH_W)],
        out_shape=[jax.ShapeDtypeStruct((t, ZB_COLS), BF16), jax.ShapeDtypeStruct((t, ZF_COLS), F32),
                   jax.ShapeDtypeStruct((t, BRANCH_W), BF16), jax.ShapeDtypeStruct((t, BRANCH_W), BF16)],
        scratch_shapes=[tile_f32] * 3 + [pltpu.VMEM((PROJ_TM, BRANCH_W), BF16)]
        + [pltpu.VMEM((RET_HEADS, RET_DK, RET_DK), F32)] * 4
        + [tile_f32, tile_f32, pltpu.VMEM((8 * (CONV_W - 1), BRANCH_W), F32), pltpu.VMEM((8, BRANCH_W), F32)],
        compiler_params=_params("arbitrary", "arbitrary"),
        name="in_proj_retention_lru",
    )(x2, w, b, cos_t, sin_t, conv_w, conv_b, wa_bd, ba, wx_bd, bx, lam)


def _lru_piece(r0, hf, zlx_ref, cw_ref, cb_ref, wa_ref, ba_ref, wx_ref, bx_ref, lam_ref, gate_ref,
               tail_ref, h_ref, o_ref):
    rows = LRU_ROWS
    per = rows // 8
    half = BRANCH_W // 2
    lanes = slice(hf * half, (hf + 1) * half)
    xp = pltpu.einshape("(sj)c->(js)c", zlx_ref[r0:r0 + rows, lanes], s=8)
    sub = lax.broadcasted_iota(jnp.int32, (8, half), 0)

    def delayed(k):
        heads = []
        for m in range(per - k, per):
            prev = tail_ref[8 * (m - (per - 3)):8 * (m - (per - 3)) + 8, lanes]
            heads.append(pltpu.roll(jnp.where(sub == 7, prev, xp[8 * m:8 * m + 8, :]), 1, 0))
        return jnp.concatenate(heads + [xp[0:8 * (per - k), :]], axis=0)

    xc = cb_ref[:, lanes] + cw_ref[CONV_W - 1:CONV_W, lanes] * xp
    for k in range(1, CONV_W):
        xc = xc + cw_ref[CONV_W - 1 - k:CONV_W - k, lanes] * delayed(k)
    tail_ref[:, lanes] = xp[8 * (per - 3):8 * per, :]

    xcb = xc.astype(BF16)
    r = jax.nn.sigmoid(_dot(xcb, wa_ref[hf]) + ba_ref[:, lanes])
    i = jax.nn.sigmoid(_dot(xcb, wx_ref[hf]) + bx_ref[:, lanes])
    nlam = -lam_ref[:, lanes]
    softplus = jnp.maximum(nlam, 0.0) + jnp.log1p(jnp.exp(-jnp.abs(nlam)))
    log_a = -LRU_C * r * softplus
    a_all = jnp.exp(log_a)
    u_all = jnp.sqrt(jnp.tanh(-log_a) * (1.0 + a_all * a_all)) * (i * xc)

    sub1 = sub[:, 0:128]
    slabs = []
    for s in range(half // 128):
        sl = slice(s * 128, (s + 1) * 128)
        cols = slice(hf * half + s * 128, hf * half + (s + 1) * 128)
        run_h, run_p = [u_all[0:8, sl]], [a_all[0:8, sl]]
        for j in range(1, per):
            aj = a_all[8 * j:8 * j + 8, sl]
            run_h.append(aj * run_h[-1] + u_all[8 * j:8 * j + 8, sl])
            run_p.append(aj * run_p[-1])
        tot_h, tot_p = run_h[-1], run_p[-1]
        for shift in (1, 2, 4):
            keep = sub1 >= shift
            p_sh = jnp.where(keep, pltpu.roll(tot_p, shift, 0), 1.0)
            h_sh = jnp.where(keep, pltpu.roll(tot_h, shift, 0), 0.0)
            tot_h = tot_p * h_sh + tot_h
            tot_p = tot_p * p_sh
        hb = h_ref[:, cols]
        ends = tot_h + tot_p * hb
        start = jnp.where(sub1 == 0, hb, pltpu.roll(ends, 1, 0))
        h_ref[:, cols] = jnp.broadcast_to(ends[7:8, :], (8, 128))
        slabs.append(jnp.concatenate([run_h[j] + run_p[j] * start for j in range(per)], axis=0))
    h = pltpu.einshape("(js)c->(sj)c", jnp.concatenate(slabs, axis=1), s=8)
    o_ref[r0:r0 + rows, lanes] = (h * _gelu(gate_ref[r0:r0 + rows, lanes])).astype(o_ref.dtype)


def _sb_key_blocks(q_ref, k_ref, v_ref, acc_ref, carry_ref, tri, offs, diag_mask):
    t = SB_TQ
    heads = range(SB_HEADS)
    blocks = range(len(offs))
    cols = [slice(h * SB_HD, (h + 1) * SB_HD) for h in heads]
    zs = [[_dot_nt(q_ref[:, cols[h]], k_ref[pl.ds(offs[b], t), cols[h]]) for b in blocks] for h in heads]
    cums = [[None] * len(offs) for _ in heads]
    for h in heads:
        for b in blocks:
            z = zs[h][b]
            lg = jnp.log(1.0 + jnp.exp2(jnp.abs(z) * (-LOG2E)))
            log_1m = jnp.minimum(-z, 0.0) - lg
            if b == 0 and diag_mask is not None:
                log_1m = jnp.where(diag_mask, log_1m, 0.0)
            hi = log_1m.astype(BF16)
            lo = (log_1m - hi.astype(F32)).astype(BF16)
            cums[h][b] = _dot(hi, tri) + _dot(lo, tri)
    for h in heads:
        carry = None if diag_mask is not None else carry_ref[h]
        pv = None
        for b in blocks:
            e = zs[h][b] + cums[h][b]
            if carry is not None:
                e = e + jnp.concatenate([carry] * (t // 128), axis=1)
            w = jnp.exp(e)
            if b == 0 and diag_mask is not None:
                w = jnp.where(diag_mask, w, 0.0)
            d = _dot(w.astype(BF16), v_ref[pl.ds(offs[b], t), cols[h]])
            pv = d if pv is None else pv + d
            block_sum = cums[h][b][:, 0:1]
            carry = jnp.broadcast_to(block_sum, (t, 128)) if carry is None else carry + block_sum
        carry_ref[h] = carry
        if diag_mask is not None:
            acc_ref[:, cols[h]] = pv
        else:
            acc_ref[:, cols[h]] += pv


def _sb_kernel(q_ref, k_ref, v_ref, o_ref, acc_ref, carry_ref):
    i = pl.program_id(1)
    t = SB_TQ
    row = lax.broadcasted_iota(jnp.int32, (t, t), 0)
    col = lax.broadcasted_iota(jnp.int32, (t, t), 1)
    tri = (row >= col).astype(BF16)
    diag = pl.multiple_of(i * t, t)

    @pl.when(i == 0)
    def _():
        _sb_key_blocks(q_ref, k_ref, v_ref, acc_ref, carry_ref, tri, [diag], col < row)

    @pl.when(i > 0)
    def _():
        prev = pl.multiple_of((i - 1) * t, t)
        _sb_key_blocks(q_ref, k_ref, v_ref, acc_ref, carry_ref, tri, [diag, prev], col < row)

    def any_weight_left():
        m = carry_ref[0]
        for h in range(1, SB_HEADS):
            m = jnp.maximum(m, carry_ref[h])
        return jnp.max(m) > SB_EXP_ZERO

    def cond(state):
        step, live = state
        return jnp.logical_and(step < i, live)

    def body(state):
        step, _ = state
        off = pl.multiple_of((i - 1 - step) * t, t)
        _sb_key_blocks(q_ref, k_ref, v_ref, acc_ref, carry_ref, tri, [off], None)
        return step + 1, any_weight_left()

    lax.while_loop(cond, body, (jnp.int32(1), any_weight_left()))
    o_ref[...] = acc_ref[...].astype(o_ref.dtype)


def _stick_breaking(zb, batch, seq):
    nq = seq // SB_TQ
    zb3 = zb.reshape(batch, seq, ZB_COLS)
    whole = functools.partial(pl.BlockSpec, pipeline_mode=pl.Buffered(1))
    return pl.pallas_call(
        _sb_kernel,
        grid=(batch, nq),
        in_specs=[
            pl.BlockSpec((None, SB_TQ, BRANCH_W), lambda b, i: (b, i, ZB_SB_Q)),
            whole((None, seq, BRANCH_W), lambda b, i: (b, 0, ZB_SB_K)),
            whole((None, seq, BRANCH_W), lambda b, i: (b, 0, ZB_SB_V)),
        ],
        out_specs=pl.BlockSpec((None, SB_TQ, BRANCH_W), lambda b, i: (b, i, 0)),
        out_shape=jax.ShapeDtypeStruct((batch, seq, BRANCH_W), BF16),
        scratch_shapes=[pltpu.VMEM((SB_TQ, BRANCH_W), F32), pltpu.VMEM((SB_HEADS, SB_TQ, 128), F32)],
        compiler_params=_params("arbitrary", "arbitrary"),
        name="stick_breaking",
    )(zb3, zb3, zb3).reshape(batch * seq, BRANCH_W)


def _sg_chunk(rows, u_ref, v_ref, g_ref, b_ref, ws_ref, bs_ref, o_ref):
    u = _gelu(u_ref[rows, :])
    v = _layer_norm(_gelu(v_ref[rows, :]), g_ref[...], b_ref[...]).astype(BF16)
    row = lax.broadcasted_iota(jnp.int32, (BLOCK, BLOCK), 0)
    col = lax.broadcasted_iota(jnp.int32, (BLOCK, BLOCK), 1)
    gd = BRANCH_W // SG_GROUPS
    for g in range(SG_GROUPS):
        w = jnp.where(col <= row, ws_ref[g], 0.0).astype(BF16)
        cols = slice(g * gd, (g + 1) * gd)
        s = _dot(w, v[:, cols]) + bs_ref[g]
        o_ref[rows, cols] = (u[:, cols] * s).astype(o_ref.dtype)


def _merge_kernel(ya_ref, yb_ref, yc_ref, u_ref, v_ref, x_ref, sg_g_ref, sg_b_ref, ws_ref, bs_ref,
                  wg_ref, bg_ref, wb_ref, wo_ref, lg_ref, lb_ref, o_ref, yd_ref):
    xb = x_ref[...].astype(BF16)
    nchunk = MERGE_TM // BLOCK
    sg_after = {0: range(0, nchunk // 2), 1: range(nchunk // 2, nchunk - 1), 2: range(nchunk - 1, nchunk)}
    merged = None
    for n, y_ref in enumerate((ya_ref, yb_ref, yc_ref, yd_ref)):
        cols = slice(n * D_MODEL, (n + 1) * D_MODEL)
        logits = _dot(xb, wg_ref[:, cols]) + bg_ref[:, cols]
        for c in sg_after.get(n, ()):
            _sg_chunk(slice(c * BLOCK, (c + 1) * BLOCK), u_ref, v_ref, sg_g_ref, sg_b_ref, ws_ref, bs_ref,
                      yd_ref)
        term = jax.nn.sigmoid(logits) * _dot(y_ref[...], wb_ref[n])
        merged = term if merged is None else merged + term
    mb = merged.astype(BF16)
    half = MERGE_TM // 2
    outs = [_dot(mb[r * half:(r + 1) * half, :], wo_ref[...]) for r in range(2)]
    for r in range(2):
        rows = slice(r * half, (r + 1) * half)
        o_ref[rows, :] = _layer_norm(ALPHA * x_ref[rows, :] + outs[r], lg_ref[...], lb_ref[...])


def _merge(ys, zf, x2, sg_ln_g, sg_ln_b, ws, bs_b, w_gate, b_gate, w_branch, w_out, ln_g, ln_b):
    t = x2.shape[0]
    ysp = pl.BlockSpec((MERGE_TM, BRANCH_W), lambda i: (i, 0))
    vec = pl.BlockSpec((1, D_MODEL), lambda i: (0, 0))
    hvec = pl.BlockSpec((1, BRANCH_W), lambda i: (0, 0))
    cube = pl.BlockSpec((SG_GROUPS, BLOCK, BLOCK), lambda i: (0, 0, 0))
    whole = functools.partial(pl.BlockSpec, pipeline_mode=pl.Buffered(1))
    return pl.pallas_call(
        _merge_kernel,
        grid=(t // MERGE_TM,),
        in_specs=[ysp] * 3 + [
            pl.BlockSpec((MERGE_TM, BRANCH_W), lambda i: (i, ZF_SG_U)),
            pl.BlockSpec((MERGE_TM, BRANCH_W), lambda i: (i, ZF_SG_V)),
            pl.BlockSpec((MERGE_TM, D_MODEL), lambda i: (i, 0)),
            hvec, hvec, cube, cube,
            whole((D_MODEL, N_BRANCH * D_MODEL), lambda i: (0, 0)),
            pl.BlockSpec((1, N_BRANCH * D_MODEL), lambda i: (0, 0)),
            whole((N_BRANCH, BRANCH_W, D_MODEL), lambda i: (0, 0, 0)),
            whole((D_MODEL, D_MODEL), lambda i: (0, 0)),
            vec, vec,
        ],
        out_specs=pl.BlockSpec((MERGE_TM, D_MODEL), lambda i: (i, 0)),
        out_shape=jax.ShapeDtypeStruct((t, D_MODEL), F32),
        scratch_shapes=[pltpu.VMEM((MERGE_TM, BRANCH_W), BF16)],
        compiler_params=_params("arbitrary"),
        name="sg_merge_out",
    )(*ys, zf, zf, x2, sg_ln_g, sg_ln_b, ws, bs_b, w_gate, b_gate, w_branch, w_out, ln_g, ln_b)


def _mlp_kernel(x_ref, w1_ref, b1_ref, w2_ref, b2_ref, lg_ref, lb_ref, o_ref):
    def finish(rows, acc):
        o_ref[rows, :] = _layer_norm(ALPHA * x_ref[rows, :] + (acc + b2_ref[...]), lg_ref[...], lb_ref[...])

    done = []
    for r in range(MLP_TM // MLP_SUB):
        rows = slice(r * MLP_SUB, (r + 1) * MLP_SUB)
        xb = x_ref[rows, :].astype(BF16)
        acc = None
        for c in range(D_FF // MLP_FC):
            cols = slice(c * MLP_FC, (c + 1) * MLP_FC)
            hid = jnp.maximum(_dot(xb, w1_ref[:, cols]) + b1_ref[:, cols], 0.0)
            part = _dot((hid * hid).astype(BF16), w2_ref[cols, :])
            acc = part if acc is None else acc + part
        done.append((rows, acc))
    for rows, acc in done:
        finish(rows, acc)


def _mlp(x2, w1, b1, w2, b2, ln_g, ln_b):
    t = x2.shape[0]
    vec = pl.BlockSpec((1, D_MODEL), lambda i: (0, 0))
    return pl.pallas_call(
        _mlp_kernel,
        grid=(t // MLP_TM,),
        in_specs=[
            pl.BlockSpec((MLP_TM, D_MODEL), lambda i: (i, 0)),
            pl.BlockSpec((D_MODEL, D_FF), lambda i: (0, 0), pipeline_mode=pl.Buffered(1)),
            pl.BlockSpec((1, D_FF), lambda i: (0, 0)),
            pl.BlockSpec((D_FF, D_MODEL), lambda i: (0, 0), pipeline_mode=pl.Buffered(1)),
            vec, vec, vec,
        ],
        out_specs=pl.BlockSpec((MLP_TM, D_MODEL), lambda i: (i, 0)),
        out_shape=jax.ShapeDtypeStruct((t, D_MODEL), F32),
        compiler_params=_params("arbitrary"),
        name="mlp",
    )(x2, w1, b1, w2, b2, ln_g, ln_b)


def _block_diag(w):
    per = LRU_HEADS // 2
    w4 = w.reshape(2, per, LRU_HD, LRU_HD)
    eye = jnp.eye(per, dtype=w.dtype)
    return jnp.einsum('ghij,hk->ghikj', w4, eye).reshape(2, per * LRU_HD, per * LRU_HD)


def _rope_tables(seq):
    half = RET_DK // 2
    pos = jnp.arange(seq, dtype=F32)
    inv_freq = ROPE_THETA ** (-jnp.arange(half, dtype=F32) / half)
    ang = pos[:, None] * inv_freq[None, :]
    cos, sin = jnp.cos(ang), jnp.sin(ang)
    return jnp.concatenate([cos, cos], axis=-1), jnp.concatenate([-sin, sin], axis=-1)


def _layer(x2, batch, seq, cos_t, sin_t, w_in, b_in, conv_w, conv_b, lru_wa, lru_ba, lru_wx,
           lru_bx, lru_lambda, sg_ln_g, sg_ln_b, sg_ws, sg_bs, w_branch, w_out, ln1_g, ln1_b,
           w1, b1, w2, b2, ln2_g, ln2_b):
    gl0 = W_COLS
    zb, zf, y_a, y_b = _project(
        x2, w_in[:, :gl0].astype(BF16), b_in[None, :gl0], cos_t, sin_t, conv_w, conv_b[None, :],
        _block_diag(lru_wa).astype(BF16), lru_ba[None, :], _block_diag(lru_wx).astype(BF16),
        lru_bx[None, :], lru_lambda[None, :], batch, seq)
    y_c = _stick_breaking(zb, batch, seq)
    bs_b = jnp.broadcast_to(sg_bs[:, :, None], (SG_GROUPS, BLOCK, BRANCH_W // SG_GROUPS))

    x1 = _merge((y_a, y_b, y_c), zf, x2, sg_ln_g[None, :], sg_ln_b[None, :], sg_ws, bs_b,
                w_in[:, gl0:].astype(BF16), b_in[None, gl0:], w_branch.astype(BF16), w_out.astype(BF16),
                ln1_g[None, :], ln1_b[None, :])
    return _mlp(x1, w1.astype(BF16), b1[None, :], w2.astype(BF16), b2[None, :],
                ln2_g[None, :], ln2_b[None, :])


def kernel(x, w_in, b_in, conv_w, conv_b, lru_wa, lru_ba, lru_wx, lru_bx, lru_lambda, sg_ln_g, sg_ln_b, sg_ws, sg_bs, w_branch, w_out, ln1_g, ln1_b, w1, b1, w2, b2, ln2_g, ln2_b):
    batch, seq, d = x.shape
    cos_t, sin_t = _rope_tables(seq)
    x2 = x.reshape(batch * seq, d)
    per_layer = (w_in, b_in, conv_w, conv_b, lru_wa, lru_ba, lru_wx, lru_bx, lru_lambda, sg_ln_g,
                 sg_ln_b, sg_ws, sg_bs, w_branch, w_out, ln1_g, ln1_b, w1, b1, w2, b2, ln2_g, ln2_b)
    for l in range(DEPTH):
        x2 = _layer(x2, batch, seq, cos_t, sin_t, *(p[l] for p in per_layer))
    return x2.reshape(batch, seq, d)
```

```python
import functools
import math

import jax
import jax.numpy as jnp
import numpy as np
from jax import lax
from jax.experimental import pallas as pl
from jax.experimental.pallas import tpu as pltpu

F32 = jnp.float32
BF16 = jnp.bfloat16

D_MODEL = 1024
DEPTH = 2
BLOCK = 128
BRANCH_W = D_MODEL // 2
N_BRANCH = 4
RET_HEADS = 4
RET_DK = BRANCH_W // RET_HEADS
ROPE_THETA = 10000.0
LRU_HEADS = 8
LRU_HD = BRANCH_W // LRU_HEADS
CONV_W = 4
LRU_C = 8.0
SB_HEADS = 4
SB_HD = BRANCH_W // SB_HEADS
SG_GROUPS = 4
D_FF = 4 * D_MODEL
ALPHA = (2 * DEPTH) ** 0.25
LN_EPS = 1e-5
LOG2E = 1.4426950408889634
SB_EXP_ZERO = -105.0

W_RET_Q, W_RET_K, W_RET_V, W_RET_G, W_LRU_X, W_LRU_G, W_SB_Q, W_SB_K, W_SB_V, W_SG_U, W_SG_V = range(11)
W_COLS = 11 * BRANCH_W
ZF_SG_U, ZF_SG_V = 0, 1
ZF_COLS = 2 * BRANCH_W
ZB_SB_Q, ZB_SB_K, ZB_SB_V = 0, 1, 2
ZB_COLS = 3 * BRANCH_W

VMEM_LIMIT = 56 * 1024 * 1024

PROJ_TM = 512
LRU_ROWS = 128
LRU_LANES = 256
SB_TQ = 256
MERGE_TM = 512
MLP_TM = 1024
MLP_SUB = 512
MLP_FC = 1024


def _params(*sem):
    return pltpu.CompilerParams(dimension_semantics=sem, vmem_limit_bytes=VMEM_LIMIT)


def _dot(a, b):
    return jnp.dot(a, b, preferred_element_type=F32)


def _dot_nt(a, b):
    return lax.dot_general(a, b, (((1,), (1,)), ((), ())), preferred_element_type=F32)


def _gelu(x):
    k1 = -2.0 * math.sqrt(2.0 / math.pi) * LOG2E
    return x / (1.0 + jnp.exp2(x * (k1 + (k1 * 0.044715) * (x * x))))


def _layer_norm(h, g, b):
    mu = jnp.mean(h, axis=-1, keepdims=True)
    d = h - mu
    var = jnp.mean(d * d, axis=-1, keepdims=True)
    return d * lax.rsqrt(var + LN_EPS) * g + b


def _ret_log_g(h):
    return math.log1p(-(2.0 ** (-5.0 - h)))


def _ret_chunk(rows, zq_ref, zk_ref, zg_ref, zv_ref, cos_ref, sin_ref, ya_ref,
               state_ref, dmat_ref, qdec_ref, kdec_ref):
    heads = range(RET_HEADS)
    cols = [slice(h * RET_DK, (h + 1) * RET_DK) for h in heads]
    cs = cos_ref[rows, :]
    sn = sin_ref[rows, :]
    qr, kr = [], []
    for h in heads:
        q = zq_ref[rows, cols[h]]
        k = zk_ref[rows, cols[h]]
        qr.append(q * cs + pltpu.roll(q, RET_DK // 2, 1) * sn)
        kr.append((k * cs + pltpu.roll(k, RET_DK // 2, 1) * sn) * (RET_DK ** -0.5))
    scores = [_dot_nt(qr[h].astype(BF16), kr[h].astype(BF16)) * dmat_ref[h] for h in heads]
    cross = [_dot((qr[h] * qdec_ref[h]).astype(BF16), state_ref[h].astype(BF16)) for h in heads]
    kd_t = [jnp.transpose(kr[h] * kdec_ref[h]).astype(BF16) for h in heads]
    for h in heads:
        v = zv_ref[rows, cols[h]]
        y = _dot(scores[h].astype(BF16), v) + cross[h]
        state_ref[h] = math.exp(_ret_log_g(h) * BLOCK) * state_ref[h] + _dot(kd_t[h], v)
        mu = jnp.mean(y, axis=-1, keepdims=True)
        d = y - mu
        var = jnp.mean(d * d, axis=-1, keepdims=True)
        yn = d * lax.rsqrt(var + LN_EPS)
        ya_ref[rows, cols[h]] = (jax.nn.silu(zg_ref[rows, cols[h]]) * yn).astype(ya_ref.dtype)


def _proj_kernel(x_ref, w_ref, b_ref, cos_ref, sin_ref, cw_ref, cb_ref, wa_ref, ba_ref, wx_ref, bx_ref,
                 lam_ref, zb_ref, zf_ref, ya_ref, yb_ref,
                 zq_ref, zk_ref, zg_ref, zv_ref, state_ref, dmat_ref, qdec_ref, kdec_ref,
                 zlx_ref, zlg_ref, tail_ref, h_ref):
    @pl.when(pl.program_id(1) == 0)
    def _():
        tail_ref[...] = jnp.zeros_like(tail_ref)
        h_ref[...] = jnp.zeros_like(h_ref)
        state_ref[...] = jnp.zeros_like(state_ref)
        row = lax.broadcasted_iota(jnp.int32, (BLOCK, BLOCK), 0).astype(F32)
        col = lax.broadcasted_iota(jnp.int32, (BLOCK, BLOCK), 1).astype(F32)
        diff = row - col
        for h in range(RET_HEADS):
            lg = _ret_log_g(h)
            dmat_ref[h] = jnp.where(diff >= 0, jnp.exp(lg * jnp.maximum(diff, 0.0)), 0.0)
            qdec_ref[h] = jnp.exp(lg * (row + 1.0))
            kdec_ref[h] = jnp.exp(lg * (BLOCK - 1.0 - row))

    xb = x_ref[...].astype(BF16)

    half = BRANCH_W // 2

    def project(part, dst_ref, row0, col0, scale=None):
        def piece(hf):
            cols = slice(part * BRANCH_W + hf * half, part * BRANCH_W + (hf + 1) * half)
            acc = _dot(xb, w_ref[:, cols]) + b_ref[:, cols]
            if scale is not None:
                acc = acc * scale
            dst_ref[row0:row0 + PROJ_TM, col0 + hf * half:col0 + (hf + 1) * half] = acc.astype(dst_ref.dtype)
        return [functools.partial(piece, 0), functools.partial(piece, 1)]

    def ret(c):
        _ret_chunk(slice(c * BLOCK, (c + 1) * BLOCK), zq_ref, zk_ref, zg_ref, zv_ref, cos_ref, sin_ref,
                   ya_ref, state_ref, dmat_ref, qdec_ref, kdec_ref)

    def lru(blk, hf):
        _lru_piece(blk * LRU_ROWS, hf, zlx_ref, cw_ref, cb_ref, wa_ref, ba_ref, wx_ref, bx_ref, lam_ref,
                   zlg_ref, tail_ref, h_ref, yb_ref)

    for piece in project(W_LRU_X, zlx_ref, 0, 0) + project(W_LRU_G, zlg_ref, 0, 0):
        piece()
    matmuls = (project(W_RET_Q, zq_ref, 0, 0) + project(W_RET_K, zk_ref, 0, 0)
               + project(W_RET_G, zg_ref, 0, 0) + project(W_RET_V, zv_ref, 0, 0)
               + project(W_SG_U, zf_ref, 0, ZF_SG_U * BRANCH_W) + project(W_SG_V, zf_ref, 0, ZF_SG_V * BRANCH_W)
               + project(W_SB_Q, zb_ref, 0, ZB_SB_Q * BRANCH_W, SB_HD ** -0.5)
               + project(W_SB_K, zb_ref, 0, ZB_SB_K * BRANCH_W) + project(W_SB_V, zb_ref, 0, ZB_SB_V * BRANCH_W))
    nblk = PROJ_TM // LRU_ROWS
    lru_pieces = [functools.partial(lru, blk, hf) for blk in range(nblk)
                  for hf in range(BRANCH_W // LRU_LANES)]
    ret_pieces = [functools.partial(ret, c) for c in range(PROJ_TM // BLOCK)]
    late = len(ret_pieces)
    branch = lru_pieces[:-late]
    for c, piece in enumerate(ret_pieces):
        branch += [piece] + lru_pieces[len(lru_pieces) - late + c:len(lru_pieces) - late + c + 1]
    ret_needs = 8
    issued = 0
    for n, piece in enumerate(branch):
        if piece in ret_pieces and issued < ret_needs:
            for m in matmuls[issued:ret_needs]:
                m()
            issued = ret_needs
        piece()
        upto = max(issued, min(len(matmuls), -(-len(matmuls) * (n + 1) // len(branch))))
        for m in matmuls[issued:upto]:
            m()
        issued = upto
    for m in matmuls[issued:]:
        m()


def _project(x2, w, b, cos_t, sin_t, conv_w, conv_b, wa_bd, ba, wx_bd, bx, lam, batch, seq):
    t, d = x2.shape
    nblk = seq // PROJ_TM
    row_map = lambda bi, n: (bi * nblk + n, 0)
    whole = functools.partial(pl.BlockSpec, pipeline_mode=pl.Buffered(1))
    tile = lambda width: pl.BlockSpec((PROJ_TM, width), row_map)
    vec = pl.BlockSpec((1, BRANCH_W), lambda bi, n: (0, 0))
    mat = pl.BlockSpec((BRANCH_W // LRU_LANES, LRU_LANES, LRU_LANES), lambda bi, n: (0, 0, 0))
    tile_f32 = pltpu.VMEM((PROJ_TM, BRANCH_W), F32)
    return pl.pallas_call(
        _proj_kernel,
        grid=(batch, nblk),
        in_specs=[
            tile(d),
            whole((d, W_COLS), lambda bi, n: (0, 0)),
            pl.BlockSpec((1, W_COLS), lambda bi, n: (0, 0)),
            pl.BlockSpec((PROJ_TM, RET_DK), lambda bi, n: (n, 0)),
            pl.BlockSpec((PROJ_TM, RET_DK), lambda bi, n: (n, 0)),
            pl.BlockSpec((CONV_W, BRANCH_W), lambda bi, n: (0, 0)),
            vec, mat, vec, mat, vec, vec,
        ],
        out_specs=[tile(ZB_COLS), tile(ZF_COLS), tile(BRANCH_W), tile(BRANCH_W)],
        out_shape=[jax.ShapeDtypeStruct((t, ZB_COLS), BF16), jax.ShapeDtypeStruct((t, ZF_COLS), F32),
                   jax.ShapeDtypeStruct((t, BRANCH_W), BF16), jax.ShapeDtypeStruct((t, BRANCH_W), BF16)],
        scratch_shapes=[tile_f32] * 3 + [pltpu.VMEM((PROJ_TM, BRANCH_W), BF16)]
        + [pltpu.VMEM((RET_HEADS, RET_DK, RET_DK), F32)] * 4
        + [tile_f32, tile_f32, pltpu.VMEM((8 * (CONV_W - 1), BRANCH_W), F32), pltpu.VMEM((8, BRANCH_W), F32)],
        compiler_params=_params("arbitrary", "arbitrary"),
        name="in_proj_retention_lru",
    )(x2, w, b, cos_t, sin_t, conv_w, conv_b, wa_bd, ba, wx_bd, bx, lam)


def _lru_piece(r0, hf, zlx_ref, cw_ref, cb_ref, wa_ref, ba_ref, wx_ref, bx_ref, lam_ref, gate_ref,
               tail_ref, h_ref, o_ref):
    rows = LRU_ROWS
    per = rows // 8
    half = LRU_LANES
    lanes = slice(hf * half, (hf + 1) * half)
    xp = pltpu.einshape("(sj)c->(js)c", zlx_ref[r0:r0 + rows, lanes], s=8)
    sub = lax.broadcasted_iota(jnp.int32, (8, half), 0)

    def delayed(k):
        heads = []
        for m in range(per - k, per):
            prev = tail_ref[8 * (m - (per - 3)):8 * (m - (per - 3)) + 8, lanes]
            heads.append(pltpu.roll(jnp.where(sub == 7, prev, xp[8 * m:8 * m + 8, :]), 1, 0))
        return jnp.concatenate(heads + [xp[0:8 * (per - k), :]], axis=0)

    xc = cb_ref[:, lanes] + cw_ref[CONV_W - 1:CONV_W, lanes] * xp
    for k in range(1, CONV_W):
        xc = xc + cw_ref[CONV_W - 1 - k:CONV_W - k, lanes] * delayed(k)
    tail_ref[:, lanes] = xp[8 * (per - 3):8 * per, :]

    xcb = xc.astype(BF16)
    r = jax.nn.sigmoid(_dot(xcb, wa_ref[hf]) + ba_ref[:, lanes])
    i = jax.nn.sigmoid(_dot(xcb, wx_ref[hf]) + bx_ref[:, lanes])
    nlam = -lam_ref[:, lanes]
    softplus = jnp.maximum(nlam, 0.0) + jnp.log1p(jnp.exp(-jnp.abs(nlam)))
    log_a = -LRU_C * r * softplus
    a_all = jnp.exp(log_a)
    u_all = jnp.sqrt(jnp.tanh(-log_a) * (1.0 + a_all * a_all)) * (i * xc)

    sub1 = sub[:, 0:128]
    slabs = []
    for s in range(half // 128):
        sl = slice(s * 128, (s + 1) * 128)
        cols = slice(hf * half + s * 128, hf * half + (s + 1) * 128)
        run_h, run_p = [u_all[0:8, sl]], [a_all[0:8, sl]]
        for j in range(1, per):
            aj = a_all[8 * j:8 * j + 8, sl]
            run_h.append(aj * run_h[-1] + u_all[8 * j:8 * j + 8, sl])
            run_p.append(aj * run_p[-1])
        tot_h, tot_p = run_h[-1], run_p[-1]
        for shift in (1, 2, 4):
            keep = sub1 >= shift
            p_sh = jnp.where(keep, pltpu.roll(tot_p, shift, 0), 1.0)
            h_sh = jnp.where(keep, pltpu.roll(tot_h, shift, 0), 0.0)
            tot_h = tot_p * h_sh + tot_h
            tot_p = tot_p * p_sh
        hb = h_ref[:, cols]
        ends = tot_h + tot_p * hb
        start = jnp.where(sub1 == 0, hb, pltpu.roll(ends, 1, 0))
        h_ref[:, cols] = jnp.broadcast_to(ends[7:8, :], (8, 128))
        slabs.append(jnp.concatenate([run_h[j] + run_p[j] * start for j in range(per)], axis=0))
    h = pltpu.einshape("(js)c->(sj)c", jnp.concatenate(slabs, axis=1), s=8)
    o_ref[r0:r0 + rows, lanes] = (h * _gelu(gate_ref[r0:r0 + rows, lanes])).astype(o_ref.dtype)


def _sb_key_blocks(q_ref, k_ref, v_ref, acc_ref, carry_ref, tri, offs, diag_mask):
    t = SB_TQ
    heads = range(SB_HEADS)
    blocks = range(len(offs))
    cols = [slice(h * SB_HD, (h + 1) * SB_HD) for h in heads]
    zs = [[_dot_nt(q_ref[:, cols[h]], k_ref[pl.ds(offs[b], t), cols[h]]) for b in blocks] for h in heads]
    cums = [[None] * len(offs) for _ in heads]
    for h in heads:
        for b in blocks:
            z = zs[h][b]
            lg = jnp.log(1.0 + jnp.exp2(jnp.abs(z) * (-LOG2E)))
            log_1m = jnp.minimum(-z, 0.0) - lg
            if b == 0 and diag_mask is not None:
                log_1m = jnp.where(diag_mask, log_1m, 0.0)
            hi = log_1m.astype(BF16)
            lo = (log_1m - hi.astype(F32)).astype(BF16)
            cums[h][b] = _dot(hi, tri) + _dot(lo, tri)
    for h in heads:
        carry = None if diag_mask is not None else carry_ref[h]
        pv = None
        for b in blocks:
            e = zs[h][b] + cums[h][b]
            if carry is not None:
                e = e + jnp.concatenate([carry] * (t // 128), axis=1)
            w = jnp.exp(e)
            if b == 0 and diag_mask is not None:
                w = jnp.where(diag_mask, w, 0.0)
            d = _dot(w.astype(BF16), v_ref[pl.ds(offs[b], t), cols[h]])
            pv = d if pv is None else pv + d
            block_sum = cums[h][b][:, 0:1]
            carry = jnp.broadcast_to(block_sum, (t, 128)) if carry is None else carry + block_sum
        carry_ref[h] = carry
        if diag_mask is not None:
            acc_ref[:, cols[h]] = pv
        else:
            acc_ref[:, cols[h]] += pv


def _sb_kernel(q_ref, k_ref, v_ref, o_ref, acc_ref, carry_ref):
    i = pl.program_id(1)
    t = SB_TQ
    row = lax.broadcasted_iota(jnp.int32, (t, t), 0)
    col = lax.broadcasted_iota(jnp.int32, (t, t), 1)
    tri = (row >= col).astype(BF16)
    diag = pl.multiple_of(i * t, t)

    @pl.when(i == 0)
    def _():
        _sb_key_blocks(q_ref, k_ref, v_ref, acc_ref, carry_ref, tri, [diag], col < row)

    @pl.when(i > 0)
    def _():
        prev = pl.multiple_of((i - 1) * t, t)
        _sb_key_blocks(q_ref, k_ref, v_ref, acc_ref, carry_ref, tri, [diag, prev], col < row)

    def any_weight_left():
        m = carry_ref[0]
        for h in range(1, SB_HEADS):
            m = jnp.maximum(m, carry_ref[h])
        return jnp.max(m) > SB_EXP_ZERO

    def cond(state):
        step, live = state
        return jnp.logical_and(step < i, live)

    def body(state):
        step, _ = state
        off = pl.multiple_of((i - 1 - step) * t, t)
        _sb_key_blocks(q_ref, k_ref, v_ref, acc_ref, carry_ref, tri, [off], None)
        return step + 1, any_weight_left()

    lax.while_loop(cond, body, (jnp.int32(1), any_weight_left()))
    o_ref[...] = acc_ref[...].astype(o_ref.dtype)


def _stick_breaking(zb, batch, seq):
    nq = seq // SB_TQ
    zb3 = zb.reshape(batch, seq, ZB_COLS)
    whole = functools.partial(pl.BlockSpec, pipeline_mode=pl.Buffered(1))
    return pl.pallas_call(
        _sb_kernel,
        grid=(batch, nq),
        in_specs=[
            pl.BlockSpec((None, SB_TQ, BRANCH_W), lambda b, i: (b, i, ZB_SB_Q)),
            whole((None, seq, BRANCH_W), lambda b, i: (b, 0, ZB_SB_K)),
            whole((None, seq, BRANCH_W), lambda b, i: (b, 0, ZB_SB_V)),
        ],
        out_specs=pl.BlockSpec((None, SB_TQ, BRANCH_W), lambda b, i: (b, i, 0)),
        out_shape=jax.ShapeDtypeStruct((batch, seq, BRANCH_W), BF16),
        scratch_shapes=[pltpu.VMEM((SB_TQ, BRANCH_W), F32), pltpu.VMEM((SB_HEADS, SB_TQ, 128), F32)],
        compiler_params=_params("arbitrary", "arbitrary"),
        name="stick_breaking",
    )(zb3, zb3, zb3).reshape(batch * seq, BRANCH_W)


def _sg_chunk(rows, u_ref, v_ref, g_ref, b_ref, ws_ref, bs_ref, o_ref):
    u = _gelu(u_ref[rows, :])
    v = _layer_norm(_gelu(v_ref[rows, :]), g_ref[...], b_ref[...]).astype(BF16)
    row = lax.broadcasted_iota(jnp.int32, (BLOCK, BLOCK), 0)
    col = lax.broadcasted_iota(jnp.int32, (BLOCK, BLOCK), 1)
    gd = BRANCH_W // SG_GROUPS
    for g in range(SG_GROUPS):
        w = jnp.where(col <= row, ws_ref[g], 0.0).astype(BF16)
        cols = slice(g * gd, (g + 1) * gd)
        s = _dot(w, v[:, cols]) + bs_ref[g]
        o_ref[rows, cols] = (u[:, cols] * s).astype(o_ref.dtype)


def _merge_kernel(ya_ref, yb_ref, yc_ref, u_ref, v_ref, x_ref, sg_g_ref, sg_b_ref, ws_ref, bs_ref,
                  wg_ref, bg_ref, wb_ref, wo_ref, lg_ref, lb_ref, o_ref, yd_ref):
    xb = x_ref[...].astype(BF16)
    nchunk = MERGE_TM // BLOCK
    sg_after = {0: range(0, nchunk // 2), 1: range(nchunk // 2, nchunk - 1), 2: range(nchunk - 1, nchunk)}
    merged = None
    for n, y_ref in enumerate((ya_ref, yb_ref, yc_ref, yd_ref)):
        cols = slice(n * D_MODEL, (n + 1) * D_MODEL)
        logits = _dot(xb, wg_ref[:, cols]) + bg_ref[:, cols]
        for c in sg_after.get(n, ()):
            _sg_chunk(slice(c * BLOCK, (c + 1) * BLOCK), u_ref, v_ref, sg_g_ref, sg_b_ref, ws_ref, bs_ref,
                      yd_ref)
        term = jax.nn.sigmoid(logits) * _dot(y_ref[...], wb_ref[n])
        merged = term if merged is None else merged + term
    mb = merged.astype(BF16)
    nsub = 4
    sub = MERGE_TM // nsub
    outs = [_dot(mb[r * sub:(r + 1) * sub, :], wo_ref[...]) for r in range(nsub)]
    for r in range(nsub):
        rows = slice(r * sub, (r + 1) * sub)
        o_ref[rows, :] = _layer_norm(ALPHA * x_ref[rows, :] + outs[r], lg_ref[...], lb_ref[...])


def _merge(ys, zf, x2, sg_ln_g, sg_ln_b, ws, bs_b, w_gate, b_gate, w_branch, w_out, ln_g, ln_b):
    t = x2.shape[0]
    ysp = pl.BlockSpec((MERGE_TM, BRANCH_W), lambda i: (i, 0))
    vec = pl.BlockSpec((1, D_MODEL), lambda i: (0, 0))
    hvec = pl.BlockSpec((1, BRANCH_W), lambda i: (0, 0))
    cube = pl.BlockSpec((SG_GROUPS, BLOCK, BLOCK), lambda i: (0, 0, 0))
    whole = functools.partial(pl.BlockSpec, pipeline_mode=pl.Buffered(1))
    return pl.pallas_call(
        _merge_kernel,
        grid=(t // MERGE_TM,),
        in_specs=[ysp] * 3 + [
            pl.BlockSpec((MERGE_TM, BRANCH_W), lambda i: (i, ZF_SG_U)),
            pl.BlockSpec((MERGE_TM, BRANCH_W), lambda i: (i, ZF_SG_V)),
            pl.BlockSpec((MERGE_TM, D_MODEL), lambda i: (i, 0)),
            hvec, hvec, cube, cube,
            whole((D_MODEL, N_BRANCH * D_MODEL), lambda i: (0, 0)),
            pl.BlockSpec((1, N_BRANCH * D_MODEL), lambda i: (0, 0)),
            whole((N_BRANCH, BRANCH_W, D_MODEL), lambda i: (0, 0, 0)),
            whole((D_MODEL, D_MODEL), lambda i: (0, 0)),
            vec, vec,
        ],
        out_specs=pl.BlockSpec((MERGE_TM, D_MODEL), lambda i: (i, 0)),
        out_shape=jax.ShapeDtypeStruct((t, D_MODEL), F32),
        scratch_shapes=[pltpu.VMEM((MERGE_TM, BRANCH_W), BF16)],
        compiler_params=_params("arbitrary"),
        name="sg_merge_out",
    )(*ys, zf, zf, x2, sg_ln_g, sg_ln_b, ws, bs_b, w_gate, b_gate, w_branch, w_out, ln_g, ln_b)


def _mlp_kernel(x_ref, w1_ref, b1_ref, w2_ref, b2_ref, lg_ref, lb_ref, o_ref):
    def finish(rows, acc):
        o_ref[rows, :] = _layer_norm(ALPHA * x_ref[rows, :] + (acc + b2_ref[...]), lg_ref[...], lb_ref[...])

    done = []
    for r in range(MLP_TM // MLP_SUB):
        rows = slice(r * MLP_SUB, (r + 1) * MLP_SUB)
        xb = x_ref[rows, :].astype(BF16)
        acc = None
        for c in range(D_FF // MLP_FC):
            cols = slice(c * MLP_FC, (c + 1) * MLP_FC)
            hid = jnp.maximum(_dot(xb, w1_ref[:, cols]) + b1_ref[:, cols], 0.0)
            part = _dot((hid * hid).astype(BF16), w2_ref[cols, :])
            acc = part if acc is None else acc + part
        done.append((rows, acc))
    for rows, acc in done:
        finish(rows, acc)


def _mlp(x2, w1, b1, w2, b2, ln_g, ln_b):
    t = x2.shape[0]
    vec = pl.BlockSpec((1, D_MODEL), lambda i: (0, 0))
    return pl.pallas_call(
        _mlp_kernel,
        grid=(t // MLP_TM,),
        in_specs=[
            pl.BlockSpec((MLP_TM, D_MODEL), lambda i: (i, 0)),
            pl.BlockSpec((D_MODEL, D_FF), lambda i: (0, 0), pipeline_mode=pl.Buffered(1)),
            pl.BlockSpec((1, D_FF), lambda i: (0, 0)),
            pl.BlockSpec((D_FF, D_MODEL), lambda i: (0, 0), pipeline_mode=pl.Buffered(1)),
            vec, vec, vec,
        ],
        out_specs=pl.BlockSpec((MLP_TM, D_MODEL), lambda i: (i, 0)),
        out_shape=jax.ShapeDtypeStruct((t, D_MODEL), F32),
        compiler_params=_params("arbitrary"),
        name="mlp",
    )(x2, w1, b1, w2, b2, ln_g, ln_b)


def _block_diag(w):
    per = LRU_LANES // LRU_HD
    groups = LRU_HEADS // per
    w4 = w.reshape(groups, per, LRU_HD, LRU_HD)
    eye = jnp.eye(per, dtype=w.dtype)
    return jnp.einsum('ghij,hk->ghikj', w4, eye).reshape(groups, LRU_LANES, LRU_LANES)


def _rope_tables(seq):
    half = RET_DK // 2
    pos = np.arange(seq, dtype=np.float32)
    inv_freq = (np.float32(ROPE_THETA) ** (-np.arange(half, dtype=np.float32) / np.float32(half))).astype(np.float32)
    ang = pos[:, None] * inv_freq[None, :]
    cos, sin = np.cos(ang), np.sin(ang)
    return (jnp.asarray(np.concatenate([cos, cos], axis=-1), F32),
            jnp.asarray(np.concatenate([-sin, sin], axis=-1), F32))


def _layer(x2, batch, seq, cos_t, sin_t, w_in, b_in, conv_w, conv_b, lru_wa, lru_ba, lru_wx,
           lru_bx, lru_lambda, sg_ln_g, sg_ln_b, sg_ws, sg_bs, w_branch, w_out, ln1_g, ln1_b,
           w1, b1, w2, b2, ln2_g, ln2_b):
    gl0 = W_COLS
    zb, zf, y_a, y_b = _project(
        x2, w_in[:, :gl0].astype(BF16), b_in[None, :gl0], cos_t, sin_t, conv_w, conv_b[None, :],
        _block_diag(lru_wa).astype(BF16), lru_ba[None, :], _block_diag(lru_wx).astype(BF16),
        lru_bx[None, :], lru_lambda[None, :], batch, seq)
    y_c = _stick_breaking(zb, batch, seq)
    bs_b = jnp.broadcast_to(sg_bs[:, :, None], (SG_GROUPS, BLOCK, BRANCH_W // SG_GROUPS))

    x1 = _merge((y_a, y_b, y_c), zf, x2, sg_ln_g[None, :], sg_ln_b[None, :], sg_ws, bs_b,
                w_in[:, gl0:].astype(BF16), b_in[None, gl0:], w_branch.astype(BF16), w_out.astype(BF16),
                ln1_g[None, :], ln1_b[None, :])
    return _mlp(x1, w1.astype(BF16), b1[None, :], w2.astype(BF16), b2[None, :],
                ln2_g[None, :], ln2_b[None, :])


def kernel(x, w_in, b_in, conv_w, conv_b, lru_wa, lru_ba, lru_wx, lru_bx, lru_lambda, sg_ln_g, sg_ln_b, sg_ws, sg_bs, w_branch, w_out, ln1_g, ln1_b, w1, b1, w2, b2, ln2_g, ln2_b):
    batch, seq, d = x.shape
    cos_t, sin_t = _rope_tables(seq)
    x2 = x.reshape(batch * seq, d)
    per_layer = (w_in, b_in, conv_w, conv_b, lru_wa, lru_ba, lru_wx, lru_bx, lru_lambda, sg_ln_g,
                 sg_ln_b, sg_ws, sg_bs, w_branch, w_out, ln1_g, ln1_b, w1, b1, w2, b2, ln2_g, ln2_b)
    for l in range(DEPTH):
        x2 = _layer(x2, batch, seq, cos_t, sin_t, *(p[l] for p in per_layer))
    return x2.reshape(batch, seq, d)
```

```python
import functools
import math

import jax
import jax.numpy as jnp
import numpy as np
from jax import lax
from jax.experimental import pallas as pl
from jax.experimental.pallas import tpu as pltpu

F32 = jnp.float32
BF16 = jnp.bfloat16

D_MODEL = 1024
DEPTH = 2
BLOCK = 128
BRANCH_W = D_MODEL // 2
N_BRANCH = 4
RET_HEADS = 4
RET_DK = BRANCH_W // RET_HEADS
ROPE_THETA = 10000.0
LRU_HEADS = 8
LRU_HD = BRANCH_W // LRU_HEADS
CONV_W = 4
LRU_C = 8.0
SB_HEADS = 4
SB_HD = BRANCH_W // SB_HEADS
SG_GROUPS = 4
D_FF = 4 * D_MODEL
ALPHA = (2 * DEPTH) ** 0.25
LN_EPS = 1e-5
LOG2E = 1.4426950408889634
SB_EXP_ZERO = -105.0

W_RET_Q, W_RET_K, W_RET_V, W_RET_G, W_LRU_X, W_LRU_G, W_SB_Q, W_SB_K, W_SB_V, W_SG_U, W_SG_V = range(11)
W_COLS = 11 * BRANCH_W
ZF_SG_U, ZF_SG_V = 0, 1
ZF_COLS = 2 * BRANCH_W
ZB_SB_Q, ZB_SB_K, ZB_SB_V = 0, 1, 2
ZB_COLS = 3 * BRANCH_W

VMEM_LIMIT = 56 * 1024 * 1024

PROJ_TM = 512
LRU_ROWS = 128
LRU_LANES = 256
SB_TQ = 256
SB_STEP_BLOCKS = 2
MERGE_TM = 512
MLP_TM = 1024
MLP_SUB = 512
MLP_FC = 1024


def _params(*sem):
    return pltpu.CompilerParams(dimension_semantics=sem, vmem_limit_bytes=VMEM_LIMIT)


def _dot(a, b):
    return jnp.dot(a, b, preferred_element_type=F32)


def _dot_nt(a, b):
    return lax.dot_general(a, b, (((1,), (1,)), ((), ())), preferred_element_type=F32)


def _gelu(x):
    k1 = -2.0 * math.sqrt(2.0 / math.pi) * LOG2E
    return x / (1.0 + jnp.exp2(x * (k1 + (k1 * 0.044715) * (x * x))))


def _layer_norm(h, g, b):
    mu = jnp.mean(h, axis=-1, keepdims=True)
    d = h - mu
    var = jnp.mean(d * d, axis=-1, keepdims=True)
    return d * lax.rsqrt(var + LN_EPS) * g + b


def _ret_log_g(h):
    return math.log1p(-(2.0 ** (-5.0 - h)))


def _ret_chunk(rows, zq_ref, zk_ref, zg_ref, zv_ref, cos_ref, sin_ref, ya_ref,
               state_ref, dmat_ref, qdec_ref, kdec_ref):
    heads = range(RET_HEADS)
    cols = [slice(h * RET_DK, (h + 1) * RET_DK) for h in heads]
    cs = cos_ref[rows, :]
    sn = sin_ref[rows, :]
    qr, kr = [], []
    for h in heads:
        q = zq_ref[rows, cols[h]]
        k = zk_ref[rows, cols[h]]
        qr.append(q * cs + pltpu.roll(q, RET_DK // 2, 1) * sn)
        kr.append((k * cs + pltpu.roll(k, RET_DK // 2, 1) * sn) * (RET_DK ** -0.5))
    scores = [_dot_nt(qr[h].astype(BF16), kr[h].astype(BF16)) * dmat_ref[h] for h in heads]
    cross = [_dot((qr[h] * qdec_ref[h]).astype(BF16), state_ref[h].astype(BF16)) for h in heads]
    kd_t = [jnp.transpose(kr[h] * kdec_ref[h]).astype(BF16) for h in heads]
    for h in heads:
        v = zv_ref[rows, cols[h]]
        y = _dot(scores[h].astype(BF16), v) + cross[h]
        state_ref[h] = math.exp(_ret_log_g(h) * BLOCK) * state_ref[h] + _dot(kd_t[h], v)
        mu = jnp.mean(y, axis=-1, keepdims=True)
        d = y - mu
        var = jnp.mean(d * d, axis=-1, keepdims=True)
        yn = d * lax.rsqrt(var + LN_EPS)
        ya_ref[rows, cols[h]] = (jax.nn.silu(zg_ref[rows, cols[h]]) * yn).astype(ya_ref.dtype)


def _proj_kernel(x_ref, w_ref, b_ref, cos_ref, sin_ref, cw_ref, cb_ref, wa_ref, ba_ref, wx_ref, bx_ref,
                 lam_ref, zb_ref, zf_ref, ya_ref, yb_ref,
                 zq_ref, zk_ref, zg_ref, zv_ref, state_ref, dmat_ref, qdec_ref, kdec_ref,
                 zlx_ref, zlg_ref, tail_ref, h_ref):
    @pl.when(pl.program_id(1) == 0)
    def _():
        tail_ref[...] = jnp.zeros_like(tail_ref)
        h_ref[...] = jnp.zeros_like(h_ref)
        state_ref[...] = jnp.zeros_like(state_ref)
        row = lax.broadcasted_iota(jnp.int32, (BLOCK, BLOCK), 0).astype(F32)
        col = lax.broadcasted_iota(jnp.int32, (BLOCK, BLOCK), 1).astype(F32)
        diff = row - col
        for h in range(RET_HEADS):
            lg = _ret_log_g(h)
            dmat_ref[h] = jnp.where(diff >= 0, jnp.exp(lg * jnp.maximum(diff, 0.0)), 0.0)
            qdec_ref[h] = jnp.exp(lg * (row + 1.0))
            kdec_ref[h] = jnp.exp(lg * (BLOCK - 1.0 - row))

    xb = x_ref[...].astype(BF16)

    half = BRANCH_W // 2

    def project(part, dst_ref, row0, col0, scale=None):
        def piece(hf):
            cols = slice(part * BRANCH_W + hf * half, part * BRANCH_W + (hf + 1) * half)
            acc = _dot(xb, w_ref[:, cols]) + b_ref[:, cols]
            if scale is not None:
                acc = acc * scale
            dst_ref[row0:row0 + PROJ_TM, col0 + hf * half:col0 + (hf + 1) * half] = acc.astype(dst_ref.dtype)
        return [functools.partial(piece, 0), functools.partial(piece, 1)]

    def ret(c):
        _ret_chunk(slice(c * BLOCK, (c + 1) * BLOCK), zq_ref, zk_ref, zg_ref, zv_ref, cos_ref, sin_ref,
                   ya_ref, state_ref, dmat_ref, qdec_ref, kdec_ref)

    def lru(blk, hf):
        _lru_piece(blk * LRU_ROWS, hf, zlx_ref, cw_ref, cb_ref, wa_ref, ba_ref, wx_ref, bx_ref, lam_ref,
                   zlg_ref, tail_ref, h_ref, yb_ref)

    for piece in project(W_LRU_X, zlx_ref, 0, 0) + project(W_LRU_G, zlg_ref, 0, 0):
        piece()
    matmuls = (project(W_RET_Q, zq_ref, 0, 0) + project(W_RET_K, zk_ref, 0, 0)
               + project(W_RET_G, zg_ref, 0, 0) + project(W_RET_V, zv_ref, 0, 0)
               + project(W_SG_U, zf_ref, 0, ZF_SG_U * BRANCH_W) + project(W_SG_V, zf_ref, 0, ZF_SG_V * BRANCH_W)
               + project(W_SB_Q, zb_ref, 0, ZB_SB_Q * BRANCH_W, SB_HD ** -0.5)
               + project(W_SB_K, zb_ref, 0, ZB_SB_K * BRANCH_W) + project(W_SB_V, zb_ref, 0, ZB_SB_V * BRANCH_W))
    nblk = PROJ_TM // LRU_ROWS
    lru_pieces = [functools.partial(lru, blk, hf) for blk in range(nblk)
                  for hf in range(BRANCH_W // LRU_LANES)]
    ret_pieces = [functools.partial(ret, c) for c in range(PROJ_TM // BLOCK)]
    late = len(ret_pieces)
    branch = lru_pieces[:-late]
    for c, piece in enumerate(ret_pieces):
        branch += [piece] + lru_pieces[len(lru_pieces) - late + c:len(lru_pieces) - late + c + 1]
    ret_needs = 8
    issued = 0
    for n, piece in enumerate(branch):
        if piece in ret_pieces and issued < ret_needs:
            for m in matmuls[issued:ret_needs]:
                m()
            issued = ret_needs
        piece()
        upto = max(issued, min(len(matmuls), -(-len(matmuls) * (n + 1) // len(branch))))
        for m in matmuls[issued:upto]:
            m()
        issued = upto
    for m in matmuls[issued:]:
        m()


def _project(x2, w, b, cos_t, sin_t, conv_w, conv_b, wa_bd, ba, wx_bd, bx, lam, batch, seq):
    t, d = x2.shape
    nblk = seq // PROJ_TM
    row_map = lambda bi, n: (bi * nblk + n, 0)
    whole = functools.partial(pl.BlockSpec, pipeline_mode=pl.Buffered(1))
    tile = lambda width: pl.BlockSpec((PROJ_TM, width), row_map)
    vec = pl.BlockSpec((1, BRANCH_W), lambda bi, n: (0, 0))
    mat = pl.BlockSpec((BRANCH_W // LRU_LANES, LRU_LANES, LRU_LANES), lambda bi, n: (0, 0, 0))
    tile_f32 = pltpu.VMEM((PROJ_TM, BRANCH_W), F32)
    return pl.pallas_call(
        _proj_kernel,
        grid=(batch, nblk),
        in_specs=[
            tile(d),
            whole((d, W_COLS), lambda bi, n: (0, 0)),
            pl.BlockSpec((1, W_COLS), lambda bi, n: (0, 0)),
            pl.BlockSpec((PROJ_TM, RET_DK), lambda bi, n: (n, 0)),
            pl.BlockSpec((PROJ_TM, RET_DK), lambda bi, n: (n, 0)),
            pl.BlockSpec((CONV_W, BRANCH_W), lambda bi, n: (0, 0)),
            vec, mat, vec, mat, vec, vec,
        ],
        out_specs=[tile(ZB_COLS), tile(ZF_COLS), tile(BRANCH_W), tile(BRANCH_W)],
        out_shape=[jax.ShapeDtypeStruct((t, ZB_COLS), BF16), jax.ShapeDtypeStruct((t, ZF_COLS), F32),
                   jax.ShapeDtypeStruct((t, BRANCH_W), BF16), jax.ShapeDtypeStruct((t, BRANCH_W), BF16)],
        scratch_shapes=[tile_f32] * 3 + [pltpu.VMEM((PROJ_TM, BRANCH_W), BF16)]
        + [pltpu.VMEM((RET_HEADS, RET_DK, RET_DK), F32)] * 4
        + [tile_f32, tile_f32, pltpu.VMEM((8 * (CONV_W - 1), BRANCH_W), F32), pltpu.VMEM((8, BRANCH_W), F32)],
        compiler_params=_params("arbitrary", "arbitrary"),
        name="in_proj_retention_lru",
    )(x2, w, b, cos_t, sin_t, conv_w, conv_b, wa_bd, ba, wx_bd, bx, lam)


def _lru_piece(r0, hf, zlx_ref, cw_ref, cb_ref, wa_ref, ba_ref, wx_ref, bx_ref, lam_ref, gate_ref,
               tail_ref, h_ref, o_ref):
    rows = LRU_ROWS
    per = rows // 8
    half = LRU_LANES
    lanes = slice(hf * half, (hf + 1) * half)
    xp = pltpu.einshape("(sj)c->(js)c", zlx_ref[r0:r0 + rows, lanes], s=8)
    sub = lax.broadcasted_iota(jnp.int32, (8, half), 0)

    def delayed(k):
        heads = []
        for m in range(per - k, per):
            prev = tail_ref[8 * (m - (per - 3)):8 * (m - (per - 3)) + 8, lanes]
            heads.append(pltpu.roll(jnp.where(sub == 7, prev, xp[8 * m:8 * m + 8, :]), 1, 0))
        return jnp.concatenate(heads + [xp[0:8 * (per - k), :]], axis=0)

    xc = cb_ref[:, lanes] + cw_ref[CONV_W - 1:CONV_W, lanes] * xp
    for k in range(1, CONV_W):
        xc = xc + cw_ref[CONV_W - 1 - k:CONV_W - k, lanes] * delayed(k)
    tail_ref[:, lanes] = xp[8 * (per - 3):8 * per, :]

    xcb = xc.astype(BF16)
    r = jax.nn.sigmoid(_dot(xcb, wa_ref[hf]) + ba_ref[:, lanes])
    i = jax.nn.sigmoid(_dot(xcb, wx_ref[hf]) + bx_ref[:, lanes])
    nlam = -lam_ref[:, lanes]
    softplus = jnp.maximum(nlam, 0.0) + jnp.log1p(jnp.exp(-jnp.abs(nlam)))
    log_a = -LRU_C * r * softplus
    a_all = jnp.exp(log_a)
    u_all = jnp.sqrt(jnp.tanh(-log_a) * (1.0 + a_all * a_all)) * (i * xc)

    sub1 = sub[:, 0:128]
    slabs = []
    for s in range(half // 128):
        sl = slice(s * 128, (s + 1) * 128)
        cols = slice(hf * half + s * 128, hf * half + (s + 1) * 128)
        run_h, run_p = [u_all[0:8, sl]], [a_all[0:8, sl]]
        for j in range(1, per):
            aj = a_all[8 * j:8 * j + 8, sl]
            run_h.append(aj * run_h[-1] + u_all[8 * j:8 * j + 8, sl])
            run_p.append(aj * run_p[-1])
        tot_h, tot_p = run_h[-1], run_p[-1]
        for shift in (1, 2, 4):
            keep = sub1 >= shift
            p_sh = jnp.where(keep, pltpu.roll(tot_p, shift, 0), 1.0)
            h_sh = jnp.where(keep, pltpu.roll(tot_h, shift, 0), 0.0)
            tot_h = tot_p * h_sh + tot_h
            tot_p = tot_p * p_sh
        hb = h_ref[:, cols]
        ends = tot_h + tot_p * hb
        start = jnp.where(sub1 == 0, hb, pltpu.roll(ends, 1, 0))
        h_ref[:, cols] = jnp.broadcast_to(ends[7:8, :], (8, 128))
        slabs.append(jnp.concatenate([run_h[j] + run_p[j] * start for j in range(per)], axis=0))
    h = pltpu.einshape("(js)c->(sj)c", jnp.concatenate(slabs, axis=1), s=8)
    o_ref[r0:r0 + rows, lanes] = (h * _gelu(gate_ref[r0:r0 + rows, lanes])).astype(o_ref.dtype)


def _sb_key_blocks(q_ref, qrows, k_ref, v_ref, acc_ref, carry_ref, tri, offs, diag_mask):
    t = SB_TQ
    heads = range(SB_HEADS)
    blocks = range(len(offs))
    cols = [slice(h * SB_HD, (h + 1) * SB_HD) for h in heads]
    zs = [[_dot_nt(q_ref[qrows, cols[h]], k_ref[pl.ds(offs[b], t), cols[h]]) for b in blocks] for h in heads]
    cums = [[None] * len(offs) for _ in heads]
    for h in heads:
        for b in blocks:
            z = zs[h][b]
            lg = jnp.log(1.0 + jnp.exp2(jnp.abs(z) * (-LOG2E)))
            log_1m = jnp.minimum(-z, 0.0) - lg
            if b == 0 and diag_mask is not None:
                log_1m = jnp.where(diag_mask, log_1m, 0.0)
            hi = log_1m.astype(BF16)
            lo = (log_1m - hi.astype(F32)).astype(BF16)
            cums[h][b] = _dot(hi, tri) + _dot(lo, tri)
    for h in heads:
        carry = None if diag_mask is not None else carry_ref[h]
        pv = None
        for b in blocks:
            e = zs[h][b] + cums[h][b]
            if carry is not None:
                e = e + jnp.concatenate([carry] * (t // 128), axis=1)
            w = jnp.exp(e)
            if b == 0 and diag_mask is not None:
                w = jnp.where(diag_mask, w, 0.0)
            d = _dot(w.astype(BF16), v_ref[pl.ds(offs[b], t), cols[h]])
            pv = d if pv is None else pv + d
            block_sum = cums[h][b][:, 0:1]
            carry = jnp.broadcast_to(block_sum, (t, 128)) if carry is None else carry + block_sum
        carry_ref[h] = carry
        if diag_mask is not None:
            acc_ref[:, cols[h]] = pv
        else:
            acc_ref[:, cols[h]] += pv


def _sb_kernel(q_ref, k_ref, v_ref, o_ref, acc_ref, carry_ref):
    t = SB_TQ
    row = lax.broadcasted_iota(jnp.int32, (t, t), 0)
    col = lax.broadcasted_iota(jnp.int32, (t, t), 1)
    tri = (row >= col).astype(BF16)

    def any_weight_left():
        m = carry_ref[0]
        for h in range(1, SB_HEADS):
            m = jnp.maximum(m, carry_ref[h])
        return jnp.max(m) > SB_EXP_ZERO

    def query_block(j, _):
        i = pl.program_id(1) * SB_STEP_BLOCKS + j
        qrows = pl.ds(pl.multiple_of(j * t, t), t)
        diag = pl.multiple_of(i * t, t)

        @pl.when(i == 0)
        def _():
            _sb_key_blocks(q_ref, qrows, k_ref, v_ref, acc_ref, carry_ref, tri, [diag], col < row)

        @pl.when(i > 0)
        def _():
            prev = pl.multiple_of((i - 1) * t, t)
            _sb_key_blocks(q_ref, qrows, k_ref, v_ref, acc_ref, carry_ref, tri, [diag, prev], col < row)

        def cond(state):
            step, live = state
            return jnp.logical_and(step < i, live)

        def body(state):
            step, _ = state
            off = pl.multiple_of((i - 1 - step) * t, t)
            _sb_key_blocks(q_ref, qrows, k_ref, v_ref, acc_ref, carry_ref, tri, [off], None)
            return step + 1, any_weight_left()

        lax.while_loop(cond, body, (jnp.int32(1), any_weight_left()))
        o_ref[qrows, :] = acc_ref[...].astype(o_ref.dtype)
        return 0

    lax.fori_loop(0, SB_STEP_BLOCKS, query_block, 0)


def _stick_breaking(zb, batch, seq):
    rows = SB_TQ * SB_STEP_BLOCKS
    zb3 = zb.reshape(batch, seq, ZB_COLS)
    return pl.pallas_call(
        _sb_kernel,
        grid=(batch, seq // rows),
        in_specs=[
            pl.BlockSpec((None, rows, BRANCH_W), lambda b, i: (b, i, ZB_SB_Q)),
            pl.BlockSpec((None, seq, BRANCH_W), lambda b, i: (b, 0, ZB_SB_K)),
            pl.BlockSpec((None, seq, BRANCH_W), lambda b, i: (b, 0, ZB_SB_V)),
        ],
        out_specs=pl.BlockSpec((None, rows, BRANCH_W), lambda b, i: (b, i, 0)),
        out_shape=jax.ShapeDtypeStruct((batch, seq, BRANCH_W), BF16),
        scratch_shapes=[pltpu.VMEM((SB_TQ, BRANCH_W), F32), pltpu.VMEM((SB_HEADS, SB_TQ, 128), F32)],
        compiler_params=_params("arbitrary", "arbitrary"),
        name="stick_breaking",
    )(zb3, zb3, zb3).reshape(batch * seq, BRANCH_W)


def _sg_chunk(rows, u_ref, v_ref, g_ref, b_ref, ws_ref, bs_ref, o_ref):
    u = _gelu(u_ref[rows, :])
    v = _layer_norm(_gelu(v_ref[rows, :]), g_ref[...], b_ref[...]).astype(BF16)
    row = lax.broadcasted_iota(jnp.int32, (BLOCK, BLOCK), 0)
    col = lax.broadcasted_iota(jnp.int32, (BLOCK, BLOCK), 1)
    gd = BRANCH_W // SG_GROUPS
    for g in range(SG_GROUPS):
        w = jnp.where(col <= row, ws_ref[g], 0.0).astype(BF16)
        cols = slice(g * gd, (g + 1) * gd)
        s = _dot(w, v[:, cols]) + bs_ref[g]
        o_ref[rows, cols] = (u[:, cols] * s).astype(o_ref.dtype)


def _merge_kernel(ya_ref, yb_ref, yc_ref, u_ref, v_ref, x_ref, sg_g_ref, sg_b_ref, ws_ref, bs_ref,
                  wg_ref, bg_ref, wb_ref, wo_ref, lg_ref, lb_ref, o_ref, yd_ref):
    xb = x_ref[...].astype(BF16)
    nchunk = MERGE_TM // BLOCK
    sg_after = {0: range(0, nchunk // 2), 1: range(nchunk // 2, nchunk - 1), 2: range(nchunk - 1, nchunk)}
    merged = None
    for n, y_ref in enumerate((ya_ref, yb_ref, yc_ref, yd_ref)):
        cols = slice(n * D_MODEL, (n + 1) * D_MODEL)
        logits = _dot(xb, wg_ref[:, cols]) + bg_ref[:, cols]
        for c in sg_after.get(n, ()):
            _sg_chunk(slice(c * BLOCK, (c + 1) * BLOCK), u_ref, v_ref, sg_g_ref, sg_b_ref, ws_ref, bs_ref,
                      yd_ref)
        term = jax.nn.sigmoid(logits) * _dot(y_ref[...], wb_ref[n])
        merged = term if merged is None else merged + term
    mb = merged.astype(BF16)
    nsub = 4
    sub = MERGE_TM // nsub
    outs = [_dot(mb[r * sub:(r + 1) * sub, :], wo_ref[...]) for r in range(nsub)]
    for r in range(nsub):
        rows = slice(r * sub, (r + 1) * sub)
        o_ref[rows, :] = _layer_norm(ALPHA * x_ref[rows, :] + outs[r], lg_ref[...], lb_ref[...])


def _merge(ys, zf, x2, sg_ln_g, sg_ln_b, ws, bs_b, w_gate, b_gate, w_branch, w_out, ln_g, ln_b):
    t = x2.shape[0]
    ysp = pl.BlockSpec((MERGE_TM, BRANCH_W), lambda i: (i, 0))
    vec = pl.BlockSpec((1, D_MODEL), lambda i: (0, 0))
    hvec = pl.BlockSpec((1, BRANCH_W), lambda i: (0, 0))
    cube = pl.BlockSpec((SG_GROUPS, BLOCK, BLOCK), lambda i: (0, 0, 0))
    whole = functools.partial(pl.BlockSpec, pipeline_mode=pl.Buffered(1))
    return pl.pallas_call(
        _merge_kernel,
        grid=(t // MERGE_TM,),
        in_specs=[ysp] * 3 + [
            pl.BlockSpec((MERGE_TM, BRANCH_W), lambda i: (i, ZF_SG_U)),
            pl.BlockSpec((MERGE_TM, BRANCH_W), lambda i: (i, ZF_SG_V)),
            pl.BlockSpec((MERGE_TM, D_MODEL), lambda i: (i, 0)),
            hvec, hvec, cube, cube,
            whole((D_MODEL, N_BRANCH * D_MODEL), lambda i: (0, 0)),
            pl.BlockSpec((1, N_BRANCH * D_MODEL), lambda i: (0, 0)),
            whole((N_BRANCH, BRANCH_W, D_MODEL), lambda i: (0, 0, 0)),
            whole((D_MODEL, D_MODEL), lambda i: (0, 0)),
            vec, vec,
        ],
        out_specs=pl.BlockSpec((MERGE_TM, D_MODEL), lambda i: (i, 0)),
        out_shape=jax.ShapeDtypeStruct((t, D_MODEL), F32),
        scratch_shapes=[pltpu.VMEM((MERGE_TM, BRANCH_W), BF16)],
        compiler_params=_params("arbitrary"),
        name="sg_merge_out",
    )(*ys, zf, zf, x2, sg_ln_g, sg_ln_b, ws, bs_b, w_gate, b_gate, w_branch, w_out, ln_g, ln_b)


def _mlp_kernel(x_ref, w1_ref, b1_ref, w2_ref, b2_ref, lg_ref, lb_ref, o_ref):
    def finish(rows, acc):
        o_ref[rows, :] = _layer_norm(ALPHA * x_ref[rows, :] + (acc + b2_ref[...]), lg_ref[...], lb_ref[...])

    done = []
    for r in range(MLP_TM // MLP_SUB):
        rows = slice(r * MLP_SUB, (r + 1) * MLP_SUB)
        xb = x_ref[rows, :].astype(BF16)
        acc = None
        for c in range(D_FF // MLP_FC):
            cols = slice(c * MLP_FC, (c + 1) * MLP_FC)
            hid = jnp.maximum(_dot(xb, w1_ref[:, cols]) + b1_ref[:, cols], 0.0)
            part = _dot((hid * hid).astype(BF16), w2_ref[cols, :])
            acc = part if acc is None else acc + part
        done.append((rows, acc))
    for rows, acc in done:
        finish(rows, acc)


def _mlp(x2, w1, b1, w2, b2, ln_g, ln_b):
    t = x2.shape[0]
    vec = pl.BlockSpec((1, D_MODEL), lambda i: (0, 0))
    return pl.pallas_call(
        _mlp_kernel,
        grid=(t // MLP_TM,),
        in_specs=[
            pl.BlockSpec((MLP_TM, D_MODEL), lambda i: (i, 0)),
            pl.BlockSpec((D_MODEL, D_FF), lambda i: (0, 0), pipeline_mode=pl.Buffered(1)),
            pl.BlockSpec((1, D_FF), lambda i: (0, 0)),
            pl.BlockSpec((D_FF, D_MODEL), lambda i: (0, 0), pipeline_mode=pl.Buffered(1)),
            vec, vec, vec,
        ],
        out_specs=pl.BlockSpec((MLP_TM, D_MODEL), lambda i: (i, 0)),
        out_shape=jax.ShapeDtypeStruct((t, D_MODEL), F32),
        compiler_params=_params("arbitrary"),
        name="mlp",
    )(x2, w1, b1, w2, b2, ln_g, ln_b)


def _block_diag(w):
    per = LRU_LANES // LRU_HD
    groups = LRU_HEADS // per
    w4 = w.reshape(groups, per, LRU_HD, LRU_HD)
    eye = jnp.eye(per, dtype=w.dtype)
    return jnp.einsum('ghij,hk->ghikj', w4, eye).reshape(groups, LRU_LANES, LRU_LANES)


def _rope_tables(seq):
    half = RET_DK // 2
    pos = np.arange(seq, dtype=np.float32)
    inv_freq = (np.float32(ROPE_THETA) ** (-np.arange(half, dtype=np.float32) / np.float32(half))).astype(np.float32)
    ang = pos[:, None] * inv_freq[None, :]
    cos, sin = np.cos(ang), np.sin(ang)
    return (jnp.asarray(np.concatenate([cos, cos], axis=-1), F32),
            jnp.asarray(np.concatenate([-sin, sin], axis=-1), F32))


def _layer(x2, batch, seq, cos_t, sin_t, w_in, b_in, conv_w, conv_b, lru_wa, lru_ba, lru_wx,
           lru_bx, lru_lambda, sg_ln_g, sg_ln_b, sg_ws, sg_bs, w_branch, w_out, ln1_g, ln1_b,
           w1, b1, w2, b2, ln2_g, ln2_b):
    gl0 = W_COLS
    zb, zf, y_a, y_b = _project(
        x2, w_in[:, :gl0].astype(BF16), b_in[None, :gl0], cos_t, sin_t, conv_w, conv_b[None, :],
        _block_diag(lru_wa).astype(BF16), lru_ba[None, :], _block_diag(lru_wx).astype(BF16),
        lru_bx[None, :], lru_lambda[None, :], batch, seq)
    y_c = _stick_breaking(zb, batch, seq)
    bs_b = jnp.broadcast_to(sg_bs[:, :, None], (SG_GROUPS, BLOCK, BRANCH_W // SG_GROUPS))

    x1 = _merge((y_a, y_b, y_c), zf, x2, sg_ln_g[None, :], sg_ln_b[None, :], sg_ws, bs_b,
                w_in[:, gl0:].astype(BF16), b_in[None, gl0:], w_branch.astype(BF16), w_out.astype(BF16),
                ln1_g[None, :], ln1_b[None, :])
    return _mlp(x1, w1.astype(BF16), b1[None, :], w2.astype(BF16), b2[None, :],
                ln2_g[None, :], ln2_b[None, :])


def kernel(x, w_in, b_in, conv_w, conv_b, lru_wa, lru_ba, lru_wx, lru_bx, lru_lambda, sg_ln_g, sg_ln_b, sg_ws, sg_bs, w_branch, w_out, ln1_g, ln1_b, w1, b1, w2, b2, ln2_g, ln2_b):
    batch, seq, d = x.shape
    cos_t, sin_t = _rope_tables(seq)
    x2 = x.reshape(batch * seq, d)
    per_layer = (w_in, b_in, conv_w, conv_b, lru_wa, lru_ba, lru_wx, lru_bx, lru_lambda, sg_ln_g,
                 sg_ln_b, sg_ws, sg_bs, w_branch, w_out, ln1_g, ln1_b, w1, b1, w2, b2, ln2_g, ln2_b)
    for l in range(DEPTH):
        x2 = _layer(x2, batch, seq, cos_t, sin_t, *(p[l] for p in per_layer))
    return x2.reshape(batch, seq, d)
```

```python
import functools
import math

import jax
import jax.numpy as jnp
import numpy as np
from jax import lax
from jax.experimental import pallas as pl
from jax.experimental.pallas import tpu as pltpu

F32 = jnp.float32
BF16 = jnp.bfloat16

D_MODEL = 1024
DEPTH = 2
BLOCK = 128
BRANCH_W = D_MODEL // 2
N_BRANCH = 4
RET_HEADS = 4
RET_DK = BRANCH_W // RET_HEADS
ROPE_THETA = 10000.0
LRU_HEADS = 8
LRU_HD = BRANCH_W // LRU_HEADS
CONV_W = 4
LRU_C = 8.0
SB_HEADS = 4
SB_HD = BRANCH_W // SB_HEADS
SG_GROUPS = 4
D_FF = 4 * D_MODEL
ALPHA = (2 * DEPTH) ** 0.25
LN_EPS = 1e-5
LOG2E = 1.4426950408889634
SB_EXP_ZERO = -105.0

W_RET_Q, W_RET_K, W_RET_V, W_RET_G, W_LRU_X, W_LRU_G, W_SB_Q, W_SB_K, W_SB_V, W_SG_U, W_SG_V = range(11)
W_COLS = 11 * BRANCH_W
ZF_SG_U, ZF_SG_V = 0, 1
ZF_COLS = 2 * BRANCH_W
ZB_SB_Q, ZB_SB_K, ZB_SB_V = 0, 1, 2
ZB_COLS = 3 * BRANCH_W

VMEM_LIMIT = 56 * 1024 * 1024

PROJ_TM = 512
LRU_ROWS = 128
LRU_LANES = 256
SB_TQ = 256
SB_STEP_BLOCKS = 2
MERGE_TM = 1024
MLP_TM = 1024
MLP_SUB = 512
MLP_FC = 2048


def _params(*sem):
    return pltpu.CompilerParams(dimension_semantics=sem, vmem_limit_bytes=VMEM_LIMIT)


def _dot(a, b):
    return jnp.dot(a, b, preferred_element_type=F32)


def _dot_nt(a, b):
    return lax.dot_general(a, b, (((1,), (1,)), ((), ())), preferred_element_type=F32)


def _gelu(x):
    k1 = -2.0 * math.sqrt(2.0 / math.pi) * LOG2E
    return x / (1.0 + jnp.exp2(x * (k1 + (k1 * 0.044715) * (x * x))))


def _layer_norm(h, g, b):
    mu = jnp.mean(h, axis=-1, keepdims=True)
    d = h - mu
    var = jnp.mean(d * d, axis=-1, keepdims=True)
    return d * lax.rsqrt(var + LN_EPS) * g + b


def _ret_log_g(h):
    return math.log1p(-(2.0 ** (-5.0 - h)))


def _ret_chunk(rows, zq_ref, zk_ref, zg_ref, zv_ref, cos_ref, sin_ref, ya_ref,
               state_ref, dmat_ref, qdec_ref, kdec_ref):
    heads = range(RET_HEADS)
    cols = [slice(h * RET_DK, (h + 1) * RET_DK) for h in heads]
    cs = cos_ref[rows, :]
    sn = sin_ref[rows, :]
    qr, kr = [], []
    for h in heads:
        q = zq_ref[rows, cols[h]]
        k = zk_ref[rows, cols[h]]
        qr.append(q * cs + pltpu.roll(q, RET_DK // 2, 1) * sn)
        kr.append((k * cs + pltpu.roll(k, RET_DK // 2, 1) * sn) * (RET_DK ** -0.5))
    scores = [_dot_nt(qr[h].astype(BF16), kr[h].astype(BF16)) * dmat_ref[h] for h in heads]
    cross = [_dot((qr[h] * qdec_ref[h]).astype(BF16), state_ref[h].astype(BF16)) for h in heads]
    kd_t = [jnp.transpose(kr[h] * kdec_ref[h]).astype(BF16) for h in heads]
    for h in heads:
        v = zv_ref[rows, cols[h]]
        y = _dot(scores[h].astype(BF16), v) + cross[h]
        state_ref[h] = math.exp(_ret_log_g(h) * BLOCK) * state_ref[h] + _dot(kd_t[h], v)
        mu = jnp.mean(y, axis=-1, keepdims=True)
        d = y - mu
        var = jnp.mean(d * d, axis=-1, keepdims=True)
        yn = d * lax.rsqrt(var + LN_EPS)
        ya_ref[rows, cols[h]] = (jax.nn.silu(zg_ref[rows, cols[h]]) * yn).astype(ya_ref.dtype)


def _proj_kernel(x_ref, w_ref, b_ref, cos_ref, sin_ref, cw_ref, cb_ref, wa_ref, ba_ref, wx_ref, bx_ref,
                 lam_ref, zb_ref, zf_ref, ya_ref, yb_ref,
                 zq_ref, zk_ref, zg_ref, zv_ref, state_ref, dmat_ref, qdec_ref, kdec_ref,
                 zlx_ref, zlg_ref, tail_ref, h_ref):
    @pl.when(pl.program_id(1) == 0)
    def _():
        tail_ref[...] = jnp.zeros_like(tail_ref)
        h_ref[...] = jnp.zeros_like(h_ref)
        state_ref[...] = jnp.zeros_like(state_ref)
        row = lax.broadcasted_iota(jnp.int32, (BLOCK, BLOCK), 0).astype(F32)
        col = lax.broadcasted_iota(jnp.int32, (BLOCK, BLOCK), 1).astype(F32)
        diff = row - col
        for h in range(RET_HEADS):
            lg = _ret_log_g(h)
            dmat_ref[h] = jnp.where(diff >= 0, jnp.exp(lg * jnp.maximum(diff, 0.0)), 0.0)
            qdec_ref[h] = jnp.exp(lg * (row + 1.0))
            kdec_ref[h] = jnp.exp(lg * (BLOCK - 1.0 - row))

    xb = x_ref[...].astype(BF16)

    half = BRANCH_W // 2

    def project(part, dst_ref, row0, col0, scale=None):
        def piece(hf):
            cols = slice(part * BRANCH_W + hf * half, part * BRANCH_W + (hf + 1) * half)
            acc = _dot(xb, w_ref[:, cols]) + b_ref[:, cols]
            if scale is not None:
                acc = acc * scale
            dst_ref[row0:row0 + PROJ_TM, col0 + hf * half:col0 + (hf + 1) * half] = acc.astype(dst_ref.dtype)
        return [functools.partial(piece, 0), functools.partial(piece, 1)]

    def ret(c):
        _ret_chunk(slice(c * BLOCK, (c + 1) * BLOCK), zq_ref, zk_ref, zg_ref, zv_ref, cos_ref, sin_ref,
                   ya_ref, state_ref, dmat_ref, qdec_ref, kdec_ref)

    def lru(blk, hf):
        _lru_piece(blk * LRU_ROWS, hf, zlx_ref, cw_ref, cb_ref, wa_ref, ba_ref, wx_ref, bx_ref, lam_ref,
                   zlg_ref, tail_ref, h_ref, yb_ref)

    for piece in project(W_LRU_X, zlx_ref, 0, 0) + project(W_LRU_G, zlg_ref, 0, 0):
        piece()
    matmuls = (project(W_RET_Q, zq_ref, 0, 0) + project(W_RET_K, zk_ref, 0, 0)
               + project(W_RET_G, zg_ref, 0, 0) + project(W_RET_V, zv_ref, 0, 0)
               + project(W_SG_U, zf_ref, 0, ZF_SG_U * BRANCH_W) + project(W_SG_V, zf_ref, 0, ZF_SG_V * BRANCH_W)
               + project(W_SB_Q, zb_ref, 0, ZB_SB_Q * BRANCH_W, SB_HD ** -0.5)
               + project(W_SB_K, zb_ref, 0, ZB_SB_K * BRANCH_W) + project(W_SB_V, zb_ref, 0, ZB_SB_V * BRANCH_W))
    nblk = PROJ_TM // LRU_ROWS
    lru_pieces = [functools.partial(lru, blk, hf) for blk in range(nblk)
                  for hf in range(BRANCH_W // LRU_LANES)]
    ret_pieces = [functools.partial(ret, c) for c in range(PROJ_TM // BLOCK)]
    late = len(ret_pieces)
    branch = lru_pieces[:-late]
    for c, piece in enumerate(ret_pieces):
        branch += [piece] + lru_pieces[len(lru_pieces) - late + c:len(lru_pieces) - late + c + 1]
    ret_needs = 8
    issued = 0
    for n, piece in enumerate(branch):
        if piece in ret_pieces and issued < ret_needs:
            for m in matmuls[issued:ret_needs]:
                m()
            issued = ret_needs
        piece()
        upto = max(issued, min(len(matmuls), -(-len(matmuls) * (n + 1) // len(branch))))
        for m in matmuls[issued:upto]:
            m()
        issued = upto
    for m in matmuls[issued:]:
        m()


def _project(x2, w, b, cos_t, sin_t, conv_w, conv_b, wa_bd, ba, wx_bd, bx, lam, batch, seq):
    t, d = x2.shape
    nblk = seq // PROJ_TM
    row_map = lambda bi, n: (bi * nblk + n, 0)
    whole = functools.partial(pl.BlockSpec, pipeline_mode=pl.Buffered(1))
    tile = lambda width: pl.BlockSpec((PROJ_TM, width), row_map)
    vec = pl.BlockSpec((1, BRANCH_W), lambda bi, n: (0, 0))
    mat = pl.BlockSpec((BRANCH_W // LRU_LANES, LRU_LANES, LRU_LANES), lambda bi, n: (0, 0, 0))
    tile_f32 = pltpu.VMEM((PROJ_TM, BRANCH_W), F32)
    return pl.pallas_call(
        _proj_kernel,
        grid=(batch, nblk),
        in_specs=[
            tile(d),
            whole((d, W_COLS), lambda bi, n: (0, 0)),
            pl.BlockSpec((1, W_COLS), lambda bi, n: (0, 0)),
            pl.BlockSpec((PROJ_TM, RET_DK), lambda bi, n: (n, 0)),
            pl.BlockSpec((PROJ_TM, RET_DK), lambda bi, n: (n, 0)),
            pl.BlockSpec((CONV_W, BRANCH_W), lambda bi, n: (0, 0)),
            vec, mat, vec, mat, vec, vec,
        ],
        out_specs=[tile(ZB_COLS), tile(ZF_COLS), tile(BRANCH_W), tile(BRANCH_W)],
        out_shape=[jax.ShapeDtypeStruct((t, ZB_COLS), BF16), jax.ShapeDtypeStruct((t, ZF_COLS), F32),
                   jax.ShapeDtypeStruct((t, BRANCH_W), BF16), jax.ShapeDtypeStruct((t, BRANCH_W), BF16)],
        scratch_shapes=[tile_f32] * 3 + [pltpu.VMEM((PROJ_TM, BRANCH_W), BF16)]
        + [pltpu.VMEM((RET_HEADS, RET_DK, RET_DK), F32)] * 4
        + [tile_f32, tile_f32, pltpu.VMEM((8 * (CONV_W - 1), BRANCH_W), F32), pltpu.VMEM((8, BRANCH_W), F32)],
        compiler_params=_params("arbitrary", "arbitrary"),
        name="in_proj_retention_lru",
    )(x2, w, b, cos_t, sin_t, conv_w, conv_b, wa_bd, ba, wx_bd, bx, lam)


def _lru_piece(r0, hf, zlx_ref, cw_ref, cb_ref, wa_ref, ba_ref, wx_ref, bx_ref, lam_ref, gate_ref,
               tail_ref, h_ref, o_ref):
    rows = LRU_ROWS
    per = rows // 8
    half = LRU_LANES
    lanes = slice(hf * half, (hf + 1) * half)
    xp = pltpu.einshape("(sj)c->(js)c", zlx_ref[r0:r0 + rows, lanes], s=8)
    sub = lax.broadcasted_iota(jnp.int32, (8, half), 0)

    def delayed(k):
        heads = []
        for m in range(per - k, per):
            prev = tail_ref[8 * (m - (per - 3)):8 * (m - (per - 3)) + 8, lanes]
            heads.append(pltpu.roll(jnp.where(sub == 7, prev, xp[8 * m:8 * m + 8, :]), 1, 0))
        return jnp.concatenate(heads + [xp[0:8 * (per - k), :]], axis=0)

    xc = cb_ref[:, lanes] + cw_ref[CONV_W - 1:CONV_W, lanes] * xp
    for k in range(1, CONV_W):
        xc = xc + cw_ref[CONV_W - 1 - k:CONV_W - k, lanes] * delayed(k)
    tail_ref[:, lanes] = xp[8 * (per - 3):8 * per, :]

    xcb = xc.astype(BF16)
    r = jax.nn.sigmoid(_dot(xcb, wa_ref[hf]) + ba_ref[:, lanes])
    i = jax.nn.sigmoid(_dot(xcb, wx_ref[hf]) + bx_ref[:, lanes])
    nlam = -lam_ref[:, lanes]
    softplus = jnp.maximum(nlam, 0.0) + jnp.log1p(jnp.exp(-jnp.abs(nlam)))
    log_a = -LRU_C * r * softplus
    a_all = jnp.exp(log_a)
    u_all = jnp.sqrt(jnp.tanh(-log_a) * (1.0 + a_all * a_all)) * (i * xc)

    sub1 = sub[:, 0:128]
    slabs = []
    for s in range(half // 128):
        sl = slice(s * 128, (s + 1) * 128)
        cols = slice(hf * half + s * 128, hf * half + (s + 1) * 128)
        run_h, run_p = [u_all[0:8, sl]], [a_all[0:8, sl]]
        for j in range(1, per):
            aj = a_all[8 * j:8 * j + 8, sl]
            run_h.append(aj * run_h[-1] + u_all[8 * j:8 * j + 8, sl])
            run_p.append(aj * run_p[-1])
        tot_h, tot_p = run_h[-1], run_p[-1]
        for shift in (1, 2, 4):
            keep = sub1 >= shift
            p_sh = jnp.where(keep, pltpu.roll(tot_p, shift, 0), 1.0)
            h_sh = jnp.where(keep, pltpu.roll(tot_h, shift, 0), 0.0)
            tot_h = tot_p * h_sh + tot_h
            tot_p = tot_p * p_sh
        hb = h_ref[:, cols]
        ends = tot_h + tot_p * hb
        start = jnp.where(sub1 == 0, hb, pltpu.roll(ends, 1, 0))
        h_ref[:, cols] = jnp.broadcast_to(ends[7:8, :], (8, 128))
        slabs.append(jnp.concatenate([run_h[j] + run_p[j] * start for j in range(per)], axis=0))
    h = pltpu.einshape("(js)c->(sj)c", jnp.concatenate(slabs, axis=1), s=8)
    o_ref[r0:r0 + rows, lanes] = (h * _gelu(gate_ref[r0:r0 + rows, lanes])).astype(o_ref.dtype)


def _sb_key_blocks(q_ref, qrows, k_ref, v_ref, acc_ref, carry_ref, tri, offs, diag_mask):
    t = SB_TQ
    heads = range(SB_HEADS)
    blocks = range(len(offs))
    cols = [slice(h * SB_HD, (h + 1) * SB_HD) for h in heads]
    zs = [[_dot_nt(q_ref[qrows, cols[h]], k_ref[pl.ds(offs[b], t), cols[h]]) for b in blocks] for h in heads]
    cums = [[None] * len(offs) for _ in heads]
    for h in heads:
        for b in blocks:
            z = zs[h][b]
            lg = jnp.log(1.0 + jnp.exp2(jnp.abs(z) * (-LOG2E)))
            log_1m = jnp.minimum(-z, 0.0) - lg
            if b == 0 and diag_mask is not None:
                log_1m = jnp.where(diag_mask, log_1m, 0.0)
            hi = log_1m.astype(BF16)
            lo = (log_1m - hi.astype(F32)).astype(BF16)
            cums[h][b] = _dot(hi, tri) + _dot(lo, tri)
    for h in heads:
        carry = None if diag_mask is not None else carry_ref[h]
        pv = None
        for b in blocks:
            e = zs[h][b] + cums[h][b]
            if carry is not None:
                e = e + jnp.concatenate([carry] * (t // 128), axis=1)
            w = jnp.exp(e)
            if b == 0 and diag_mask is not None:
                w = jnp.where(diag_mask, w, 0.0)
            d = _dot(w.astype(BF16), v_ref[pl.ds(offs[b], t), cols[h]])
            pv = d if pv is None else pv + d
            block_sum = cums[h][b][:, 0:1]
            carry = jnp.broadcast_to(block_sum, (t, 128)) if carry is None else carry + block_sum
        carry_ref[h] = carry
        if diag_mask is not None:
            acc_ref[:, cols[h]] = pv
        else:
            acc_ref[:, cols[h]] += pv


def _sb_kernel(q_ref, k_ref, v_ref, o_ref, acc_ref, carry_ref):
    t = SB_TQ
    row = lax.broadcasted_iota(jnp.int32, (t, t), 0)
    col = lax.broadcasted_iota(jnp.int32, (t, t), 1)
    tri = (row >= col).astype(BF16)

    def any_weight_left():
        m = carry_ref[0]
        for h in range(1, SB_HEADS):
            m = jnp.maximum(m, carry_ref[h])
        return jnp.max(m) > SB_EXP_ZERO

    def query_block(j, _):
        i = pl.program_id(1) * SB_STEP_BLOCKS + j
        qrows = pl.ds(pl.multiple_of(j * t, t), t)
        diag = pl.multiple_of(i * t, t)

        @pl.when(i == 0)
        def _():
            _sb_key_blocks(q_ref, qrows, k_ref, v_ref, acc_ref, carry_ref, tri, [diag], col < row)

        @pl.when(i > 0)
        def _():
            prev = pl.multiple_of((i - 1) * t, t)
            _sb_key_blocks(q_ref, qrows, k_ref, v_ref, acc_ref, carry_ref, tri, [diag, prev], col < row)

        def cond(state):
            step, live = state
            return jnp.logical_and(step < i, live)

        def body(state):
            step, _ = state
            off = pl.multiple_of((i - 1 - step) * t, t)
            _sb_key_blocks(q_ref, qrows, k_ref, v_ref, acc_ref, carry_ref, tri, [off], None)
            return step + 1, any_weight_left()

        lax.while_loop(cond, body, (jnp.int32(1), any_weight_left()))
        o_ref[qrows, :] = acc_ref[...].astype(o_ref.dtype)
        return 0

    lax.fori_loop(0, SB_STEP_BLOCKS, query_block, 0)


def _stick_breaking(zb, batch, seq):
    rows = SB_TQ * SB_STEP_BLOCKS
    zb3 = zb.reshape(batch, seq, ZB_COLS)
    return pl.pallas_call(
        _sb_kernel,
        grid=(batch, seq // rows),
        in_specs=[
            pl.BlockSpec((None, rows, BRANCH_W), lambda b, i: (b, i, ZB_SB_Q)),
            pl.BlockSpec((None, seq, BRANCH_W), lambda b, i: (b, 0, ZB_SB_K)),
            pl.BlockSpec((None, seq, BRANCH_W), lambda b, i: (b, 0, ZB_SB_V)),
        ],
        out_specs=pl.BlockSpec((None, rows, BRANCH_W), lambda b, i: (b, i, 0)),
        out_shape=jax.ShapeDtypeStruct((batch, seq, BRANCH_W), BF16),
        scratch_shapes=[pltpu.VMEM((SB_TQ, BRANCH_W), F32), pltpu.VMEM((SB_HEADS, SB_TQ, 128), F32)],
        compiler_params=_params("arbitrary", "arbitrary"),
        name="stick_breaking",
    )(zb3, zb3, zb3).reshape(batch * seq, BRANCH_W)


def _sg_chunk(rows, u_ref, v_ref, g_ref, b_ref, ws_ref, bs_ref, o_ref):
    u = _gelu(u_ref[rows, :])
    v = _layer_norm(_gelu(v_ref[rows, :]), g_ref[...], b_ref[...]).astype(BF16)
    row = lax.broadcasted_iota(jnp.int32, (BLOCK, BLOCK), 0)
    col = lax.broadcasted_iota(jnp.int32, (BLOCK, BLOCK), 1)
    gd = BRANCH_W // SG_GROUPS
    for g in range(SG_GROUPS):
        w = jnp.where(col <= row, ws_ref[g], 0.0).astype(BF16)
        cols = slice(g * gd, (g + 1) * gd)
        s = _dot(w, v[:, cols]) + bs_ref[g]
        o_ref[rows, cols] = (u[:, cols] * s).astype(o_ref.dtype)


def _merge_kernel(ya_ref, yb_ref, yc_ref, u_ref, v_ref, x_ref, sg_g_ref, sg_b_ref, ws_ref, bs_ref,
                  wg_ref, bg_ref, wb_ref, wo_ref, lg_ref, lb_ref, o_ref, yd_ref):
    xb = x_ref[...].astype(BF16)
    nchunk = MERGE_TM // BLOCK
    sg_after = {0: range(0, nchunk // 2), 1: range(nchunk // 2, nchunk - 1), 2: range(nchunk - 1, nchunk)}
    merged = None
    for n, y_ref in enumerate((ya_ref, yb_ref, yc_ref, yd_ref)):
        cols = slice(n * D_MODEL, (n + 1) * D_MODEL)
        logits = _dot(xb, wg_ref[:, cols]) + bg_ref[:, cols]
        for c in sg_after.get(n, ()):
            _sg_chunk(slice(c * BLOCK, (c + 1) * BLOCK), u_ref, v_ref, sg_g_ref, sg_b_ref, ws_ref, bs_ref,
                      yd_ref)
        term = jax.nn.sigmoid(logits) * _dot(y_ref[...], wb_ref[n])
        merged = term if merged is None else merged + term
    mb = merged.astype(BF16)
    nsub = 4
    sub = MERGE_TM // nsub
    outs = [_dot(mb[r * sub:(r + 1) * sub, :], wo_ref[...]) for r in range(nsub)]
    for r in range(nsub):
        rows = slice(r * sub, (r + 1) * sub)
        o_ref[rows, :] = _layer_norm(ALPHA * x_ref[rows, :] + outs[r], lg_ref[...], lb_ref[...])


def _merge(ys, zf, x2, sg_ln_g, sg_ln_b, ws, bs_b, w_gate, b_gate, w_branch, w_out, ln_g, ln_b):
    t = x2.shape[0]
    ysp = pl.BlockSpec((MERGE_TM, BRANCH_W), lambda i: (i, 0))
    vec = pl.BlockSpec((1, D_MODEL), lambda i: (0, 0))
    hvec = pl.BlockSpec((1, BRANCH_W), lambda i: (0, 0))
    cube = pl.BlockSpec((SG_GROUPS, BLOCK, BLOCK), lambda i: (0, 0, 0))
    whole = functools.partial(pl.BlockSpec, pipeline_mode=pl.Buffered(1))
    return pl.pallas_call(
        _merge_kernel,
        grid=(t // MERGE_TM,),
        in_specs=[ysp] * 3 + [
            pl.BlockSpec((MERGE_TM, BRANCH_W), lambda i: (i, ZF_SG_U)),
            pl.BlockSpec((MERGE_TM, BRANCH_W), lambda i: (i, ZF_SG_V)),
            pl.BlockSpec((MERGE_TM, D_MODEL), lambda i: (i, 0)),
            hvec, hvec, cube, cube,
            whole((D_MODEL, N_BRANCH * D_MODEL), lambda i: (0, 0)),
            pl.BlockSpec((1, N_BRANCH * D_MODEL), lambda i: (0, 0)),
            whole((N_BRANCH, BRANCH_W, D_MODEL), lambda i: (0, 0, 0)),
            whole((D_MODEL, D_MODEL), lambda i: (0, 0)),
            vec, vec,
        ],
        out_specs=pl.BlockSpec((MERGE_TM, D_MODEL), lambda i: (i, 0)),
        out_shape=jax.ShapeDtypeStruct((t, D_MODEL), F32),
        scratch_shapes=[pltpu.VMEM((MERGE_TM, BRANCH_W), BF16)],
        compiler_params=_params("arbitrary"),
        name="sg_merge_out",
    )(*ys, zf, zf, x2, sg_ln_g, sg_ln_b, ws, bs_b, w_gate, b_gate, w_branch, w_out, ln_g, ln_b)


def _mlp_kernel(x_ref, w1_ref, b1_ref, w2_ref, b2_ref, lg_ref, lb_ref, o_ref):
    def finish(rows, acc):
        o_ref[rows, :] = _layer_norm(ALPHA * x_ref[rows, :] + (acc + b2_ref[...]), lg_ref[...], lb_ref[...])

    done = []
    for r in range(MLP_TM // MLP_SUB):
        rows = slice(r * MLP_SUB, (r + 1) * MLP_SUB)
        xb = x_ref[rows, :].astype(BF16)
        acc = None
        for c in range(D_FF // MLP_FC):
            cols = slice(c * MLP_FC, (c + 1) * MLP_FC)
            hid = jnp.maximum(_dot(xb, w1_ref[:, cols]) + b1_ref[:, cols], 0.0)
            part = _dot((hid * hid).astype(BF16), w2_ref[cols, :])
            acc = part if acc is None else acc + part
        done.append((rows, acc))
    for rows, acc in done:
        finish(rows, acc)


def _mlp(x2, w1, b1, w2, b2, ln_g, ln_b):
    t = x2.shape[0]
    vec = pl.BlockSpec((1, D_MODEL), lambda i: (0, 0))
    return pl.pallas_call(
        _mlp_kernel,
        grid=(t // MLP_TM,),
        in_specs=[
            pl.BlockSpec((MLP_TM, D_MODEL), lambda i: (i, 0)),
            pl.BlockSpec((D_MODEL, D_FF), lambda i: (0, 0), pipeline_mode=pl.Buffered(1)),
            pl.BlockSpec((1, D_FF), lambda i: (0, 0)),
            pl.BlockSpec((D_FF, D_MODEL), lambda i: (0, 0), pipeline_mode=pl.Buffered(1)),
            vec, vec, vec,
        ],
        out_specs=pl.BlockSpec((MLP_TM, D_MODEL), lambda i: (i, 0)),
        out_shape=jax.ShapeDtypeStruct((t, D_MODEL), F32),
        compiler_params=_params("arbitrary"),
        name="mlp",
    )(x2, w1, b1, w2, b2, ln_g, ln_b)


def _block_diag(w):
    per = LRU_LANES // LRU_HD
    groups = LRU_HEADS // per
    w4 = w.reshape(groups, per, LRU_HD, LRU_HD)
    eye = jnp.eye(per, dtype=w.dtype)
    return jnp.einsum('ghij,hk->ghikj', w4, eye).reshape(groups, LRU_LANES, LRU_LANES)


def _rope_tables(seq):
    half = RET_DK // 2
    pos = np.arange(seq, dtype=np.float32)
    inv_freq = (np.float32(ROPE_THETA) ** (-np.arange(half, dtype=np.float32) / np.float32(half))).astype(np.float32)
    ang = pos[:, None] * inv_freq[None, :]
    cos, sin = np.cos(ang), np.sin(ang)
    return (jnp.asarray(np.concatenate([cos, cos], axis=-1), F32),
            jnp.asarray(np.concatenate([-sin, sin], axis=-1), F32))


def _layer(x2, batch, seq, cos_t, sin_t, w_in, b_in, conv_w, conv_b, lru_wa, lru_ba, lru_wx,
           lru_bx, lru_lambda, sg_ln_g, sg_ln_b, sg_ws, sg_bs, w_branch, w_out, ln1_g, ln1_b,
           w1, b1, w2, b2, ln2_g, ln2_b):
    gl0 = W_COLS
    zb, zf, y_a, y_b = _project(
        x2, w_in[:, :gl0].astype(BF16), b_in[None, :gl0], cos_t, sin_t, conv_w, conv_b[None, :],
        _block_diag(lru_wa).astype(BF16), lru_ba[None, :], _block_diag(lru_wx).astype(BF16),
        lru_bx[None, :], lru_lambda[None, :], batch, seq)
    y_c = _stick_breaking(zb, batch, seq)
    bs_b = jnp.broadcast_to(sg_bs[:, :, None], (SG_GROUPS, BLOCK, BRANCH_W // SG_GROUPS))

    x1 = _merge((y_a, y_b, y_c), zf, x2, sg_ln_g[None, :], sg_ln_b[None, :], sg_ws, bs_b,
                w_in[:, gl0:].astype(BF16), b_in[None, gl0:], w_branch.astype(BF16), w_out.astype(BF16),
                ln1_g[None, :], ln1_b[None, :])
    return _mlp(x1, w1.astype(BF16), b1[None, :], w2.astype(BF16), b2[None, :],
                ln2_g[None, :], ln2_b[None, :])


def kernel(x, w_in, b_in, conv_w, conv_b, lru_wa, lru_ba, lru_wx, lru_bx, lru_lambda, sg_ln_g, sg_ln_b, sg_ws, sg_bs, w_branch, w_out, ln1_g, ln1_b, w1, b1, w2, b2, ln2_g, ln2_b):
    batch, seq, d = x.shape
    cos_t, sin_t = _rope_tables(seq)
    x2 = x.reshape(batch * seq, d)
    per_layer = (w_in, b_in, conv_w, conv_b, lru_wa, lru_ba, lru_wx, lru_bx, lru_lambda, sg_ln_g,
                 sg_ln_b, sg_ws, sg_bs, w_branch, w_out, ln1_g, ln1_b, w1, b1, w2, b2, ln2_g, ln2_b)
    for l in range(DEPTH):
        x2 = _layer(x2, batch, seq, cos_t, sin_t, *(p[l] for p in per_layer))
    return x2.reshape(batch, seq, d)
```

```python
import functools
import math

import jax
import jax.numpy as jnp
import numpy as np
from jax import lax
from jax.experimental import pallas as pl
from jax.experimental.pallas import tpu as pltpu

F32 = jnp.float32
BF16 = jnp.bfloat16

D_MODEL = 1024
DEPTH = 2
BLOCK = 128
BRANCH_W = D_MODEL // 2
N_BRANCH = 4
RET_HEADS = 4
RET_DK = BRANCH_W // RET_HEADS
ROPE_THETA = 10000.0
LRU_HEADS = 8
LRU_HD = BRANCH_W // LRU_HEADS
CONV_W = 4
LRU_C = 8.0
SB_HEADS = 4
SB_HD = BRANCH_W // SB_HEADS
SG_GROUPS = 4
D_FF = 4 * D_MODEL
ALPHA = (2 * DEPTH) ** 0.25
LN_EPS = 1e-5
LOG2E = 1.4426950408889634
SB_EXP_ZERO = -105.0

W_RET_Q, W_RET_K, W_RET_V, W_RET_G, W_LRU_X, W_LRU_G, W_SB_Q, W_SB_K, W_SB_V, W_SG_U, W_SG_V = range(11)
W_COLS = 11 * BRANCH_W
ZF_SG_U, ZF_SG_V = 0, 1
ZF_COLS = 2 * BRANCH_W
ZB_SB_Q, ZB_SB_K, ZB_SB_V = 0, 1, 2
ZB_COLS = 3 * BRANCH_W

VMEM_LIMIT = 56 * 1024 * 1024

PROJ_TM = 512
LRU_ROWS = 128
LRU_LANES = 256
SB_TQ = 256
SB_STEP_BLOCKS = 2
MERGE_TM = 1024
MLP_TM = 1024
MLP_SUB = 512
MLP_FC = 2048


def _params(*sem):
    return pltpu.CompilerParams(dimension_semantics=sem, vmem_limit_bytes=VMEM_LIMIT)


def _dot(a, b):
    return jnp.dot(a, b, preferred_element_type=F32)


def _dot_nt(a, b):
    return lax.dot_general(a, b, (((1,), (1,)), ((), ())), preferred_element_type=F32)


def _gelu(x):
    k1 = -2.0 * math.sqrt(2.0 / math.pi) * LOG2E
    return x / (1.0 + jnp.exp2(x * (k1 + (k1 * 0.044715) * (x * x))))


def _layer_norm(h, g, b):
    mu = jnp.mean(h, axis=-1, keepdims=True)
    d = h - mu
    var = jnp.mean(d * d, axis=-1, keepdims=True)
    return d * lax.rsqrt(var + LN_EPS) * g + b


def _ret_log_g(h):
    return math.log1p(-(2.0 ** (-5.0 - h)))


def _ret_chunk(rows, zq_ref, zk_ref, zg_ref, zv_ref, cos_ref, sin_ref, ya_ref,
               state_ref, dmat_ref, qdec_ref, kdec_ref):
    heads = range(RET_HEADS)
    cols = [slice(h * RET_DK, (h + 1) * RET_DK) for h in heads]
    cs = cos_ref[rows, :]
    sn = sin_ref[rows, :]
    qr, kr = [], []
    for h in heads:
        q = zq_ref[rows, cols[h]]
        k = zk_ref[rows, cols[h]]
        qr.append(q * cs + pltpu.roll(q, RET_DK // 2, 1) * sn)
        kr.append((k * cs + pltpu.roll(k, RET_DK // 2, 1) * sn) * (RET_DK ** -0.5))
    scores = [_dot_nt(qr[h].astype(BF16), kr[h].astype(BF16)) * dmat_ref[h] for h in heads]
    cross = [_dot((qr[h] * qdec_ref[h]).astype(BF16), state_ref[h].astype(BF16)) for h in heads]
    kd_t = [jnp.transpose(kr[h] * kdec_ref[h]).astype(BF16) for h in heads]
    for h in heads:
        v = zv_ref[rows, cols[h]]
        y = _dot(scores[h].astype(BF16), v) + cross[h]
        state_ref[h] = math.exp(_ret_log_g(h) * BLOCK) * state_ref[h] + _dot(kd_t[h], v)
        mu = jnp.mean(y, axis=-1, keepdims=True)
        d = y - mu
        var = jnp.mean(d * d, axis=-1, keepdims=True)
        yn = d * lax.rsqrt(var + LN_EPS)
        ya_ref[rows, cols[h]] = (jax.nn.silu(zg_ref[rows, cols[h]]) * yn).astype(ya_ref.dtype)


def _proj_kernel(x_ref, w_ref, b_ref, cos_ref, sin_ref, cw_ref, cb_ref, wa_ref, ba_ref, wx_ref, bx_ref,
                 lam_ref, zb_ref, zf_ref, ya_ref, yb_ref,
                 zq_ref, zk_ref, zg_ref, zv_ref, state_ref, dmat_ref, qdec_ref, kdec_ref,
                 zlx_ref, zlg_ref, tail_ref, h_ref):
    @pl.when(pl.program_id(1) == 0)
    def _():
        tail_ref[...] = jnp.zeros_like(tail_ref)
        h_ref[...] = jnp.zeros_like(h_ref)
        state_ref[...] = jnp.zeros_like(state_ref)
        row = lax.broadcasted_iota(jnp.int32, (BLOCK, BLOCK), 0).astype(F32)
        col = lax.broadcasted_iota(jnp.int32, (BLOCK, BLOCK), 1).astype(F32)
        diff = row - col
        for h in range(RET_HEADS):
            lg = _ret_log_g(h)
            dmat_ref[h] = jnp.where(diff >= 0, jnp.exp(lg * jnp.maximum(diff, 0.0)), 0.0)
            qdec_ref[h] = jnp.exp(lg * (row + 1.0))
            kdec_ref[h] = jnp.exp(lg * (BLOCK - 1.0 - row))

    xb = x_ref[...].astype(BF16)

    half = BRANCH_W // 2

    def project(part, dst_ref, row0, col0, scale=None):
        def piece(hf):
            cols = slice(part * BRANCH_W + hf * half, part * BRANCH_W + (hf + 1) * half)
            acc = _dot(xb, w_ref[:, cols]) + b_ref[:, cols]
            if scale is not None:
                acc = acc * scale
            dst_ref[row0:row0 + PROJ_TM, col0 + hf * half:col0 + (hf + 1) * half] = acc.astype(dst_ref.dtype)
        return [functools.partial(piece, 0), functools.partial(piece, 1)]

    def ret(c):
        _ret_chunk(slice(c * BLOCK, (c + 1) * BLOCK), zq_ref, zk_ref, zg_ref, zv_ref, cos_ref, sin_ref,
                   ya_ref, state_ref, dmat_ref, qdec_ref, kdec_ref)

    def lru(blk, hf):
        _lru_piece(blk * LRU_ROWS, hf, zlx_ref, cw_ref, cb_ref, wa_ref, ba_ref, wx_ref, bx_ref, lam_ref,
                   zlg_ref, tail_ref, h_ref, yb_ref)

    for piece in project(W_LRU_X, zlx_ref, 0, 0) + project(W_LRU_G, zlg_ref, 0, 0):
        piece()
    matmuls = (project(W_RET_Q, zq_ref, 0, 0) + project(W_RET_K, zk_ref, 0, 0)
               + project(W_RET_G, zg_ref, 0, 0) + project(W_RET_V, zv_ref, 0, 0)
               + project(W_SG_U, zf_ref, 0, ZF_SG_U * BRANCH_W) + project(W_SG_V, zf_ref, 0, ZF_SG_V * BRANCH_W)
               + project(W_SB_Q, zb_ref, 0, ZB_SB_Q * BRANCH_W, SB_HD ** -0.5)
               + project(W_SB_K, zb_ref, 0, ZB_SB_K * BRANCH_W) + project(W_SB_V, zb_ref, 0, ZB_SB_V * BRANCH_W))
    nblk = PROJ_TM // LRU_ROWS
    lru_pieces = [functools.partial(lru, blk, hf) for blk in range(nblk)
                  for hf in range(BRANCH_W // LRU_LANES)]
    ret_pieces = [functools.partial(ret, c) for c in range(PROJ_TM // BLOCK)]
    late = len(ret_pieces)
    branch = lru_pieces[:-late]
    for c, piece in enumerate(ret_pieces):
        branch += [piece] + lru_pieces[len(lru_pieces) - late + c:len(lru_pieces) - late + c + 1]
    ret_needs = 8
    issued = 0
    for n, piece in enumerate(branch):
        if piece in ret_pieces and issued < ret_needs:
            for m in matmuls[issued:ret_needs]:
                m()
            issued = ret_needs
        piece()
        upto = max(issued, min(len(matmuls), -(-len(matmuls) * (n + 1) // len(branch))))
        for m in matmuls[issued:upto]:
            m()
        issued = upto
    for m in matmuls[issued:]:
        m()


def _project(x2, w_layers, layer, b, cos_t, sin_t, conv_w, conv_b, wa_bd, ba, wx_bd, bx, lam, batch, seq):
    t, d = x2.shape
    nblk = seq // PROJ_TM
    row_map = lambda bi, n: (bi * nblk + n, 0)
    whole = functools.partial(pl.BlockSpec, pipeline_mode=pl.Buffered(1))
    tile = lambda width: pl.BlockSpec((PROJ_TM, width), row_map)
    vec = pl.BlockSpec((1, BRANCH_W), lambda bi, n: (0, 0))
    mat = pl.BlockSpec((BRANCH_W // LRU_LANES, LRU_LANES, LRU_LANES), lambda bi, n: (0, 0, 0))
    tile_f32 = pltpu.VMEM((PROJ_TM, BRANCH_W), F32)
    return pl.pallas_call(
        _proj_kernel,
        grid=(batch, nblk),
        in_specs=[
            tile(d),
            whole((None, d, W_COLS), lambda bi, n: (layer, 0, 0)),
            pl.BlockSpec((1, W_COLS), lambda bi, n: (0, 0)),
            pl.BlockSpec((PROJ_TM, RET_DK), lambda bi, n: (n, 0)),
            pl.BlockSpec((PROJ_TM, RET_DK), lambda bi, n: (n, 0)),
            pl.BlockSpec((CONV_W, BRANCH_W), lambda bi, n: (0, 0)),
            vec, mat, vec, mat, vec, vec,
        ],
        out_specs=[tile(ZB_COLS), tile(ZF_COLS), tile(BRANCH_W), tile(BRANCH_W)],
        out_shape=[jax.ShapeDtypeStruct((t, ZB_COLS), BF16), jax.ShapeDtypeStruct((t, ZF_COLS), F32),
                   jax.ShapeDtypeStruct((t, BRANCH_W), BF16), jax.ShapeDtypeStruct((t, BRANCH_W), BF16)],
        scratch_shapes=[tile_f32] * 3 + [pltpu.VMEM((PROJ_TM, BRANCH_W), BF16)]
        + [pltpu.VMEM((RET_HEADS, RET_DK, RET_DK), F32)] * 4
        + [tile_f32, tile_f32, pltpu.VMEM((8 * (CONV_W - 1), BRANCH_W), F32), pltpu.VMEM((8, BRANCH_W), F32)],
        compiler_params=_params("arbitrary", "arbitrary"),
        name="in_proj_retention_lru",
    )(x2, w_layers, b, cos_t, sin_t, conv_w, conv_b, wa_bd, ba, wx_bd, bx, lam)


def _lru_piece(r0, hf, zlx_ref, cw_ref, cb_ref, wa_ref, ba_ref, wx_ref, bx_ref, lam_ref, gate_ref,
               tail_ref, h_ref, o_ref):
    rows = LRU_ROWS
    per = rows // 8
    half = LRU_LANES
    lanes = slice(hf * half, (hf + 1) * half)
    xp = pltpu.einshape("(sj)c->(js)c", zlx_ref[r0:r0 + rows, lanes], s=8)
    sub = lax.broadcasted_iota(jnp.int32, (8, half), 0)

    def delayed(k):
        heads = []
        for m in range(per - k, per):
            prev = tail_ref[8 * (m - (per - 3)):8 * (m - (per - 3)) + 8, lanes]
            heads.append(pltpu.roll(jnp.where(sub == 7, prev, xp[8 * m:8 * m + 8, :]), 1, 0))
        return jnp.concatenate(heads + [xp[0:8 * (per - k), :]], axis=0)

    xc = cb_ref[:, lanes] + cw_ref[CONV_W - 1:CONV_W, lanes] * xp
    for k in range(1, CONV_W):
        xc = xc + cw_ref[CONV_W - 1 - k:CONV_W - k, lanes] * delayed(k)
    tail_ref[:, lanes] = xp[8 * (per - 3):8 * per, :]

    xcb = xc.astype(BF16)
    r = jax.nn.sigmoid(_dot(xcb, wa_ref[hf]) + ba_ref[:, lanes])
    i = jax.nn.sigmoid(_dot(xcb, wx_ref[hf]) + bx_ref[:, lanes])
    nlam = -lam_ref[:, lanes]
    softplus = jnp.maximum(nlam, 0.0) + jnp.log1p(jnp.exp(-jnp.abs(nlam)))
    log_a = -LRU_C * r * softplus
    a_all = jnp.exp(log_a)
    u_all = jnp.sqrt(jnp.tanh(-log_a) * (1.0 + a_all * a_all)) * (i * xc)

    sub1 = sub[:, 0:128]
    slabs = []
    for s in range(half // 128):
        sl = slice(s * 128, (s + 1) * 128)
        cols = slice(hf * half + s * 128, hf * half + (s + 1) * 128)
        run_h, run_p = [u_all[0:8, sl]], [a_all[0:8, sl]]
        for j in range(1, per):
            aj = a_all[8 * j:8 * j + 8, sl]
            run_h.append(aj * run_h[-1] + u_all[8 * j:8 * j + 8, sl])
            run_p.append(aj * run_p[-1])
        tot_h, tot_p = run_h[-1], run_p[-1]
        for shift in (1, 2, 4):
            keep = sub1 >= shift
            p_sh = jnp.where(keep, pltpu.roll(tot_p, shift, 0), 1.0)
            h_sh = jnp.where(keep, pltpu.roll(tot_h, shift, 0), 0.0)
            tot_h = tot_p * h_sh + tot_h
            tot_p = tot_p * p_sh
        hb = h_ref[:, cols]
        ends = tot_h + tot_p * hb
        start = jnp.where(sub1 == 0, hb, pltpu.roll(ends, 1, 0))
        h_ref[:, cols] = jnp.broadcast_to(ends[7:8, :], (8, 128))
        slabs.append(jnp.concatenate([run_h[j] + run_p[j] * start for j in range(per)], axis=0))
    h = pltpu.einshape("(js)c->(sj)c", jnp.concatenate(slabs, axis=1), s=8)
    o_ref[r0:r0 + rows, lanes] = (h * _gelu(gate_ref[r0:r0 + rows, lanes])).astype(o_ref.dtype)


def _sb_key_blocks(q_ref, qrows, k_ref, v_ref, acc_ref, carry_ref, tri, offs, diag_mask):
    t = SB_TQ
    heads = range(SB_HEADS)
    blocks = range(len(offs))
    cols = [slice(h * SB_HD, (h + 1) * SB_HD) for h in heads]
    zs = [[_dot_nt(q_ref[qrows, cols[h]], k_ref[pl.ds(offs[b], t), cols[h]]) for b in blocks] for h in heads]
    cums = [[None] * len(offs) for _ in heads]
    for h in heads:
        for b in blocks:
            z = zs[h][b]
            lg = jnp.log(1.0 + jnp.exp2(jnp.abs(z) * (-LOG2E)))
            log_1m = jnp.minimum(-z, 0.0) - lg
            if b == 0 and diag_mask is not None:
                log_1m = jnp.where(diag_mask, log_1m, 0.0)
            hi = log_1m.astype(BF16)
            lo = (log_1m - hi.astype(F32)).astype(BF16)
            cums[h][b] = _dot(hi, tri) + _dot(lo, tri)
    for h in heads:
        carry = None if diag_mask is not None else carry_ref[h]
        pv = None
        for b in blocks:
            e = zs[h][b] + cums[h][b]
            if carry is not None:
                e = e + jnp.concatenate([carry] * (t // 128), axis=1)
            w = jnp.exp(e)
            if b == 0 and diag_mask is not None:
                w = jnp.where(diag_mask, w, 0.0)
            d = _dot(w.astype(BF16), v_ref[pl.ds(offs[b], t), cols[h]])
            pv = d if pv is None else pv + d
            block_sum = cums[h][b][:, 0:1]
            carry = jnp.broadcast_to(block_sum, (t, 128)) if carry is None else carry + block_sum
        carry_ref[h] = carry
        if diag_mask is not None:
            acc_ref[:, cols[h]] = pv
        else:
            acc_ref[:, cols[h]] += pv


def _sb_kernel(q_ref, k_ref, v_ref, o_ref, acc_ref, carry_ref):
    t = SB_TQ
    row = lax.broadcasted_iota(jnp.int32, (t, t), 0)
    col = lax.broadcasted_iota(jnp.int32, (t, t), 1)
    tri = (row >= col).astype(BF16)

    def any_weight_left():
        m = carry_ref[0]
        for h in range(1, SB_HEADS):
            m = jnp.maximum(m, carry_ref[h])
        return jnp.max(m) > SB_EXP_ZERO

    def query_block(j, _):
        i = pl.program_id(1) * SB_STEP_BLOCKS + j
        qrows = pl.ds(pl.multiple_of(j * t, t), t)
        diag = pl.multiple_of(i * t, t)

        @pl.when(i == 0)
        def _():
            _sb_key_blocks(q_ref, qrows, k_ref, v_ref, acc_ref, carry_ref, tri, [diag], col < row)

        @pl.when(i > 0)
        def _():
            prev = pl.multiple_of((i - 1) * t, t)
            _sb_key_blocks(q_ref, qrows, k_ref, v_ref, acc_ref, carry_ref, tri, [diag, prev], col < row)

        def cond(state):
            step, live = state
            return jnp.logical_and(step < i, live)

        def body(state):
            step, _ = state
            off = pl.multiple_of((i - 1 - step) * t, t)
            _sb_key_blocks(q_ref, qrows, k_ref, v_ref, acc_ref, carry_ref, tri, [off], None)
            return step + 1, any_weight_left()

        lax.while_loop(cond, body, (jnp.int32(1), any_weight_left()))
        o_ref[qrows, :] = acc_ref[...].astype(o_ref.dtype)
        return 0

    lax.fori_loop(0, SB_STEP_BLOCKS, query_block, 0)


def _stick_breaking(zb, batch, seq):
    rows = SB_TQ * SB_STEP_BLOCKS
    zb3 = zb.reshape(batch, seq, ZB_COLS)
    return pl.pallas_call(
        _sb_kernel,
        grid=(batch, seq // rows),
        in_specs=[
            pl.BlockSpec((None, rows, BRANCH_W), lambda b, i: (b, i, ZB_SB_Q)),
            pl.BlockSpec((None, seq, BRANCH_W), lambda b, i: (b, 0, ZB_SB_K)),
            pl.BlockSpec((None, seq, BRANCH_W), lambda b, i: (b, 0, ZB_SB_V)),
        ],
        out_specs=pl.BlockSpec((None, rows, BRANCH_W), lambda b, i: (b, i, 0)),
        out_shape=jax.ShapeDtypeStruct((batch, seq, BRANCH_W), BF16),
        scratch_shapes=[pltpu.VMEM((SB_TQ, BRANCH_W), F32), pltpu.VMEM((SB_HEADS, SB_TQ, 128), F32)],
        compiler_params=_params("arbitrary", "arbitrary"),
        name="stick_breaking",
    )(zb3, zb3, zb3).reshape(batch * seq, BRANCH_W)


def _sg_chunk(rows, u_ref, v_ref, g_ref, b_ref, ws_ref, bs_ref, o_ref):
    u = _gelu(u_ref[rows, :])
    v = _layer_norm(_gelu(v_ref[rows, :]), g_ref[...], b_ref[...]).astype(BF16)
    row = lax.broadcasted_iota(jnp.int32, (BLOCK, BLOCK), 0)
    col = lax.broadcasted_iota(jnp.int32, (BLOCK, BLOCK), 1)
    gd = BRANCH_W // SG_GROUPS
    for g in range(SG_GROUPS):
        w = jnp.where(col <= row, ws_ref[g], 0.0).astype(BF16)
        cols = slice(g * gd, (g + 1) * gd)
        s = _dot(w, v[:, cols]) + bs_ref[g]
        o_ref[rows, cols] = (u[:, cols] * s).astype(o_ref.dtype)


def _merge_kernel(ya_ref, yb_ref, yc_ref, u_ref, v_ref, x_ref, sg_g_ref, sg_b_ref, ws_ref, bs_ref,
                  wg_ref, bg_ref, wb_ref, wo_ref, lg_ref, lb_ref, o_ref, yd_ref):
    xb = x_ref[...].astype(BF16)
    nchunk = MERGE_TM // BLOCK
    sg_after = {0: range(0, nchunk // 2), 1: range(nchunk // 2, nchunk - 1), 2: range(nchunk - 1, nchunk)}
    merged = None
    for n, y_ref in enumerate((ya_ref, yb_ref, yc_ref, yd_ref)):
        cols = slice(n * D_MODEL, (n + 1) * D_MODEL)
        logits = _dot(xb, wg_ref[:, cols]) + bg_ref[:, cols]
        for c in sg_after.get(n, ()):
            _sg_chunk(slice(c * BLOCK, (c + 1) * BLOCK), u_ref, v_ref, sg_g_ref, sg_b_ref, ws_ref, bs_ref,
                      yd_ref)
        term = jax.nn.sigmoid(logits) * _dot(y_ref[...], wb_ref[n])
        merged = term if merged is None else merged + term
    mb = merged.astype(BF16)
    nsub = 4
    sub = MERGE_TM // nsub
    outs = [_dot(mb[r * sub:(r + 1) * sub, :], wo_ref[...]) for r in range(nsub)]
    for r in range(nsub):
        rows = slice(r * sub, (r + 1) * sub)
        o_ref[rows, :] = _layer_norm(ALPHA * x_ref[rows, :] + outs[r], lg_ref[...], lb_ref[...])


def _merge(ys, zf, x2, sg_ln_g, sg_ln_b, ws, bs_b, layer, w_gate, b_gate, w_branch, w_out, ln_g, ln_b):
    t = x2.shape[0]
    ysp = pl.BlockSpec((MERGE_TM, BRANCH_W), lambda i: (i, 0))
    vec = pl.BlockSpec((1, D_MODEL), lambda i: (0, 0))
    hvec = pl.BlockSpec((1, BRANCH_W), lambda i: (0, 0))
    cube = pl.BlockSpec((SG_GROUPS, BLOCK, BLOCK), lambda i: (0, 0, 0))
    whole = functools.partial(pl.BlockSpec, pipeline_mode=pl.Buffered(1))
    return pl.pallas_call(
        _merge_kernel,
        grid=(t // MERGE_TM,),
        in_specs=[ysp] * 3 + [
            pl.BlockSpec((MERGE_TM, BRANCH_W), lambda i: (i, ZF_SG_U)),
            pl.BlockSpec((MERGE_TM, BRANCH_W), lambda i: (i, ZF_SG_V)),
            pl.BlockSpec((MERGE_TM, D_MODEL), lambda i: (i, 0)),
            hvec, hvec, cube, cube,
            whole((None, D_MODEL, N_BRANCH * D_MODEL), lambda i: (layer, 0, 0)),
            pl.BlockSpec((1, N_BRANCH * D_MODEL), lambda i: (0, 0)),
            whole((None, N_BRANCH, BRANCH_W, D_MODEL), lambda i: (layer, 0, 0, 0)),
            whole((None, D_MODEL, D_MODEL), lambda i: (layer, 0, 0)),
            vec, vec,
        ],
        out_specs=pl.BlockSpec((MERGE_TM, D_MODEL), lambda i: (i, 0)),
        out_shape=jax.ShapeDtypeStruct((t, D_MODEL), F32),
        scratch_shapes=[pltpu.VMEM((MERGE_TM, BRANCH_W), BF16)],
        compiler_params=_params("arbitrary"),
        name="sg_merge_out",
    )(*ys, zf, zf, x2, sg_ln_g, sg_ln_b, ws, bs_b, w_gate, b_gate, w_branch, w_out, ln_g, ln_b)


def _mlp_kernel(x_ref, w1_ref, b1_ref, w2_ref, b2_ref, lg_ref, lb_ref, o_ref):
    def finish(rows, acc):
        o_ref[rows, :] = _layer_norm(ALPHA * x_ref[rows, :] + (acc + b2_ref[...]), lg_ref[...], lb_ref[...])

    done = []
    for r in range(MLP_TM // MLP_SUB):
        rows = slice(r * MLP_SUB, (r + 1) * MLP_SUB)
        xb = x_ref[rows, :].astype(BF16)
        acc = None
        for c in range(D_FF // MLP_FC):
            cols = slice(c * MLP_FC, (c + 1) * MLP_FC)
            hid = jnp.maximum(_dot(xb, w1_ref[:, cols]) + b1_ref[:, cols], 0.0)
            part = _dot((hid * hid).astype(BF16), w2_ref[cols, :])
            acc = part if acc is None else acc + part
        done.append((rows, acc))
    for rows, acc in done:
        finish(rows, acc)


def _mlp(x2, layer, w1, b1, w2, b2, ln_g, ln_b):
    t = x2.shape[0]
    vec = pl.BlockSpec((1, D_MODEL), lambda i: (0, 0))
    return pl.pallas_call(
        _mlp_kernel,
        grid=(t // MLP_TM,),
        in_specs=[
            pl.BlockSpec((MLP_TM, D_MODEL), lambda i: (i, 0)),
            pl.BlockSpec((None, D_MODEL, D_FF), lambda i: (layer, 0, 0), pipeline_mode=pl.Buffered(1)),
            pl.BlockSpec((1, D_FF), lambda i: (0, 0)),
            pl.BlockSpec((None, D_FF, D_MODEL), lambda i: (layer, 0, 0), pipeline_mode=pl.Buffered(1)),
            vec, vec, vec,
        ],
        out_specs=pl.BlockSpec((MLP_TM, D_MODEL), lambda i: (i, 0)),
        out_shape=jax.ShapeDtypeStruct((t, D_MODEL), F32),
        compiler_params=_params("arbitrary"),
        name="mlp",
    )(x2, w1, b1, w2, b2, ln_g, ln_b)


def _block_diag(w):
    per = LRU_LANES // LRU_HD
    groups = LRU_HEADS // per
    w4 = w.reshape(groups, per, LRU_HD, LRU_HD)
    eye = jnp.eye(per, dtype=w.dtype)
    return jnp.einsum('ghij,hk->ghikj', w4, eye).reshape(groups, LRU_LANES, LRU_LANES)


def _rope_tables(seq):
    half = RET_DK // 2
    pos = np.arange(seq, dtype=np.float32)
    inv_freq = (np.float32(ROPE_THETA) ** (-np.arange(half, dtype=np.float32) / np.float32(half))).astype(np.float32)
    ang = pos[:, None] * inv_freq[None, :]
    cos, sin = np.cos(ang), np.sin(ang)
    return (jnp.asarray(np.concatenate([cos, cos], axis=-1), F32),
            jnp.asarray(np.concatenate([-sin, sin], axis=-1), F32))


def _layer(x2, batch, seq, cos_t, sin_t, layer, big, b_in, conv_w, conv_b, lru_wa, lru_ba, lru_wx,
           lru_bx, lru_lambda, sg_ln_g, sg_ln_b, sg_ws, sg_bs, ln1_g, ln1_b, b1, b2, ln2_g, ln2_b):
    w_proj, w_gate, w_branch, w_out, w1, w2 = big
    gl0 = W_COLS
    zb, zf, y_a, y_b = _project(
        x2, w_proj, layer, b_in[None, :gl0], cos_t, sin_t, conv_w, conv_b[None, :],
        _block_diag(lru_wa).astype(BF16), lru_ba[None, :], _block_diag(lru_wx).astype(BF16),
        lru_bx[None, :], lru_lambda[None, :], batch, seq)
    y_c = _stick_breaking(zb, batch, seq)
    bs_b = jnp.broadcast_to(sg_bs[:, :, None], (SG_GROUPS, BLOCK, BRANCH_W // SG_GROUPS))

    x1 = _merge((y_a, y_b, y_c), zf, x2, sg_ln_g[None, :], sg_ln_b[None, :], sg_ws, bs_b, layer,
                w_gate, b_in[None, gl0:], w_branch, w_out, ln1_g[None, :], ln1_b[None, :])
    return _mlp(x1, layer, w1, b1[None, :], w2, b2[None, :], ln2_g[None, :], ln2_b[None, :])


def kernel(x, w_in, b_in, conv_w, conv_b, lru_wa, lru_ba, lru_wx, lru_bx, lru_lambda, sg_ln_g, sg_ln_b, sg_ws, sg_bs, w_branch, w_out, ln1_g, ln1_b, w1, b1, w2, b2, ln2_g, ln2_b):
    batch, seq, d = x.shape
    cos_t, sin_t = _rope_tables(seq)
    x2 = x.reshape(batch * seq, d)
    big = (w_in[:, :, :W_COLS].astype(BF16), w_in[:, :, W_COLS:].astype(BF16), w_branch.astype(BF16),
           w_out.astype(BF16), w1.astype(BF16), w2.astype(BF16))
    per_layer = (b_in, conv_w, conv_b, lru_wa, lru_ba, lru_wx, lru_bx, lru_lambda, sg_ln_g,
                 sg_ln_b, sg_ws, sg_bs, ln1_g, ln1_b, b1, b2, ln2_g, ln2_b)
    for l in range(DEPTH):
        x2 = _layer(x2, batch, seq, cos_t, sin_t, l, big, *(p[l] for p in per_layer))
    return x2.reshape(batch, seq, d)
```

```python
import functools
import math

import jax
import jax.numpy as jnp
import numpy as np
from jax import lax
from jax.experimental import pallas as pl
from jax.experimental.pallas import tpu as pltpu

F32 = jnp.float32
BF16 = jnp.bfloat16

D_MODEL = 1024
DEPTH = 2
BLOCK = 128
BRANCH_W = D_MODEL // 2
N_BRANCH = 4
RET_HEADS = 4
RET_DK = BRANCH_W // RET_HEADS
ROPE_THETA = 10000.0
LRU_HEADS = 8
LRU_HD = BRANCH_W // LRU_HEADS
CONV_W = 4
LRU_C = 8.0
SB_HEADS = 4
SB_HD = BRANCH_W // SB_HEADS
SG_GROUPS = 4
D_FF = 4 * D_MODEL
ALPHA = (2 * DEPTH) ** 0.25
LN_EPS = 1e-5
LOG2E = 1.4426950408889634
SB_EXP_ZERO = -105.0

W_RET_Q, W_RET_K, W_RET_V, W_RET_G, W_LRU_X, W_LRU_G, W_SB_Q, W_SB_K, W_SB_V, W_SG_U, W_SG_V = range(11)
W_COLS = 11 * BRANCH_W
ZF_SG_U, ZF_SG_V = 0, 1
ZF_COLS = 2 * BRANCH_W
ZB_SB_Q, ZB_SB_K, ZB_SB_V = 0, 1, 2
ZB_COLS = 3 * BRANCH_W

VMEM_LIMIT = 56 * 1024 * 1024

PROJ_TM = 512
LRU_ROWS = 128
LRU_LANES = 256
SB_TQ = 256
SB_STEP_BLOCKS = 2
MERGE_TM = 1024
MLP_TM = 1024
MLP_SUB = 512
MLP_FC = 2048


def _params(*sem):
    return pltpu.CompilerParams(dimension_semantics=sem, vmem_limit_bytes=VMEM_LIMIT)


def _dot(a, b):
    return jnp.dot(a, b, preferred_element_type=F32)


def _dot_nt(a, b):
    return lax.dot_general(a, b, (((1,), (1,)), ((), ())), preferred_element_type=F32)


def _gelu(x):
    k1 = -2.0 * math.sqrt(2.0 / math.pi) * LOG2E
    return x / (1.0 + jnp.exp2(x * (k1 + (k1 * 0.044715) * (x * x))))


def _layer_norm(h, g, b):
    mu = jnp.mean(h, axis=-1, keepdims=True)
    d = h - mu
    var = jnp.mean(d * d, axis=-1, keepdims=True)
    return d * lax.rsqrt(var + LN_EPS) * g + b


def _ret_log_g(h):
    return math.log1p(-(2.0 ** (-5.0 - h)))


def _ret_chunk(rows, zq_ref, zk_ref, zg_ref, zv_ref, cos_ref, sin_ref, ya_ref,
               state_ref, dmat_ref, qdec_ref, kdec_ref):
    heads = range(RET_HEADS)
    cols = [slice(h * RET_DK, (h + 1) * RET_DK) for h in heads]
    cs = cos_ref[rows, :]
    sn = sin_ref[rows, :]
    qr, kr = [], []
    for h in heads:
        q = zq_ref[rows, cols[h]]
        k = zk_ref[rows, cols[h]]
        qr.append(q * cs + pltpu.roll(q, RET_DK // 2, 1) * sn)
        kr.append((k * cs + pltpu.roll(k, RET_DK // 2, 1) * sn) * (RET_DK ** -0.5))
    scores = [_dot_nt(qr[h].astype(BF16), kr[h].astype(BF16)) * dmat_ref[h] for h in heads]
    cross = [_dot((qr[h] * qdec_ref[h]).astype(BF16), state_ref[h].astype(BF16)) for h in heads]
    kd_t = [jnp.transpose(kr[h] * kdec_ref[h]).astype(BF16) for h in heads]
    for h in heads:
        v = zv_ref[rows, cols[h]]
        y = _dot(scores[h].astype(BF16), v) + cross[h]
        state_ref[h] = math.exp(_ret_log_g(h) * BLOCK) * state_ref[h] + _dot(kd_t[h], v)
        mu = jnp.mean(y, axis=-1, keepdims=True)
        d = y - mu
        var = jnp.mean(d * d, axis=-1, keepdims=True)
        yn = d * lax.rsqrt(var + LN_EPS)
        ya_ref[rows, cols[h]] = (jax.nn.silu(zg_ref[rows, cols[h]]) * yn).astype(ya_ref.dtype)


def _proj_kernel(x_ref, w_ref, b_ref, cos_ref, sin_ref, cw_ref, cb_ref, wa_ref, ba_ref, wx_ref, bx_ref,
                 lam_ref, zb_ref, zf_ref, ya_ref, yb_ref,
                 zq_ref, zk_ref, zg_ref, zv_ref, state_ref, dmat_ref, qdec_ref, kdec_ref,
                 zlx_ref, zlg_ref, tail_ref, h_ref):
    @pl.when(pl.program_id(1) == 0)
    def _():
        tail_ref[...] = jnp.zeros_like(tail_ref)
        h_ref[...] = jnp.zeros_like(h_ref)
        state_ref[...] = jnp.zeros_like(state_ref)
        row = lax.broadcasted_iota(jnp.int32, (BLOCK, BLOCK), 0).astype(F32)
        col = lax.broadcasted_iota(jnp.int32, (BLOCK, BLOCK), 1).astype(F32)
        diff = row - col
        for h in range(RET_HEADS):
            lg = _ret_log_g(h)
            dmat_ref[h] = jnp.where(diff >= 0, jnp.exp(lg * jnp.maximum(diff, 0.0)), 0.0)
            qdec_ref[h] = jnp.exp(lg * (row + 1.0))
            kdec_ref[h] = jnp.exp(lg * (BLOCK - 1.0 - row))

    xb = x_ref[...].astype(BF16)

    half = BRANCH_W // 2

    def project(part, dst_ref, row0, col0, scale=None):
        def piece(hf):
            cols = slice(part * BRANCH_W + hf * half, part * BRANCH_W + (hf + 1) * half)
            acc = _dot(xb, w_ref[:, cols]) + b_ref[:, cols]
            if scale is not None:
                acc = acc * scale
            dst_ref[row0:row0 + PROJ_TM, col0 + hf * half:col0 + (hf + 1) * half] = acc.astype(dst_ref.dtype)
        return [functools.partial(piece, 0), functools.partial(piece, 1)]

    def ret(c):
        _ret_chunk(slice(c * BLOCK, (c + 1) * BLOCK), zq_ref, zk_ref, zg_ref, zv_ref, cos_ref, sin_ref,
                   ya_ref, state_ref, dmat_ref, qdec_ref, kdec_ref)

    def lru(blk, hf):
        _lru_piece(blk * LRU_ROWS, hf, zlx_ref, cw_ref, cb_ref, wa_ref, ba_ref, wx_ref, bx_ref, lam_ref,
                   zlg_ref, tail_ref, h_ref, yb_ref)

    for piece in project(W_LRU_X, zlx_ref, 0, 0) + project(W_LRU_G, zlg_ref, 0, 0):
        piece()
    matmuls = (project(W_RET_Q, zq_ref, 0, 0) + project(W_RET_K, zk_ref, 0, 0)
               + project(W_RET_G, zg_ref, 0, 0) + project(W_RET_V, zv_ref, 0, 0)
               + project(W_SG_U, zf_ref, 0, ZF_SG_U * BRANCH_W) + project(W_SG_V, zf_ref, 0, ZF_SG_V * BRANCH_W)
               + project(W_SB_Q, zb_ref, 0, ZB_SB_Q * BRANCH_W, SB_HD ** -0.5)
               + project(W_SB_K, zb_ref, 0, ZB_SB_K * BRANCH_W) + project(W_SB_V, zb_ref, 0, ZB_SB_V * BRANCH_W))
    nblk = PROJ_TM // LRU_ROWS
    lru_pieces = [functools.partial(lru, blk, hf) for blk in range(nblk)
                  for hf in range(BRANCH_W // LRU_LANES)]
    ret_pieces = [functools.partial(ret, c) for c in range(PROJ_TM // BLOCK)]
    late = len(ret_pieces)
    branch = lru_pieces[:-late]
    for c, piece in enumerate(ret_pieces):
        branch += [piece] + lru_pieces[len(lru_pieces) - late + c:len(lru_pieces) - late + c + 1]
    ret_needs = 8
    issued = 0
    for n, piece in enumerate(branch):
        if piece in ret_pieces and issued < ret_needs:
            for m in matmuls[issued:ret_needs]:
                m()
            issued = ret_needs
        piece()
        upto = max(issued, min(len(matmuls), -(-len(matmuls) * (n + 1) // len(branch))))
        for m in matmuls[issued:upto]:
            m()
        issued = upto
    for m in matmuls[issued:]:
        m()


def _project(x2, w_layers, layer, b, cos_t, sin_t, conv_w, conv_b, wa_bd, ba, wx_bd, bx, lam, batch, seq):
    t, d = x2.shape
    nblk = seq // PROJ_TM
    row_map = lambda bi, n: (bi * nblk + n, 0)
    whole = functools.partial(pl.BlockSpec, pipeline_mode=pl.Buffered(1))
    tile = lambda width: pl.BlockSpec((PROJ_TM, width), row_map)
    vec = pl.BlockSpec((1, BRANCH_W), lambda bi, n: (0, 0))
    mat = pl.BlockSpec((BRANCH_W // LRU_LANES, LRU_LANES, LRU_LANES), lambda bi, n: (0, 0, 0))
    tile_f32 = pltpu.VMEM((PROJ_TM, BRANCH_W), F32)
    return pl.pallas_call(
        _proj_kernel,
        grid=(batch, nblk),
        in_specs=[
            tile(d),
            whole((None, d, W_COLS), lambda bi, n: (layer, 0, 0)),
            pl.BlockSpec((1, W_COLS), lambda bi, n: (0, 0)),
            pl.BlockSpec((PROJ_TM, RET_DK), lambda bi, n: (n, 0)),
            pl.BlockSpec((PROJ_TM, RET_DK), lambda bi, n: (n, 0)),
            pl.BlockSpec((CONV_W, BRANCH_W), lambda bi, n: (0, 0)),
            vec, mat, vec, mat, vec, vec,
        ],
        out_specs=[tile(ZB_COLS), tile(ZF_COLS), tile(BRANCH_W), tile(BRANCH_W)],
        out_shape=[jax.ShapeDtypeStruct((t, ZB_COLS), BF16), jax.ShapeDtypeStruct((t, ZF_COLS), F32),
                   jax.ShapeDtypeStruct((t, BRANCH_W), BF16), jax.ShapeDtypeStruct((t, BRANCH_W), BF16)],
        scratch_shapes=[tile_f32] * 3 + [pltpu.VMEM((PROJ_TM, BRANCH_W), BF16)]
        + [pltpu.VMEM((RET_HEADS, RET_DK, RET_DK), F32)] * 4
        + [tile_f32, tile_f32, pltpu.VMEM((8 * (CONV_W - 1), BRANCH_W), F32), pltpu.VMEM((8, BRANCH_W), F32)],
        compiler_params=_params("arbitrary", "arbitrary"),
        name="in_proj_retention_lru",
    )(x2, w_layers, b, cos_t, sin_t, conv_w, conv_b, wa_bd, ba, wx_bd, bx, lam)


def _lru_piece(r0, hf, zlx_ref, cw_ref, cb_ref, wa_ref, ba_ref, wx_ref, bx_ref, lam_ref, gate_ref,
               tail_ref, h_ref, o_ref):
    rows = LRU_ROWS
    per = rows // 8
    half = LRU_LANES
    lanes = slice(hf * half, (hf + 1) * half)
    xp = pltpu.einshape("(sj)c->(js)c", zlx_ref[r0:r0 + rows, lanes], s=8)
    sub = lax.broadcasted_iota(jnp.int32, (8, half), 0)

    def delayed(k):
        heads = []
        for m in range(per - k, per):
            prev = tail_ref[8 * (m - (per - 3)):8 * (m - (per - 3)) + 8, lanes]
            heads.append(pltpu.roll(jnp.where(sub == 7, prev, xp[8 * m:8 * m + 8, :]), 1, 0))
        return jnp.concatenate(heads + [xp[0:8 * (per - k), :]], axis=0)

    xc = cb_ref[:, lanes] + cw_ref[CONV_W - 1:CONV_W, lanes] * xp
    for k in range(1, CONV_W):
        xc = xc + cw_ref[CONV_W - 1 - k:CONV_W - k, lanes] * delayed(k)
    tail_ref[:, lanes] = xp[8 * (per - 3):8 * per, :]

    xcb = xc.astype(BF16)
    r = jax.nn.sigmoid(_dot(xcb, wa_ref[hf]) + ba_ref[:, lanes])
    i = jax.nn.sigmoid(_dot(xcb, wx_ref[hf]) + bx_ref[:, lanes])
    nlam = -lam_ref[:, lanes]
    softplus = jnp.maximum(nlam, 0.0) + jnp.log1p(jnp.exp(-jnp.abs(nlam)))
    log_a = -LRU_C * r * softplus
    a_all = jnp.exp(log_a)
    u_all = jnp.sqrt(jnp.tanh(-log_a) * (1.0 + a_all * a_all)) * (i * xc)

    sub1 = sub[:, 0:128]
    slabs = []
    for s in range(half // 128):
        sl = slice(s * 128, (s + 1) * 128)
        cols = slice(hf * half + s * 128, hf * half + (s + 1) * 128)
        run_h, run_p = [u_all[0:8, sl]], [a_all[0:8, sl]]
        for j in range(1, per):
            aj = a_all[8 * j:8 * j + 8, sl]
            run_h.append(aj * run_h[-1] + u_all[8 * j:8 * j + 8, sl])
            run_p.append(aj * run_p[-1])
        tot_h, tot_p = run_h[-1], run_p[-1]
        for shift in (1, 2, 4):
            keep = sub1 >= shift
            p_sh = jnp.where(keep, pltpu.roll(tot_p, shift, 0), 1.0)
            h_sh = jnp.where(keep, pltpu.roll(tot_h, shift, 0), 0.0)
            tot_h = tot_p * h_sh + tot_h
            tot_p = tot_p * p_sh
        hb = h_ref[:, cols]
        ends = tot_h + tot_p * hb
        start = jnp.where(sub1 == 0, hb, pltpu.roll(ends, 1, 0))
        h_ref[:, cols] = jnp.broadcast_to(ends[7:8, :], (8, 128))
        slabs.append(jnp.concatenate([run_h[j] + run_p[j] * start for j in range(per)], axis=0))
    h = pltpu.einshape("(js)c->(sj)c", jnp.concatenate(slabs, axis=1), s=8)
    o_ref[r0:r0 + rows, lanes] = (h * _gelu(gate_ref[r0:r0 + rows, lanes])).astype(o_ref.dtype)


def _sb_key_blocks(q_ref, qrows, k_ref, v_ref, acc_ref, carry_ref, tri, offs, diag_mask):
    t = SB_TQ
    heads = range(SB_HEADS)
    blocks = range(len(offs))
    cols = [slice(h * SB_HD, (h + 1) * SB_HD) for h in heads]
    zs = [[_dot_nt(q_ref[qrows, cols[h]], k_ref[pl.ds(offs[b], t), cols[h]]) for b in blocks] for h in heads]
    cums = [[None] * len(offs) for _ in heads]
    for h in heads:
        for b in blocks:
            z = zs[h][b]
            lg = jnp.log(1.0 + jnp.exp2(jnp.abs(z) * (-LOG2E)))
            log_1m = jnp.minimum(-z, 0.0) - lg
            if b == 0 and diag_mask is not None:
                log_1m = jnp.where(diag_mask, log_1m, 0.0)
            hi = log_1m.astype(BF16)
            lo = (log_1m - hi.astype(F32)).astype(BF16)
            cums[h][b] = _dot(hi, tri) + _dot(lo, tri)
    for h in heads:
        carry = None if diag_mask is not None else carry_ref[h]
        pv = None
        for b in blocks:
            e = zs[h][b] + cums[h][b]
            if carry is not None:
                e = e + jnp.concatenate([carry] * (t // 128), axis=1)
            w = jnp.exp(e)
            if b == 0 and diag_mask is not None:
                w = jnp.where(diag_mask, w, 0.0)
            d = _dot(w.astype(BF16), v_ref[pl.ds(offs[b], t), cols[h]])
            pv = d if pv is None else pv + d
            block_sum = cums[h][b][:, 0:1]
            carry = jnp.broadcast_to(block_sum, (t, 128)) if carry is None else carry + block_sum
        carry_ref[h] = carry
        if diag_mask is not None:
            acc_ref[:, cols[h]] = pv
        else:
            acc_ref[:, cols[h]] += pv


def _sb_kernel(q_ref, k_ref, v_ref, o_ref, acc_ref, carry_ref):
    t = SB_TQ
    row = lax.broadcasted_iota(jnp.int32, (t, t), 0)
    col = lax.broadcasted_iota(jnp.int32, (t, t), 1)
    tri = (row >= col).astype(BF16)

    def any_weight_left():
        m = carry_ref[0]
        for h in range(1, SB_HEADS):
            m = jnp.maximum(m, carry_ref[h])
        return jnp.max(m) > SB_EXP_ZERO

    def query_block(j, _):
        i = pl.program_id(1) * SB_STEP_BLOCKS + j
        qrows = pl.ds(pl.multiple_of(j * t, t), t)
        diag = pl.multiple_of(i * t, t)

        @pl.when(i == 0)
        def _():
            _sb_key_blocks(q_ref, qrows, k_ref, v_ref, acc_ref, carry_ref, tri, [diag], col < row)

        @pl.when(i > 0)
        def _():
            prev = pl.multiple_of((i - 1) * t, t)
            _sb_key_blocks(q_ref, qrows, k_ref, v_ref, acc_ref, carry_ref, tri, [diag, prev], col < row)

        def cond(state):
            step, live = state
            return jnp.logical_and(step < i, live)

        def body(state):
            step, _ = state
            off = pl.multiple_of((i - 1 - step) * t, t)
            _sb_key_blocks(q_ref, qrows, k_ref, v_ref, acc_ref, carry_ref, tri, [off], None)
            return step + 1, any_weight_left()

        lax.while_loop(cond, body, (jnp.int32(1), any_weight_left()))
        o_ref[qrows, :] = acc_ref[...].astype(o_ref.dtype)
        return 0

    lax.fori_loop(0, SB_STEP_BLOCKS, query_block, 0)


def _stick_breaking(zb, batch, seq):
    rows = SB_TQ * SB_STEP_BLOCKS
    zb3 = zb.reshape(batch, seq, ZB_COLS)
    return pl.pallas_call(
        _sb_kernel,
        grid=(batch, seq // rows),
        in_specs=[
            pl.BlockSpec((None, rows, BRANCH_W), lambda b, i: (b, i, ZB_SB_Q)),
            pl.BlockSpec((None, seq, BRANCH_W), lambda b, i: (b, 0, ZB_SB_K)),
            pl.BlockSpec((None, seq, BRANCH_W), lambda b, i: (b, 0, ZB_SB_V)),
        ],
        out_specs=pl.BlockSpec((None, rows, BRANCH_W), lambda b, i: (b, i, 0)),
        out_shape=jax.ShapeDtypeStruct((batch, seq, BRANCH_W), BF16),
        scratch_shapes=[pltpu.VMEM((SB_TQ, BRANCH_W), F32), pltpu.VMEM((SB_HEADS, SB_TQ, 128), F32)],
        compiler_params=_params("arbitrary", "arbitrary"),
        name="stick_breaking",
    )(zb3, zb3, zb3).reshape(batch * seq, BRANCH_W)


def _sg_chunk(rows, u_ref, v_ref, g_ref, b_ref, ws_ref, bs_ref, o_ref):
    u = _gelu(u_ref[rows, :])
    v = _layer_norm(_gelu(v_ref[rows, :]), g_ref[...], b_ref[...]).astype(BF16)
    row = lax.broadcasted_iota(jnp.int32, (BLOCK, BLOCK), 0)
    col = lax.broadcasted_iota(jnp.int32, (BLOCK, BLOCK), 1)
    gd = BRANCH_W // SG_GROUPS
    for g in range(SG_GROUPS):
        w = jnp.where(col <= row, ws_ref[g], 0.0).astype(BF16)
        cols = slice(g * gd, (g + 1) * gd)
        s = _dot(w, v[:, cols]) + bs_ref[g]
        o_ref[rows, cols] = (u[:, cols] * s).astype(o_ref.dtype)


def _merge_kernel(ya_ref, yb_ref, yc_ref, u_ref, v_ref, x_ref, sg_g_ref, sg_b_ref, ws_ref, bs_ref,
                  wg_ref, bg_ref, wb_ref, wo_ref, lg_ref, lb_ref, o_ref, yd_ref):
    xb = x_ref[...].astype(BF16)
    nchunk = MERGE_TM // BLOCK
    sg_after = {0: range(0, nchunk // 2), 1: range(nchunk // 2, nchunk - 1), 2: range(nchunk - 1, nchunk)}
    merged = None
    for n, y_ref in enumerate((ya_ref, yb_ref, yc_ref, yd_ref)):
        cols = slice(n * D_MODEL, (n + 1) * D_MODEL)
        logits = _dot(xb, wg_ref[0, :, cols]) + bg_ref[:, cols]
        for c in sg_after.get(n, ()):
            _sg_chunk(slice(c * BLOCK, (c + 1) * BLOCK), u_ref, v_ref, sg_g_ref, sg_b_ref, ws_ref, bs_ref,
                      yd_ref)
        term = jax.nn.sigmoid(logits) * _dot(y_ref[...], wb_ref[n])
        merged = term if merged is None else merged + term
    mb = merged.astype(BF16)
    nsub = 4
    sub = MERGE_TM // nsub
    outs = [_dot(mb[r * sub:(r + 1) * sub, :], wo_ref[...]) for r in range(nsub)]
    for r in range(nsub):
        rows = slice(r * sub, (r + 1) * sub)
        o_ref[rows, :] = _layer_norm(ALPHA * x_ref[rows, :] + outs[r], lg_ref[...], lb_ref[...])


def _merge(ys, zf, x2, sg_ln_g, sg_ln_b, ws, bs_b, layer, w_gate, b_gate, w_branch, w_out, ln_g, ln_b):
    t = x2.shape[0]
    ysp = pl.BlockSpec((MERGE_TM, BRANCH_W), lambda i: (i, 0))
    vec = pl.BlockSpec((1, D_MODEL), lambda i: (0, 0))
    hvec = pl.BlockSpec((1, BRANCH_W), lambda i: (0, 0))
    cube = pl.BlockSpec((SG_GROUPS, BLOCK, BLOCK), lambda i: (0, 0, 0))
    whole = functools.partial(pl.BlockSpec, pipeline_mode=pl.Buffered(1))
    return pl.pallas_call(
        _merge_kernel,
        grid=(t // MERGE_TM,),
        in_specs=[ysp] * 3 + [
            pl.BlockSpec((MERGE_TM, BRANCH_W), lambda i: (i, ZF_SG_U)),
            pl.BlockSpec((MERGE_TM, BRANCH_W), lambda i: (i, ZF_SG_V)),
            pl.BlockSpec((MERGE_TM, D_MODEL), lambda i: (i, 0)),
            hvec, hvec, cube, cube,
            whole((pl.Element(1), pl.Element(D_MODEL), pl.Element(N_BRANCH * D_MODEL)),
                  lambda i: (layer, 0, W_COLS)),
            pl.BlockSpec((1, N_BRANCH * D_MODEL), lambda i: (0, 0)),
            whole((None, N_BRANCH, BRANCH_W, D_MODEL), lambda i: (layer, 0, 0, 0)),
            whole((None, D_MODEL, D_MODEL), lambda i: (layer, 0, 0)),
            vec, vec,
        ],
        out_specs=pl.BlockSpec((MERGE_TM, D_MODEL), lambda i: (i, 0)),
        out_shape=jax.ShapeDtypeStruct((t, D_MODEL), F32),
        scratch_shapes=[pltpu.VMEM((MERGE_TM, BRANCH_W), BF16)],
        compiler_params=_params("arbitrary"),
        name="sg_merge_out",
    )(*ys, zf, zf, x2, sg_ln_g, sg_ln_b, ws, bs_b, w_gate, b_gate, w_branch, w_out, ln_g, ln_b)


def _mlp_kernel(x_ref, w1_ref, b1_ref, w2_ref, b2_ref, lg_ref, lb_ref, o_ref):
    def finish(rows, acc):
        o_ref[rows, :] = _layer_norm(ALPHA * x_ref[rows, :] + (acc + b2_ref[...]), lg_ref[...], lb_ref[...])

    done = []
    for r in range(MLP_TM // MLP_SUB):
        rows = slice(r * MLP_SUB, (r + 1) * MLP_SUB)
        xb = x_ref[rows, :].astype(BF16)
        acc = None
        for c in range(D_FF // MLP_FC):
            cols = slice(c * MLP_FC, (c + 1) * MLP_FC)
            hid = jnp.maximum(_dot(xb, w1_ref[:, cols]) + b1_ref[:, cols], 0.0)
            part = _dot((hid * hid).astype(BF16), w2_ref[cols, :])
            acc = part if acc is None else acc + part
        done.append((rows, acc))
    for rows, acc in done:
        finish(rows, acc)


def _mlp(x2, layer, w1, b1, w2, b2, ln_g, ln_b):
    t = x2.shape[0]
    vec = pl.BlockSpec((1, D_MODEL), lambda i: (0, 0))
    return pl.pallas_call(
        _mlp_kernel,
        grid=(t // MLP_TM,),
        in_specs=[
            pl.BlockSpec((MLP_TM, D_MODEL), lambda i: (i, 0)),
            pl.BlockSpec((None, D_MODEL, D_FF), lambda i: (layer, 0, 0), pipeline_mode=pl.Buffered(1)),
            pl.BlockSpec((1, D_FF), lambda i: (0, 0)),
            pl.BlockSpec((None, D_FF, D_MODEL), lambda i: (layer, 0, 0), pipeline_mode=pl.Buffered(1)),
            vec, vec, vec,
        ],
        out_specs=pl.BlockSpec((MLP_TM, D_MODEL), lambda i: (i, 0)),
        out_shape=jax.ShapeDtypeStruct((t, D_MODEL), F32),
        compiler_params=_params("arbitrary"),
        name="mlp",
    )(x2, w1, b1, w2, b2, ln_g, ln_b)


def _block_diag(w):
    per = LRU_LANES // LRU_HD
    groups = LRU_HEADS // per
    w4 = w.reshape(groups, per, LRU_HD, LRU_HD)
    eye = jnp.eye(per, dtype=w.dtype)
    return jnp.einsum('ghij,hk->ghikj', w4, eye).reshape(groups, LRU_LANES, LRU_LANES)


def _rope_tables(seq):
    half = RET_DK // 2
    pos = np.arange(seq, dtype=np.float32)
    inv_freq = (np.float32(ROPE_THETA) ** (-np.arange(half, dtype=np.float32) / np.float32(half))).astype(np.float32)
    ang = pos[:, None] * inv_freq[None, :]
    cos, sin = np.cos(ang), np.sin(ang)
    return (jnp.asarray(np.concatenate([cos, cos], axis=-1), F32),
            jnp.asarray(np.concatenate([-sin, sin], axis=-1), F32))


def _layer(x2, batch, seq, cos_t, sin_t, layer, big, b_in, conv_w, conv_b, lru_wa, lru_ba, lru_wx,
           lru_bx, lru_lambda, sg_ln_g, sg_ln_b, sg_ws, sg_bs, ln1_g, ln1_b, b1, b2, ln2_g, ln2_b):
    w_proj, w_gate, w_branch, w_out, w1, w2 = big
    gl0 = W_COLS
    zb, zf, y_a, y_b = _project(
        x2, w_proj, layer, b_in[None, :gl0], cos_t, sin_t, conv_w, conv_b[None, :],
        _block_diag(lru_wa).astype(BF16), lru_ba[None, :], _block_diag(lru_wx).astype(BF16),
        lru_bx[None, :], lru_lambda[None, :], batch, seq)
    y_c = _stick_breaking(zb, batch, seq)
    bs_b = jnp.broadcast_to(sg_bs[:, :, None], (SG_GROUPS, BLOCK, BRANCH_W // SG_GROUPS))

    x1 = _merge((y_a, y_b, y_c), zf, x2, sg_ln_g[None, :], sg_ln_b[None, :], sg_ws, bs_b, layer,
                w_gate, b_in[None, gl0:], w_branch, w_out, ln1_g[None, :], ln1_b[None, :])
    return _mlp(x1, layer, w1, b1[None, :], w2, b2[None, :], ln2_g[None, :], ln2_b[None, :])


def kernel(x, w_in, b_in, conv_w, conv_b, lru_wa, lru_ba, lru_wx, lru_bx, lru_lambda, sg_ln_g, sg_ln_b, sg_ws, sg_bs, w_branch, w_out, ln1_g, ln1_b, w1, b1, w2, b2, ln2_g, ln2_b):
    batch, seq, d = x.shape
    cos_t, sin_t = _rope_tables(seq)
    x2 = x.reshape(batch * seq, d)
    w_in_b = w_in.astype(BF16)
    big = (w_in_b, w_in_b, w_branch.astype(BF16), w_out.astype(BF16), w1.astype(BF16), w2.astype(BF16))
    per_layer = (b_in, conv_w, conv_b, lru_wa, lru_ba, lru_wx, lru_bx, lru_lambda, sg_ln_g,
                 sg_ln_b, sg_ws, sg_bs, ln1_g, ln1_b, b1, b2, ln2_g, ln2_b)
    for l in range(DEPTH):
        x2 = _layer(x2, batch, seq, cos_t, sin_t, l, big, *(p[l] for p in per_layer))
    return x2.reshape(batch, seq, d)
```

```python
import functools
import math

import jax
import jax.numpy as jnp
import numpy as np
from jax import lax
from jax.experimental import pallas as pl
from jax.experimental.pallas import tpu as pltpu

F32 = jnp.float32
BF16 = jnp.bfloat16

D_MODEL = 1024
DEPTH = 2
BLOCK = 128
BRANCH_W = D_MODEL // 2
N_BRANCH = 4
RET_HEADS = 4
RET_DK = BRANCH_W // RET_HEADS
ROPE_THETA = 10000.0
LRU_HEADS = 8
LRU_HD = BRANCH_W // LRU_HEADS
CONV_W = 4
LRU_C = 8.0
SB_HEADS = 4
SB_HD = BRANCH_W // SB_HEADS
SG_GROUPS = 4
D_FF = 4 * D_MODEL
ALPHA = (2 * DEPTH) ** 0.25
LN_EPS = 1e-5
LOG2E = 1.4426950408889634
SB_EXP_ZERO = -105.0

W_RET_Q, W_RET_K, W_RET_V, W_RET_G, W_LRU_X, W_LRU_G, W_SB_Q, W_SB_K, W_SB_V, W_SG_U, W_SG_V = range(11)
W_COLS = 11 * BRANCH_W
ZF_SG_U, ZF_SG_V = 0, 1
ZF_COLS = 2 * BRANCH_W
ZB_SB_Q, ZB_SB_K, ZB_SB_V = 0, 1, 2
ZB_COLS = 3 * BRANCH_W

VMEM_LIMIT = 60 * 1024 * 1024

PROJ_TM = 1024
LRU_ROWS = 128
LRU_LANES = 256
SB_TQ = 256
SB_STEP_BLOCKS = 2
MERGE_TM = 1024
MLP_TM = 1024
MLP_SUB = 512
MLP_FC = 2048


def _params(*sem):
    return pltpu.CompilerParams(dimension_semantics=sem, vmem_limit_bytes=VMEM_LIMIT)


def _dot(a, b):
    return jnp.dot(a, b, preferred_element_type=F32)


def _dot_nt(a, b):
    return lax.dot_general(a, b, (((1,), (1,)), ((), ())), preferred_element_type=F32)


def _gelu(x):
    k1 = -2.0 * math.sqrt(2.0 / math.pi) * LOG2E
    return x / (1.0 + jnp.exp2(x * (k1 + (k1 * 0.044715) * (x * x))))


def _layer_norm(h, g, b):
    mu = jnp.mean(h, axis=-1, keepdims=True)
    d = h - mu
    var = jnp.mean(d * d, axis=-1, keepdims=True)
    return d * lax.rsqrt(var + LN_EPS) * g + b


def _ret_log_g(h):
    return math.log1p(-(2.0 ** (-5.0 - h)))


def _ret_chunk(rows, zq_ref, zk_ref, zg_ref, zv_ref, cos_ref, sin_ref, ya_ref,
               state_ref, dmat_ref, qdec_ref, kdec_ref):
    heads = range(RET_HEADS)
    cols = [slice(h * RET_DK, (h + 1) * RET_DK) for h in heads]
    cs = cos_ref[rows, :]
    sn = sin_ref[rows, :]
    qr, kr = [], []
    for h in heads:
        q = zq_ref[rows, cols[h]]
        k = zk_ref[rows, cols[h]]
        qr.append(q * cs + pltpu.roll(q, RET_DK // 2, 1) * sn)
        kr.append((k * cs + pltpu.roll(k, RET_DK // 2, 1) * sn) * (RET_DK ** -0.5))
    scores = [_dot_nt(qr[h].astype(BF16), kr[h].astype(BF16)) * dmat_ref[h] for h in heads]
    cross = [_dot((qr[h] * qdec_ref[h]).astype(BF16), state_ref[h].astype(BF16)) for h in heads]
    kd_t = [jnp.transpose(kr[h] * kdec_ref[h]).astype(BF16) for h in heads]
    for h in heads:
        v = zv_ref[rows, cols[h]]
        y = _dot(scores[h].astype(BF16), v) + cross[h]
        state_ref[h] = math.exp(_ret_log_g(h) * BLOCK) * state_ref[h] + _dot(kd_t[h], v)
        mu = jnp.mean(y, axis=-1, keepdims=True)
        d = y - mu
        var = jnp.mean(d * d, axis=-1, keepdims=True)
        yn = d * lax.rsqrt(var + LN_EPS)
        ya_ref[rows, cols[h]] = (jax.nn.silu(zg_ref[rows, cols[h]]) * yn).astype(ya_ref.dtype)


def _proj_kernel(x_ref, w_ref, b_ref, cos_ref, sin_ref, cw_ref, cb_ref, wa_ref, ba_ref, wx_ref, bx_ref,
                 lam_ref, zb_ref, zf_ref, ya_ref, yb_ref,
                 zq_ref, zk_ref, zg_ref, zv_ref, state_ref, dmat_ref, qdec_ref, kdec_ref,
                 zlx_ref, zlg_ref, tail_ref, h_ref):
    @pl.when(pl.program_id(1) == 0)
    def _():
        tail_ref[...] = jnp.zeros_like(tail_ref)
        h_ref[...] = jnp.zeros_like(h_ref)
        state_ref[...] = jnp.zeros_like(state_ref)
        row = lax.broadcasted_iota(jnp.int32, (BLOCK, BLOCK), 0).astype(F32)
        col = lax.broadcasted_iota(jnp.int32, (BLOCK, BLOCK), 1).astype(F32)
        diff = row - col
        for h in range(RET_HEADS):
            lg = _ret_log_g(h)
            dmat_ref[h] = jnp.where(diff >= 0, jnp.exp(lg * jnp.maximum(diff, 0.0)), 0.0)
            qdec_ref[h] = jnp.exp(lg * (row + 1.0))
            kdec_ref[h] = jnp.exp(lg * (BLOCK - 1.0 - row))

    xb = x_ref[...].astype(BF16)

    half = BRANCH_W // 2

    def project(part, dst_ref, row0, col0, scale=None):
        def piece(hf):
            cols = slice(part * BRANCH_W + hf * half, part * BRANCH_W + (hf + 1) * half)
            acc = _dot(xb, w_ref[:, cols]) + b_ref[:, cols]
            if scale is not None:
                acc = acc * scale
            dst_ref[row0:row0 + PROJ_TM, col0 + hf * half:col0 + (hf + 1) * half] = acc.astype(dst_ref.dtype)
        return [functools.partial(piece, 0), functools.partial(piece, 1)]

    def ret(c):
        _ret_chunk(slice(c * BLOCK, (c + 1) * BLOCK), zq_ref, zk_ref, zg_ref, zv_ref, cos_ref, sin_ref,
                   ya_ref, state_ref, dmat_ref, qdec_ref, kdec_ref)

    def lru(blk, hf):
        _lru_piece(blk * LRU_ROWS, hf, zlx_ref, cw_ref, cb_ref, wa_ref, ba_ref, wx_ref, bx_ref, lam_ref,
                   zlg_ref, tail_ref, h_ref, yb_ref)

    for piece in project(W_LRU_X, zlx_ref, 0, 0) + project(W_LRU_G, zlg_ref, 0, 0):
        piece()
    matmuls = (project(W_RET_Q, zq_ref, 0, 0) + project(W_RET_K, zk_ref, 0, 0)
               + project(W_RET_G, zg_ref, 0, 0) + project(W_RET_V, zv_ref, 0, 0)
               + project(W_SG_U, zf_ref, 0, ZF_SG_U * BRANCH_W) + project(W_SG_V, zf_ref, 0, ZF_SG_V * BRANCH_W)
               + project(W_SB_Q, zb_ref, 0, ZB_SB_Q * BRANCH_W, SB_HD ** -0.5)
               + project(W_SB_K, zb_ref, 0, ZB_SB_K * BRANCH_W) + project(W_SB_V, zb_ref, 0, ZB_SB_V * BRANCH_W))
    nblk = PROJ_TM // LRU_ROWS
    lru_pieces = [functools.partial(lru, blk, hf) for blk in range(nblk)
                  for hf in range(BRANCH_W // LRU_LANES)]
    ret_pieces = [functools.partial(ret, c) for c in range(PROJ_TM // BLOCK)]
    late = len(ret_pieces)
    branch = lru_pieces[:-late]
    for c, piece in enumerate(ret_pieces):
        branch += [piece] + lru_pieces[len(lru_pieces) - late + c:len(lru_pieces) - late + c + 1]
    ret_needs = 8
    issued = 0
    for n, piece in enumerate(branch):
        if piece in ret_pieces and issued < ret_needs:
            for m in matmuls[issued:ret_needs]:
                m()
            issued = ret_needs
        piece()
        upto = max(issued, min(len(matmuls), -(-len(matmuls) * (n + 1) // len(branch))))
        for m in matmuls[issued:upto]:
            m()
        issued = upto
    for m in matmuls[issued:]:
        m()


def _project(x2, w_layers, layer, b, cos_t, sin_t, conv_w, conv_b, wa_bd, ba, wx_bd, bx, lam, batch, seq):
    t, d = x2.shape
    nblk = seq // PROJ_TM
    row_map = lambda bi, n: (bi * nblk + n, 0)
    whole = functools.partial(pl.BlockSpec, pipeline_mode=pl.Buffered(1))
    tile = lambda width: pl.BlockSpec((PROJ_TM, width), row_map)
    vec = pl.BlockSpec((1, BRANCH_W), lambda bi, n: (0, 0))
    mat = pl.BlockSpec((BRANCH_W // LRU_LANES, LRU_LANES, LRU_LANES), lambda bi, n: (0, 0, 0))
    tile_f32 = pltpu.VMEM((PROJ_TM, BRANCH_W), F32)
    return pl.pallas_call(
        _proj_kernel,
        grid=(batch, nblk),
        in_specs=[
            tile(d),
            whole((None, d, W_COLS), lambda bi, n: (layer, 0, 0)),
            pl.BlockSpec((1, W_COLS), lambda bi, n: (0, 0)),
            pl.BlockSpec((PROJ_TM, RET_DK), lambda bi, n: (n, 0)),
            pl.BlockSpec((PROJ_TM, RET_DK), lambda bi, n: (n, 0)),
            pl.BlockSpec((CONV_W, BRANCH_W), lambda bi, n: (0, 0)),
            vec, mat, vec, mat, vec, vec,
        ],
        out_specs=[tile(ZB_COLS), tile(ZF_COLS), tile(BRANCH_W), tile(BRANCH_W)],
        out_shape=[jax.ShapeDtypeStruct((t, ZB_COLS), BF16), jax.ShapeDtypeStruct((t, ZF_COLS), F32),
                   jax.ShapeDtypeStruct((t, BRANCH_W), BF16), jax.ShapeDtypeStruct((t, BRANCH_W), BF16)],
        scratch_shapes=[tile_f32] * 3 + [pltpu.VMEM((PROJ_TM, BRANCH_W), BF16)]
        + [pltpu.VMEM((RET_HEADS, RET_DK, RET_DK), F32)] * 4
        + [tile_f32, tile_f32, pltpu.VMEM((8 * (CONV_W - 1), BRANCH_W), F32), pltpu.VMEM((8, BRANCH_W), F32)],
        compiler_params=_params("arbitrary", "arbitrary"),
        name="in_proj_retention_lru",
    )(x2, w_layers, b, cos_t, sin_t, conv_w, conv_b, wa_bd, ba, wx_bd, bx, lam)


def _lru_piece(r0, hf, zlx_ref, cw_ref, cb_ref, wa_ref, ba_ref, wx_ref, bx_ref, lam_ref, gate_ref,
               tail_ref, h_ref, o_ref):
    rows = LRU_ROWS
    per = rows // 8
    half = LRU_LANES
    lanes = slice(hf * half, (hf + 1) * half)
    xp = pltpu.einshape("(sj)c->(js)c", zlx_ref[r0:r0 + rows, lanes], s=8)
    sub = lax.broadcasted_iota(jnp.int32, (8, half), 0)

    def delayed(k):
        heads = []
        for m in range(per - k, per):
            prev = tail_ref[8 * (m - (per - 3)):8 * (m - (per - 3)) + 8, lanes]
            heads.append(pltpu.roll(jnp.where(sub == 7, prev, xp[8 * m:8 * m + 8, :]), 1, 0))
        return jnp.concatenate(heads + [xp[0:8 * (per - k), :]], axis=0)

    xc = cb_ref[:, lanes] + cw_ref[CONV_W - 1:CONV_W, lanes] * xp
    for k in range(1, CONV_W):
        xc = xc + cw_ref[CONV_W - 1 - k:CONV_W - k, lanes] * delayed(k)
    tail_ref[:, lanes] = xp[8 * (per - 3):8 * per, :]

    xcb = xc.astype(BF16)
    r = jax.nn.sigmoid(_dot(xcb, wa_ref[hf]) + ba_ref[:, lanes])
    i = jax.nn.sigmoid(_dot(xcb, wx_ref[hf]) + bx_ref[:, lanes])
    nlam = -lam_ref[:, lanes]
    softplus = jnp.maximum(nlam, 0.0) + jnp.log1p(jnp.exp(-jnp.abs(nlam)))
    log_a = -LRU_C * r * softplus
    a_all = jnp.exp(log_a)
    u_all = jnp.sqrt(jnp.tanh(-log_a) * (1.0 + a_all * a_all)) * (i * xc)

    sub1 = sub[:, 0:128]
    slabs = []
    for s in range(half // 128):
        sl = slice(s * 128, (s + 1) * 128)
        cols = slice(hf * half + s * 128, hf * half + (s + 1) * 128)
        run_h, run_p = [u_all[0:8, sl]], [a_all[0:8, sl]]
        for j in range(1, per):
            aj = a_all[8 * j:8 * j + 8, sl]
            run_h.append(aj * run_h[-1] + u_all[8 * j:8 * j + 8, sl])
            run_p.append(aj * run_p[-1])
        tot_h, tot_p = run_h[-1], run_p[-1]
        for shift in (1, 2, 4):
            keep = sub1 >= shift
            p_sh = jnp.where(keep, pltpu.roll(tot_p, shift, 0), 1.0)
            h_sh = jnp.where(keep, pltpu.roll(tot_h, shift, 0), 0.0)
            tot_h = tot_p * h_sh + tot_h
            tot_p = tot_p * p_sh
        hb = h_ref[:, cols]
        ends = tot_h + tot_p * hb
        start = jnp.where(sub1 == 0, hb, pltpu.roll(ends, 1, 0))
        h_ref[:, cols] = jnp.broadcast_to(ends[7:8, :], (8, 128))
        slabs.append(jnp.concatenate([run_h[j] + run_p[j] * start for j in range(per)], axis=0))
    h = pltpu.einshape("(js)c->(sj)c", jnp.concatenate(slabs, axis=1), s=8)
    o_ref[r0:r0 + rows, lanes] = (h * _gelu(gate_ref[r0:r0 + rows, lanes])).astype(o_ref.dtype)


def _sb_key_blocks(q_ref, qrows, k_ref, v_ref, acc_ref, carry_ref, tri, offs, diag_mask):
    t = SB_TQ
    heads = range(SB_HEADS)
    blocks = range(len(offs))
    cols = [slice(h * SB_HD, (h + 1) * SB_HD) for h in heads]
    zs = [[_dot_nt(q_ref[qrows, cols[h]], k_ref[pl.ds(offs[b], t), cols[h]]) for b in blocks] for h in heads]
    cums = [[None] * len(offs) for _ in heads]
    for h in heads:
        for b in blocks:
            z = zs[h][b]
            lg = jnp.log(1.0 + jnp.exp2(jnp.abs(z) * (-LOG2E)))
            log_1m = jnp.minimum(-z, 0.0) - lg
            if b == 0 and diag_mask is not None:
                log_1m = jnp.where(diag_mask, log_1m, 0.0)
            hi = log_1m.astype(BF16)
            lo = (log_1m - hi.astype(F32)).astype(BF16)
            cums[h][b] = _dot(hi, tri) + _dot(lo, tri)
    for h in heads:
        carry = None if diag_mask is not None else carry_ref[h]
        pv = None
        for b in blocks:
            e = zs[h][b] + cums[h][b]
            if carry is not None:
                e = e + jnp.concatenate([carry] * (t // 128), axis=1)
            w = jnp.exp(e)
            if b == 0 and diag_mask is not None:
                w = jnp.where(diag_mask, w, 0.0)
            d = _dot(w.astype(BF16), v_ref[pl.ds(offs[b], t), cols[h]])
            pv = d if pv is None else pv + d
            block_sum = cums[h][b][:, 0:1]
            carry = jnp.broadcast_to(block_sum, (t, 128)) if carry is None else carry + block_sum
        carry_ref[h] = carry
        if diag_mask is not None:
            acc_ref[:, cols[h]] = pv
        else:
            acc_ref[:, cols[h]] += pv


def _sb_kernel(q_ref, k_ref, v_ref, o_ref, acc_ref, carry_ref):
    t = SB_TQ
    row = lax.broadcasted_iota(jnp.int32, (t, t), 0)
    col = lax.broadcasted_iota(jnp.int32, (t, t), 1)
    tri = (row >= col).astype(BF16)

    def any_weight_left():
        m = carry_ref[0]
        for h in range(1, SB_HEADS):
            m = jnp.maximum(m, carry_ref[h])
        return jnp.max(m) > SB_EXP_ZERO

    def query_block(j, _):
        i = pl.program_id(1) * SB_STEP_BLOCKS + j
        qrows = pl.ds(pl.multiple_of(j * t, t), t)
        diag = pl.multiple_of(i * t, t)

        @pl.when(i == 0)
        def _():
            _sb_key_blocks(q_ref, qrows, k_ref, v_ref, acc_ref, carry_ref, tri, [diag], col < row)

        @pl.when(i > 0)
        def _():
            prev = pl.multiple_of((i - 1) * t, t)
            _sb_key_blocks(q_ref, qrows, k_ref, v_ref, acc_ref, carry_ref, tri, [diag, prev], col < row)

        def cond(state):
            step, live = state
            return jnp.logical_and(step < i, live)

        def body(state):
            step, _ = state
            off = pl.multiple_of((i - 1 - step) * t, t)
            _sb_key_blocks(q_ref, qrows, k_ref, v_ref, acc_ref, carry_ref, tri, [off], None)
            return step + 1, any_weight_left()

        lax.while_loop(cond, body, (jnp.int32(1), any_weight_left()))
        o_ref[qrows, :] = acc_ref[...].astype(o_ref.dtype)
        return 0

    lax.fori_loop(0, SB_STEP_BLOCKS, query_block, 0)


def _stick_breaking(zb, batch, seq):
    rows = SB_TQ * SB_STEP_BLOCKS
    zb3 = zb.reshape(batch, seq, ZB_COLS)
    return pl.pallas_call(
        _sb_kernel,
        grid=(batch, seq // rows),
        in_specs=[
            pl.BlockSpec((None, rows, BRANCH_W), lambda b, i: (b, i, ZB_SB_Q)),
            pl.BlockSpec((None, seq, BRANCH_W), lambda b, i: (b, 0, ZB_SB_K)),
            pl.BlockSpec((None, seq, BRANCH_W), lambda b, i: (b, 0, ZB_SB_V)),
        ],
        out_specs=pl.BlockSpec((None, rows, BRANCH_W), lambda b, i: (b, i, 0)),
        out_shape=jax.ShapeDtypeStruct((batch, seq, BRANCH_W), BF16),
        scratch_shapes=[pltpu.VMEM((SB_TQ, BRANCH_W), F32), pltpu.VMEM((SB_HEADS, SB_TQ, 128), F32)],
        compiler_params=_params("arbitrary", "arbitrary"),
        name="stick_breaking",
    )(zb3, zb3, zb3).reshape(batch * seq, BRANCH_W)


def _sg_chunk(rows, u_ref, v_ref, g_ref, b_ref, ws_ref, bs_ref, o_ref):
    u = _gelu(u_ref[rows, :])
    v = _layer_norm(_gelu(v_ref[rows, :]), g_ref[...], b_ref[...]).astype(BF16)
    row = lax.broadcasted_iota(jnp.int32, (BLOCK, BLOCK), 0)
    col = lax.broadcasted_iota(jnp.int32, (BLOCK, BLOCK), 1)
    gd = BRANCH_W // SG_GROUPS
    for g in range(SG_GROUPS):
        w = jnp.where(col <= row, ws_ref[g], 0.0).astype(BF16)
        cols = slice(g * gd, (g + 1) * gd)
        s = _dot(w, v[:, cols]) + bs_ref[g]
        o_ref[rows, cols] = (u[:, cols] * s).astype(o_ref.dtype)


def _merge_kernel(ya_ref, yb_ref, yc_ref, u_ref, v_ref, x_ref, sg_g_ref, sg_b_ref, ws_ref, bs_ref,
                  wg_ref, bg_ref, wb_ref, wo_ref, lg_ref, lb_ref, o_ref, yd_ref):
    xb = x_ref[...].astype(BF16)
    nchunk = MERGE_TM // BLOCK
    sg_after = {0: range(0, nchunk // 2), 1: range(nchunk // 2, nchunk - 1), 2: range(nchunk - 1, nchunk)}
    merged = None
    for n, y_ref in enumerate((ya_ref, yb_ref, yc_ref, yd_ref)):
        cols = slice(n * D_MODEL, (n + 1) * D_MODEL)
        logits = _dot(xb, wg_ref[0, :, cols]) + bg_ref[:, cols]
        for c in sg_after.get(n, ()):
            _sg_chunk(slice(c * BLOCK, (c + 1) * BLOCK), u_ref, v_ref, sg_g_ref, sg_b_ref, ws_ref, bs_ref,
                      yd_ref)
        term = jax.nn.sigmoid(logits) * _dot(y_ref[...], wb_ref[n])
        merged = term if merged is None else merged + term
    mb = merged.astype(BF16)
    nsub = 4
    sub = MERGE_TM // nsub
    outs = [_dot(mb[r * sub:(r + 1) * sub, :], wo_ref[...]) for r in range(nsub)]
    for r in range(nsub):
        rows = slice(r * sub, (r + 1) * sub)
        o_ref[rows, :] = _layer_norm(ALPHA * x_ref[rows, :] + outs[r], lg_ref[...], lb_ref[...])


def _merge(ys, zf, x2, sg_ln_g, sg_ln_b, ws, bs_b, layer, w_gate, b_gate, w_branch, w_out, ln_g, ln_b):
    t = x2.shape[0]
    ysp = pl.BlockSpec((MERGE_TM, BRANCH_W), lambda i: (i, 0))
    vec = pl.BlockSpec((1, D_MODEL), lambda i: (0, 0))
    hvec = pl.BlockSpec((1, BRANCH_W), lambda i: (0, 0))
    cube = pl.BlockSpec((SG_GROUPS, BLOCK, BLOCK), lambda i: (0, 0, 0))
    whole = functools.partial(pl.BlockSpec, pipeline_mode=pl.Buffered(1))
    return pl.pallas_call(
        _merge_kernel,
        grid=(t // MERGE_TM,),
        in_specs=[ysp] * 3 + [
            pl.BlockSpec((MERGE_TM, BRANCH_W), lambda i: (i, ZF_SG_U)),
            pl.BlockSpec((MERGE_TM, BRANCH_W), lambda i: (i, ZF_SG_V)),
            pl.BlockSpec((MERGE_TM, D_MODEL), lambda i: (i, 0)),
            hvec, hvec, cube, cube,
            whole((pl.Element(1), pl.Element(D_MODEL), pl.Element(N_BRANCH * D_MODEL)),
                  lambda i: (layer, 0, W_COLS)),
            pl.BlockSpec((1, N_BRANCH * D_MODEL), lambda i: (0, 0)),
            whole((None, N_BRANCH, BRANCH_W, D_MODEL), lambda i: (layer, 0, 0, 0)),
            whole((None, D_MODEL, D_MODEL), lambda i: (layer, 0, 0)),
            vec, vec,
        ],
        out_specs=pl.BlockSpec((MERGE_TM, D_MODEL), lambda i: (i, 0)),
        out_shape=jax.ShapeDtypeStruct((t, D_MODEL), F32),
        scratch_shapes=[pltpu.VMEM((MERGE_TM, BRANCH_W), BF16)],
        compiler_params=_params("arbitrary"),
        name="sg_merge_out",
    )(*ys, zf, zf, x2, sg_ln_g, sg_ln_b, ws, bs_b, w_gate, b_gate, w_branch, w_out, ln_g, ln_b)


def _mlp_kernel(x_ref, w1_ref, b1_ref, w2_ref, b2_ref, lg_ref, lb_ref, o_ref):
    def finish(rows, acc):
        o_ref[rows, :] = _layer_norm(ALPHA * x_ref[rows, :] + (acc + b2_ref[...]), lg_ref[...], lb_ref[...])

    done = []
    for r in range(MLP_TM // MLP_SUB):
        rows = slice(r * MLP_SUB, (r + 1) * MLP_SUB)
        xb = x_ref[rows, :].astype(BF16)
        acc = None
        for c in range(D_FF // MLP_FC):
            cols = slice(c * MLP_FC, (c + 1) * MLP_FC)
            hid = jnp.maximum(_dot(xb, w1_ref[:, cols]) + b1_ref[:, cols], 0.0)
            part = _dot((hid * hid).astype(BF16), w2_ref[cols, :])
            acc = part if acc is None else acc + part
        done.append((rows, acc))
    for rows, acc in done:
        finish(rows, acc)


def _mlp(x2, layer, w1, b1, w2, b2, ln_g, ln_b):
    t = x2.shape[0]
    vec = pl.BlockSpec((1, D_MODEL), lambda i: (0, 0))
    return pl.pallas_call(
        _mlp_kernel,
        grid=(t // MLP_TM,),
        in_specs=[
            pl.BlockSpec((MLP_TM, D_MODEL), lambda i: (i, 0)),
            pl.BlockSpec((None, D_MODEL, D_FF), lambda i: (layer, 0, 0), pipeline_mode=pl.Buffered(1)),
            pl.BlockSpec((1, D_FF), lambda i: (0, 0)),
            pl.BlockSpec((None, D_FF, D_MODEL), lambda i: (layer, 0, 0), pipeline_mode=pl.Buffered(1)),
            vec, vec, vec,
        ],
        out_specs=pl.BlockSpec((MLP_TM, D_MODEL), lambda i: (i, 0)),
        out_shape=jax.ShapeDtypeStruct((t, D_MODEL), F32),
        compiler_params=_params("arbitrary"),
        name="mlp",
    )(x2, w1, b1, w2, b2, ln_g, ln_b)


def _block_diag(w):
    per = LRU_LANES // LRU_HD
    groups = LRU_HEADS // per
    w4 = w.reshape(groups, per, LRU_HD, LRU_HD)
    eye = jnp.eye(per, dtype=w.dtype)
    return jnp.einsum('ghij,hk->ghikj', w4, eye).reshape(groups, LRU_LANES, LRU_LANES)


def _rope_tables(seq):
    half = RET_DK // 2
    pos = np.arange(seq, dtype=np.float32)
    inv_freq = (np.float32(ROPE_THETA) ** (-np.arange(half, dtype=np.float32) / np.float32(half))).astype(np.float32)
    ang = pos[:, None] * inv_freq[None, :]
    cos, sin = np.cos(ang), np.sin(ang)
    return (jnp.asarray(np.concatenate([cos, cos], axis=-1), F32),
            jnp.asarray(np.concatenate([-sin, sin], axis=-1), F32))


def _layer(x2, batch, seq, cos_t, sin_t, layer, big, b_in, conv_w, conv_b, lru_wa, lru_ba, lru_wx,
           lru_bx, lru_lambda, sg_ln_g, sg_ln_b, sg_ws, sg_bs, ln1_g, ln1_b, b1, b2, ln2_g, ln2_b):
    w_proj, w_gate, w_branch, w_out, w1, w2 = big
    gl0 = W_COLS
    zb, zf, y_a, y_b = _project(
        x2, w_proj, layer, b_in[None, :gl0], cos_t, sin_t, conv_w, conv_b[None, :],
        _block_diag(lru_wa).astype(BF16), lru_ba[None, :], _block_diag(lru_wx).astype(BF16),
        lru_bx[None, :], lru_lambda[None, :], batch, seq)
    y_c = _stick_breaking(zb, batch, seq)
    bs_b = jnp.broadcast_to(sg_bs[:, :, None], (SG_GROUPS, BLOCK, BRANCH_W // SG_GROUPS))

    x1 = _merge((y_a, y_b, y_c), zf, x2, sg_ln_g[None, :], sg_ln_b[None, :], sg_ws, bs_b, layer,
                w_gate, b_in[None, gl0:], w_branch, w_out, ln1_g[None, :], ln1_b[None, :])
    return _mlp(x1, layer, w1, b1[None, :], w2, b2[None, :], ln2_g[None, :], ln2_b[None, :])


def kernel(x, w_in, b_in, conv_w, conv_b, lru_wa, lru_ba, lru_wx, lru_bx, lru_lambda, sg_ln_g, sg_ln_b, sg_ws, sg_bs, w_branch, w_out, ln1_g, ln1_b, w1, b1, w2, b2, ln2_g, ln2_b):
    batch, seq, d = x.shape
    cos_t, sin_t = _rope_tables(seq)
    x2 = x.reshape(batch * seq, d)
    w_in_b = w_in.astype(BF16)
    big = (w_in_b, w_in_b, w_branch.astype(BF16), w_out.astype(BF16), w1.astype(BF16), w2.astype(BF16))
    per_layer = (b_in, conv_w, conv_b, lru_wa, lru_ba, lru_wx, lru_bx, lru_lambda, sg_ln_g,
                 sg_ln_b, sg_ws, sg_bs, ln1_g, ln1_b, b1, b2, ln2_g, ln2_b)
    for l in range(DEPTH):
        x2 = _layer(x2, batch, seq, cos_t, sin_t, l, big, *(p[l] for p in per_layer))
    return x2.reshape(batch, seq, d)
```

```python
import functools
import math

import jax
import jax.numpy as jnp
import numpy as np
from jax import lax
from jax.experimental import pallas as pl
from jax.experimental.pallas import tpu as pltpu

F32 = jnp.float32
BF16 = jnp.bfloat16

D_MODEL = 1024
DEPTH = 2
BLOCK = 128
BRANCH_W = D_MODEL // 2
N_BRANCH = 4
RET_HEADS = 4
RET_DK = BRANCH_W // RET_HEADS
ROPE_THETA = 10000.0
LRU_HEADS = 8
LRU_HD = BRANCH_W // LRU_HEADS
CONV_W = 4
LRU_C = 8.0
SB_HEADS = 4
SB_HD = BRANCH_W // SB_HEADS
SG_GROUPS = 4
D_FF = 4 * D_MODEL
ALPHA = (2 * DEPTH) ** 0.25
LN_EPS = 1e-5
LOG2E = 1.4426950408889634
SB_EXP_ZERO = -105.0

W_RET_Q, W_RET_K, W_RET_V, W_RET_G, W_LRU_X, W_LRU_G, W_SB_Q, W_SB_K, W_SB_V, W_SG_U, W_SG_V = range(11)
W_COLS = 11 * BRANCH_W
ZF_SG_U, ZF_SG_V = 0, 1
ZF_COLS = 2 * BRANCH_W
ZB_SB_Q, ZB_SB_K, ZB_SB_V = 0, 1, 2
ZB_COLS = 3 * BRANCH_W

VMEM_LIMIT = 60 * 1024 * 1024

PROJ_TM = 1024
LRU_ROWS = 128
LRU_LANES = 256
SB_TQ = 256
SB_STEP_BLOCKS = 4
MERGE_TM = 1024
MLP_TM = 1024
MLP_SUB = 512
MLP_FC = 2048


def _params(*sem):
    return pltpu.CompilerParams(dimension_semantics=sem, vmem_limit_bytes=VMEM_LIMIT)


def _dot(a, b):
    return jnp.dot(a, b, preferred_element_type=F32)


def _dot_nt(a, b):
    return lax.dot_general(a, b, (((1,), (1,)), ((), ())), preferred_element_type=F32)


def _gelu(x):
    k1 = -2.0 * math.sqrt(2.0 / math.pi) * LOG2E
    return x / (1.0 + jnp.exp2(x * (k1 + (k1 * 0.044715) * (x * x))))


def _layer_norm(h, g, b):
    mu = jnp.mean(h, axis=-1, keepdims=True)
    d = h - mu
    var = jnp.mean(d * d, axis=-1, keepdims=True)
    return d * lax.rsqrt(var + LN_EPS) * g + b


def _ret_log_g(h):
    return math.log1p(-(2.0 ** (-5.0 - h)))


def _ret_chunk(rows, zq_ref, zk_ref, zg_ref, zv_ref, cos_ref, sin_ref, ya_ref,
               state_ref, dmat_ref, qdec_ref, kdec_ref):
    heads = range(RET_HEADS)
    cols = [slice(h * RET_DK, (h + 1) * RET_DK) for h in heads]
    cs = cos_ref[rows, :]
    sn = sin_ref[rows, :]
    qr, kr = [], []
    for h in heads:
        q = zq_ref[rows, cols[h]]
        k = zk_ref[rows, cols[h]]
        qr.append(q * cs + pltpu.roll(q, RET_DK // 2, 1) * sn)
        kr.append((k * cs + pltpu.roll(k, RET_DK // 2, 1) * sn) * (RET_DK ** -0.5))
    scores = [_dot_nt(qr[h].astype(BF16), kr[h].astype(BF16)) * dmat_ref[h] for h in heads]
    cross = [_dot((qr[h] * qdec_ref[h]).astype(BF16), state_ref[h].astype(BF16)) for h in heads]
    kd_t = [jnp.transpose(kr[h] * kdec_ref[h]).astype(BF16) for h in heads]
    for h in heads:
        v = zv_ref[rows, cols[h]]
        y = _dot(scores[h].astype(BF16), v) + cross[h]
        state_ref[h] = math.exp(_ret_log_g(h) * BLOCK) * state_ref[h] + _dot(kd_t[h], v)
        mu = jnp.mean(y, axis=-1, keepdims=True)
        d = y - mu
        var = jnp.mean(d * d, axis=-1, keepdims=True)
        yn = d * lax.rsqrt(var + LN_EPS)
        ya_ref[rows, cols[h]] = (jax.nn.silu(zg_ref[rows, cols[h]]) * yn).astype(ya_ref.dtype)


def _proj_kernel(x_ref, w_ref, b_ref, cos_ref, sin_ref, cw_ref, cb_ref, wa_ref, ba_ref, wx_ref, bx_ref,
                 lam_ref, zb_ref, zf_ref, ya_ref, yb_ref,
                 zq_ref, zk_ref, zg_ref, zv_ref, state_ref, dmat_ref, qdec_ref, kdec_ref,
                 zlx_ref, zlg_ref, tail_ref, h_ref):
    @pl.when(pl.program_id(1) == 0)
    def _():
        tail_ref[...] = jnp.zeros_like(tail_ref)
        h_ref[...] = jnp.zeros_like(h_ref)
        state_ref[...] = jnp.zeros_like(state_ref)
        row = lax.broadcasted_iota(jnp.int32, (BLOCK, BLOCK), 0).astype(F32)
        col = lax.broadcasted_iota(jnp.int32, (BLOCK, BLOCK), 1).astype(F32)
        diff = row - col
        for h in range(RET_HEADS):
            lg = _ret_log_g(h)
            dmat_ref[h] = jnp.where(diff >= 0, jnp.exp(lg * jnp.maximum(diff, 0.0)), 0.0)
            qdec_ref[h] = jnp.exp(lg * (row + 1.0))
            kdec_ref[h] = jnp.exp(lg * (BLOCK - 1.0 - row))

    xb = x_ref[...].astype(BF16)

    half = BRANCH_W // 2

    def project(part, dst_ref, row0, col0, scale=None):
        def piece(hf):
            cols = slice(part * BRANCH_W + hf * half, part * BRANCH_W + (hf + 1) * half)
            acc = _dot(xb, w_ref[:, cols]) + b_ref[:, cols]
            if scale is not None:
                acc = acc * scale
            dst_ref[row0:row0 + PROJ_TM, col0 + hf * half:col0 + (hf + 1) * half] = acc.astype(dst_ref.dtype)
        return [functools.partial(piece, 0), functools.partial(piece, 1)]

    def ret(c):
        _ret_chunk(slice(c * BLOCK, (c + 1) * BLOCK), zq_ref, zk_ref, zg_ref, zv_ref, cos_ref, sin_ref,
                   ya_ref, state_ref, dmat_ref, qdec_ref, kdec_ref)

    def lru(blk, hf):
        _lru_piece(blk * LRU_ROWS, hf, zlx_ref, cw_ref, cb_ref, wa_ref, ba_ref, wx_ref, bx_ref, lam_ref,
                   zlg_ref, tail_ref, h_ref, yb_ref)

    for piece in project(W_LRU_X, zlx_ref, 0, 0) + project(W_LRU_G, zlg_ref, 0, 0):
        piece()
    matmuls = (project(W_RET_Q, zq_ref, 0, 0) + project(W_RET_K, zk_ref, 0, 0)
               + project(W_RET_G, zg_ref, 0, 0) + project(W_RET_V, zv_ref, 0, 0)
               + project(W_SG_U, zf_ref, 0, ZF_SG_U * BRANCH_W) + project(W_SG_V, zf_ref, 0, ZF_SG_V * BRANCH_W)
               + project(W_SB_Q, zb_ref, 0, ZB_SB_Q * BRANCH_W, SB_HD ** -0.5)
               + project(W_SB_K, zb_ref, 0, ZB_SB_K * BRANCH_W) + project(W_SB_V, zb_ref, 0, ZB_SB_V * BRANCH_W))
    nblk = PROJ_TM // LRU_ROWS
    lru_pieces = [functools.partial(lru, blk, hf) for blk in range(nblk)
                  for hf in range(BRANCH_W // LRU_LANES)]
    ret_pieces = [functools.partial(ret, c) for c in range(PROJ_TM // BLOCK)]
    late = len(ret_pieces)
    branch = lru_pieces[:-late]
    for c, piece in enumerate(ret_pieces):
        branch += [piece] + lru_pieces[len(lru_pieces) - late + c:len(lru_pieces) - late + c + 1]
    ret_needs = 8
    issued = 0
    for n, piece in enumerate(branch):
        if piece in ret_pieces and issued < ret_needs:
            for m in matmuls[issued:ret_needs]:
                m()
            issued = ret_needs
        piece()
        upto = max(issued, min(len(matmuls), -(-len(matmuls) * (n + 1) // len(branch))))
        for m in matmuls[issued:upto]:
            m()
        issued = upto
    for m in matmuls[issued:]:
        m()


def _project(x2, w_layers, layer, b, cos_t, sin_t, conv_w, conv_b, wa_bd, ba, wx_bd, bx, lam, batch, seq):
    t, d = x2.shape
    nblk = seq // PROJ_TM
    row_map = lambda bi, n: (bi * nblk + n, 0)
    whole = functools.partial(pl.BlockSpec, pipeline_mode=pl.Buffered(1))
    tile = lambda width: pl.BlockSpec((PROJ_TM, width), row_map)
    vec = pl.BlockSpec((1, BRANCH_W), lambda bi, n: (0, 0))
    mat = pl.BlockSpec((BRANCH_W // LRU_LANES, LRU_LANES, LRU_LANES), lambda bi, n: (0, 0, 0))
    tile_f32 = pltpu.VMEM((PROJ_TM, BRANCH_W), F32)
    return pl.pallas_call(
        _proj_kernel,
        grid=(batch, nblk),
        in_specs=[
            tile(d),
            whole((None, d, W_COLS), lambda bi, n: (layer, 0, 0)),
            pl.BlockSpec((1, W_COLS), lambda bi, n: (0, 0)),
            pl.BlockSpec((PROJ_TM, RET_DK), lambda bi, n: (n, 0)),
            pl.BlockSpec((PROJ_TM, RET_DK), lambda bi, n: (n, 0)),
            pl.BlockSpec((CONV_W, BRANCH_W), lambda bi, n: (0, 0)),
            vec, mat, vec, mat, vec, vec,
        ],
        out_specs=[tile(ZB_COLS), tile(ZF_COLS), tile(BRANCH_W), tile(BRANCH_W)],
        out_shape=[jax.ShapeDtypeStruct((t, ZB_COLS), BF16), jax.ShapeDtypeStruct((t, ZF_COLS), F32),
                   jax.ShapeDtypeStruct((t, BRANCH_W), BF16), jax.ShapeDtypeStruct((t, BRANCH_W), BF16)],
        scratch_shapes=[tile_f32] * 3 + [pltpu.VMEM((PROJ_TM, BRANCH_W), BF16)]
        + [pltpu.VMEM((RET_HEADS, RET_DK, RET_DK), F32)] * 4
        + [tile_f32, tile_f32, pltpu.VMEM((8 * (CONV_W - 1), BRANCH_W), F32), pltpu.VMEM((8, BRANCH_W), F32)],
        compiler_params=_params("arbitrary", "arbitrary"),
        name="in_proj_retention_lru",
    )(x2, w_layers, b, cos_t, sin_t, conv_w, conv_b, wa_bd, ba, wx_bd, bx, lam)


def _lru_piece(r0, hf, zlx_ref, cw_ref, cb_ref, wa_ref, ba_ref, wx_ref, bx_ref, lam_ref, gate_ref,
               tail_ref, h_ref, o_ref):
    rows = LRU_ROWS
    per = rows // 8
    half = LRU_LANES
    lanes = slice(hf * half, (hf + 1) * half)
    xp = pltpu.einshape("(sj)c->(js)c", zlx_ref[r0:r0 + rows, lanes], s=8)
    sub = lax.broadcasted_iota(jnp.int32, (8, half), 0)

    def delayed(k):
        heads = []
        for m in range(per - k, per):
            prev = tail_ref[8 * (m - (per - 3)):8 * (m - (per - 3)) + 8, lanes]
            heads.append(pltpu.roll(jnp.where(sub == 7, prev, xp[8 * m:8 * m + 8, :]), 1, 0))
        return jnp.concatenate(heads + [xp[0:8 * (per - k), :]], axis=0)

    xc = cb_ref[:, lanes] + cw_ref[CONV_W - 1:CONV_W, lanes] * xp
    for k in range(1, CONV_W):
        xc = xc + cw_ref[CONV_W - 1 - k:CONV_W - k, lanes] * delayed(k)
    tail_ref[:, lanes] = xp[8 * (per - 3):8 * per, :]

    xcb = xc.astype(BF16)
    r = jax.nn.sigmoid(_dot(xcb, wa_ref[hf]) + ba_ref[:, lanes])
    i = jax.nn.sigmoid(_dot(xcb, wx_ref[hf]) + bx_ref[:, lanes])
    nlam = -lam_ref[:, lanes]
    softplus = jnp.maximum(nlam, 0.0) + jnp.log1p(jnp.exp(-jnp.abs(nlam)))
    log_a = -LRU_C * r * softplus
    a_all = jnp.exp(log_a)
    u_all = jnp.sqrt(jnp.tanh(-log_a) * (1.0 + a_all * a_all)) * (i * xc)

    sub1 = sub[:, 0:128]
    slabs = []
    for s in range(half // 128):
        sl = slice(s * 128, (s + 1) * 128)
        cols = slice(hf * half + s * 128, hf * half + (s + 1) * 128)
        run_h, run_p = [u_all[0:8, sl]], [a_all[0:8, sl]]
        for j in range(1, per):
            aj = a_all[8 * j:8 * j + 8, sl]
            run_h.append(aj * run_h[-1] + u_all[8 * j:8 * j + 8, sl])
            run_p.append(aj * run_p[-1])
        tot_h, tot_p = run_h[-1], run_p[-1]
        for shift in (1, 2, 4):
            keep = sub1 >= shift
            p_sh = jnp.where(keep, pltpu.roll(tot_p, shift, 0), 1.0)
            h_sh = jnp.where(keep, pltpu.roll(tot_h, shift, 0), 0.0)
            tot_h = tot_p * h_sh + tot_h
            tot_p = tot_p * p_sh
        hb = h_ref[:, cols]
        ends = tot_h + tot_p * hb
        start = jnp.where(sub1 == 0, hb, pltpu.roll(ends, 1, 0))
        h_ref[:, cols] = jnp.broadcast_to(ends[7:8, :], (8, 128))
        slabs.append(jnp.concatenate([run_h[j] + run_p[j] * start for j in range(per)], axis=0))
    h = pltpu.einshape("(js)c->(sj)c", jnp.concatenate(slabs, axis=1), s=8)
    o_ref[r0:r0 + rows, lanes] = (h * _gelu(gate_ref[r0:r0 + rows, lanes])).astype(o_ref.dtype)


def _sb_key_blocks(q_ref, qrows, k_ref, v_ref, acc_ref, carry_ref, tri, offs, diag_mask):
    t = SB_TQ
    heads = range(SB_HEADS)
    blocks = range(len(offs))
    cols = [slice(h * SB_HD, (h + 1) * SB_HD) for h in heads]
    zs = [[_dot_nt(q_ref[qrows, cols[h]], k_ref[pl.ds(offs[b], t), cols[h]]) for b in blocks] for h in heads]
    cums = [[None] * len(offs) for _ in heads]
    for h in heads:
        for b in blocks:
            z = zs[h][b]
            lg = jnp.log(1.0 + jnp.exp2(jnp.abs(z) * (-LOG2E)))
            log_1m = jnp.minimum(-z, 0.0) - lg
            if b == 0 and diag_mask is not None:
                log_1m = jnp.where(diag_mask, log_1m, 0.0)
            hi = log_1m.astype(BF16)
            lo = (log_1m - hi.astype(F32)).astype(BF16)
            cums[h][b] = _dot(hi, tri) + _dot(lo, tri)
    for h in heads:
        carry = None if diag_mask is not None else carry_ref[h]
        pv = None
        for b in blocks:
            e = zs[h][b] + cums[h][b]
            if carry is not None:
                e = e + jnp.concatenate([carry] * (t // 128), axis=1)
            w = jnp.exp(e)
            if b == 0 and diag_mask is not None:
                w = jnp.where(diag_mask, w, 0.0)
            d = _dot(w.astype(BF16), v_ref[pl.ds(offs[b], t), cols[h]])
            pv = d if pv is None else pv + d
            block_sum = cums[h][b][:, 0:1]
            carry = jnp.broadcast_to(block_sum, (t, 128)) if carry is None else carry + block_sum
        carry_ref[h] = carry
        if diag_mask is not None:
            acc_ref[:, cols[h]] = pv
        else:
            acc_ref[:, cols[h]] += pv


def _sb_kernel(q_ref, k_ref, v_ref, o_ref, acc_ref, carry_ref):
    t = SB_TQ
    row = lax.broadcasted_iota(jnp.int32, (t, t), 0)
    col = lax.broadcasted_iota(jnp.int32, (t, t), 1)
    tri = (row >= col).astype(BF16)

    def any_weight_left():
        m = carry_ref[0]
        for h in range(1, SB_HEADS):
            m = jnp.maximum(m, carry_ref[h])
        return jnp.max(m) > SB_EXP_ZERO

    def query_block(j, _):
        i = pl.program_id(1) * SB_STEP_BLOCKS + j
        qrows = pl.ds(pl.multiple_of(j * t, t), t)
        diag = pl.multiple_of(i * t, t)

        @pl.when(i == 0)
        def _():
            _sb_key_blocks(q_ref, qrows, k_ref, v_ref, acc_ref, carry_ref, tri, [diag], col < row)

        @pl.when(i > 0)
        def _():
            prev = pl.multiple_of((i - 1) * t, t)
            _sb_key_blocks(q_ref, qrows, k_ref, v_ref, acc_ref, carry_ref, tri, [diag, prev], col < row)

        def cond(state):
            step, live = state
            return jnp.logical_and(step < i, live)

        def body(state):
            step, _ = state
            off = pl.multiple_of((i - 1 - step) * t, t)
            _sb_key_blocks(q_ref, qrows, k_ref, v_ref, acc_ref, carry_ref, tri, [off], None)
            return step + 1, any_weight_left()

        lax.while_loop(cond, body, (jnp.int32(1), any_weight_left()))
        o_ref[qrows, :] = acc_ref[...].astype(o_ref.dtype)
        return 0

    lax.fori_loop(0, SB_STEP_BLOCKS, query_block, 0)


def _stick_breaking(zb, batch, seq):
    rows = SB_TQ * SB_STEP_BLOCKS
    zb3 = zb.reshape(batch, seq, ZB_COLS)
    return pl.pallas_call(
        _sb_kernel,
        grid=(batch, seq // rows),
        in_specs=[
            pl.BlockSpec((None, rows, BRANCH_W), lambda b, i: (b, i, ZB_SB_Q)),
            pl.BlockSpec((None, seq, BRANCH_W), lambda b, i: (b, 0, ZB_SB_K)),
            pl.BlockSpec((None, seq, BRANCH_W), lambda b, i: (b, 0, ZB_SB_V)),
        ],
        out_specs=pl.BlockSpec((None, rows, BRANCH_W), lambda b, i: (b, i, 0)),
        out_shape=jax.ShapeDtypeStruct((batch, seq, BRANCH_W), BF16),
        scratch_shapes=[pltpu.VMEM((SB_TQ, BRANCH_W), F32), pltpu.VMEM((SB_HEADS, SB_TQ, 128), F32)],
        compiler_params=_params("arbitrary", "arbitrary"),
        name="stick_breaking",
    )(zb3, zb3, zb3).reshape(batch * seq, BRANCH_W)


def _sg_chunk(rows, u_ref, v_ref, g_ref, b_ref, ws_ref, bs_ref, o_ref):
    u = _gelu(u_ref[rows, :])
    v = _layer_norm(_gelu(v_ref[rows, :]), g_ref[...], b_ref[...]).astype(BF16)
    row = lax.broadcasted_iota(jnp.int32, (BLOCK, BLOCK), 0)
    col = lax.broadcasted_iota(jnp.int32, (BLOCK, BLOCK), 1)
    gd = BRANCH_W // SG_GROUPS
    for g in range(SG_GROUPS):
        w = jnp.where(col <= row, ws_ref[g], 0.0).astype(BF16)
        cols = slice(g * gd, (g + 1) * gd)
        s = _dot(w, v[:, cols]) + bs_ref[g]
        o_ref[rows, cols] = (u[:, cols] * s).astype(o_ref.dtype)


def _merge_kernel(ya_ref, yb_ref, yc_ref, u_ref, v_ref, x_ref, sg_g_ref, sg_b_ref, ws_ref, bs_ref,
                  wg_ref, bg_ref, wb_ref, wo_ref, lg_ref, lb_ref, o_ref, yd_ref):
    xb = x_ref[...].astype(BF16)
    nchunk = MERGE_TM // BLOCK
    third = -(-nchunk // 3)
    sg_after = {0: range(0, third), 1: range(third, 2 * third), 2: range(2 * third, nchunk)}
    merged = None
    for n, y_ref in enumerate((ya_ref, yb_ref, yc_ref, yd_ref)):
        cols = slice(n * D_MODEL, (n + 1) * D_MODEL)
        logits = _dot(xb, wg_ref[0, :, cols]) + bg_ref[:, cols]
        for c in sg_after.get(n, ()):
            _sg_chunk(slice(c * BLOCK, (c + 1) * BLOCK), u_ref, v_ref, sg_g_ref, sg_b_ref, ws_ref, bs_ref,
                      yd_ref)
        term = jax.nn.sigmoid(logits) * _dot(y_ref[...], wb_ref[n])
        merged = term if merged is None else merged + term
    mb = merged.astype(BF16)
    nsub = 4
    sub = MERGE_TM // nsub
    outs = [_dot(mb[r * sub:(r + 1) * sub, :], wo_ref[...]) for r in range(nsub)]
    for r in range(nsub):
        rows = slice(r * sub, (r + 1) * sub)
        o_ref[rows, :] = _layer_norm(ALPHA * x_ref[rows, :] + outs[r], lg_ref[...], lb_ref[...])


def _merge(ys, zf, x2, sg_ln_g, sg_ln_b, ws, bs_b, layer, w_gate, b_gate, w_branch, w_out, ln_g, ln_b):
    t = x2.shape[0]
    ysp = pl.BlockSpec((MERGE_TM, BRANCH_W), lambda i: (i, 0))
    vec = pl.BlockSpec((1, D_MODEL), lambda i: (0, 0))
    hvec = pl.BlockSpec((1, BRANCH_W), lambda i: (0, 0))
    cube = pl.BlockSpec((SG_GROUPS, BLOCK, BLOCK), lambda i: (0, 0, 0))
    whole = functools.partial(pl.BlockSpec, pipeline_mode=pl.Buffered(1))
    return pl.pallas_call(
        _merge_kernel,
        grid=(t // MERGE_TM,),
        in_specs=[ysp] * 3 + [
            pl.BlockSpec((MERGE_TM, BRANCH_W), lambda i: (i, ZF_SG_U)),
            pl.BlockSpec((MERGE_TM, BRANCH_W), lambda i: (i, ZF_SG_V)),
            pl.BlockSpec((MERGE_TM, D_MODEL), lambda i: (i, 0)),
            hvec, hvec, cube, cube,
            whole((pl.Element(1), pl.Element(D_MODEL), pl.Element(N_BRANCH * D_MODEL)),
                  lambda i: (layer, 0, W_COLS)),
            pl.BlockSpec((1, N_BRANCH * D_MODEL), lambda i: (0, 0)),
            whole((None, N_BRANCH, BRANCH_W, D_MODEL), lambda i: (layer, 0, 0, 0)),
            whole((None, D_MODEL, D_MODEL), lambda i: (layer, 0, 0)),
            vec, vec,
        ],
        out_specs=pl.BlockSpec((MERGE_TM, D_MODEL), lambda i: (i, 0)),
        out_shape=jax.ShapeDtypeStruct((t, D_MODEL), F32),
        scratch_shapes=[pltpu.VMEM((MERGE_TM, BRANCH_W), BF16)],
        compiler_params=_params("arbitrary"),
        name="sg_merge_out",
    )(*ys, zf, zf, x2, sg_ln_g, sg_ln_b, ws, bs_b, w_gate, b_gate, w_branch, w_out, ln_g, ln_b)


def _mlp_kernel(x_ref, w1_ref, b1_ref, w2_ref, b2_ref, lg_ref, lb_ref, o_ref):
    def finish(rows, acc):
        o_ref[rows, :] = _layer_norm(ALPHA * x_ref[rows, :] + (acc + b2_ref[...]), lg_ref[...], lb_ref[...])

    done = []
    for r in range(MLP_TM // MLP_SUB):
        rows = slice(r * MLP_SUB, (r + 1) * MLP_SUB)
        xb = x_ref[rows, :].astype(BF16)
        acc = None
        for c in range(D_FF // MLP_FC):
            cols = slice(c * MLP_FC, (c + 1) * MLP_FC)
            hid = jnp.maximum(_dot(xb, w1_ref[:, cols]) + b1_ref[:, cols], 0.0)
            part = _dot((hid * hid).astype(BF16), w2_ref[cols, :])
            acc = part if acc is None else acc + part
        done.append((rows, acc))
    for rows, acc in done:
        finish(rows, acc)


def _mlp(x2, layer, w1, b1, w2, b2, ln_g, ln_b):
    t = x2.shape[0]
    vec = pl.BlockSpec((1, D_MODEL), lambda i: (0, 0))
    return pl.pallas_call(
        _mlp_kernel,
        grid=(t // MLP_TM,),
        in_specs=[
            pl.BlockSpec((MLP_TM, D_MODEL), lambda i: (i, 0)),
            pl.BlockSpec((None, D_MODEL, D_FF), lambda i: (layer, 0, 0), pipeline_mode=pl.Buffered(1)),
            pl.BlockSpec((1, D_FF), lambda i: (0, 0)),
            pl.BlockSpec((None, D_FF, D_MODEL), lambda i: (layer, 0, 0), pipeline_mode=pl.Buffered(1)),
            vec, vec, vec,
        ],
        out_specs=pl.BlockSpec((MLP_TM, D_MODEL), lambda i: (i, 0)),
        out_shape=jax.ShapeDtypeStruct((t, D_MODEL), F32),
        compiler_params=_params("arbitrary"),
        name="mlp",
    )(x2, w1, b1, w2, b2, ln_g, ln_b)


def _block_diag(w):
    per = LRU_LANES // LRU_HD
    groups = LRU_HEADS // per
    w4 = w.reshape(groups, per, LRU_HD, LRU_HD)
    eye = jnp.eye(per, dtype=w.dtype)
    return jnp.einsum('ghij,hk->ghikj', w4, eye).reshape(groups, LRU_LANES, LRU_LANES)


def _rope_tables(seq):
    half = RET_DK // 2
    pos = np.arange(seq, dtype=np.float32)
    inv_freq = (np.float32(ROPE_THETA) ** (-np.arange(half, dtype=np.float32) / np.float32(half))).astype(np.float32)
    ang = pos[:, None] * inv_freq[None, :]
    cos, sin = np.cos(ang), np.sin(ang)
    return (jnp.asarray(np.concatenate([cos, cos], axis=-1), F32),
            jnp.asarray(np.concatenate([-sin, sin], axis=-1), F32))


def _layer(x2, batch, seq, cos_t, sin_t, layer, big, b_in, conv_w, conv_b, lru_wa, lru_ba, lru_wx,
           lru_bx, lru_lambda, sg_ln_g, sg_ln_b, sg_ws, sg_bs, ln1_g, ln1_b, b1, b2, ln2_g, ln2_b):
    w_proj, w_gate, w_branch, w_out, w1, w2 = big
    gl0 = W_COLS
    zb, zf, y_a, y_b = _project(
        x2, w_proj, layer, b_in[None, :gl0], cos_t, sin_t, conv_w, conv_b[None, :],
        _block_diag(lru_wa).astype(BF16), lru_ba[None, :], _block_diag(lru_wx).astype(BF16),
        lru_bx[None, :], lru_lambda[None, :], batch, seq)
    y_c = _stick_breaking(zb, batch, seq)
    bs_b = jnp.broadcast_to(sg_bs[:, :, None], (SG_GROUPS, BLOCK, BRANCH_W // SG_GROUPS))

    x1 = _merge((y_a, y_b, y_c), zf, x2, sg_ln_g[None, :], sg_ln_b[None, :], sg_ws, bs_b, layer,
                w_gate, b_in[None, gl0:], w_branch, w_out, ln1_g[None, :], ln1_b[None, :])
    return _mlp(x1, layer, w1, b1[None, :], w2, b2[None, :], ln2_g[None, :], ln2_b[None, :])


def kernel(x, w_in, b_in, conv_w, conv_b, lru_wa, lru_ba, lru_wx, lru_bx, lru_lambda, sg_ln_g, sg_ln_b, sg_ws, sg_bs, w_branch, w_out, ln1_g, ln1_b, w1, b1, w2, b2, ln2_g, ln2_b):
    batch, seq, d = x.shape
    cos_t, sin_t = _rope_tables(seq)
    x2 = x.reshape(batch * seq, d)
    w_in_b = w_in.astype(BF16)
    big = (w_in_b, w_in_b, w_branch.astype(BF16), w_out.astype(BF16), w1.astype(BF16), w2.astype(BF16))
    per_layer = (b_in, conv_w, conv_b, lru_wa, lru_ba, lru_wx, lru_bx, lru_lambda, sg_ln_g,
                 sg_ln_b, sg_ws, sg_bs, ln1_g, ln1_b, b1, b2, ln2_g, ln2_b)
    for l in range(DEPTH):
        x2 = _layer(x2, batch, seq, cos_t, sin_t, l, big, *(p[l] for p in per_layer))
    return x2.reshape(batch, seq, d)
```

```python
import functools
import math

import jax
import jax.numpy as jnp
import numpy as np
from jax import lax
from jax.experimental import pallas as pl
from jax.experimental.pallas import tpu as pltpu

F32 = jnp.float32
BF16 = jnp.bfloat16

D_MODEL = 1024
DEPTH = 2
BLOCK = 128
BRANCH_W = D_MODEL // 2
N_BRANCH = 4
RET_HEADS = 4
RET_DK = BRANCH_W // RET_HEADS
ROPE_THETA = 10000.0
LRU_HEADS = 8
LRU_HD = BRANCH_W // LRU_HEADS
CONV_W = 4
LRU_C = 8.0
SB_HEADS = 4
SB_HD = BRANCH_W // SB_HEADS
SG_GROUPS = 4
D_FF = 4 * D_MODEL
ALPHA = (2 * DEPTH) ** 0.25
LN_EPS = 1e-5
LOG2E = 1.4426950408889634
SB_EXP_ZERO = -105.0

W_RET_Q, W_RET_K, W_RET_V, W_RET_G, W_LRU_X, W_LRU_G, W_SB_Q, W_SB_K, W_SB_V, W_SG_U, W_SG_V = range(11)
W_COLS = 11 * BRANCH_W
ZF_SG_U, ZF_SG_V = 0, 1
ZF_COLS = 2 * BRANCH_W
ZB_SB_Q, ZB_SB_K, ZB_SB_V = 0, 1, 2
ZB_COLS = 3 * BRANCH_W

VMEM_LIMIT = 60 * 1024 * 1024

PROJ_TM = 1024
LRU_ROWS = 128
LRU_LANES = 256
SB_TQ = 256
SB_STEP_BLOCKS = 4
MERGE_TM = 1024
MLP_TM = 1024
MLP_SUB = 512
MLP_FC = 2048


def _params(*sem):
    return pltpu.CompilerParams(dimension_semantics=sem, vmem_limit_bytes=VMEM_LIMIT)


def _dot(a, b):
    return jnp.dot(a, b, preferred_element_type=F32)


def _dot_nt(a, b):
    return lax.dot_general(a, b, (((1,), (1,)), ((), ())), preferred_element_type=F32)


def _gelu(x):
    k1 = -2.0 * math.sqrt(2.0 / math.pi) * LOG2E
    return x / (1.0 + jnp.exp2(x * (k1 + (k1 * 0.044715) * (x * x))))


def _layer_norm(h, g, b):
    mu = jnp.mean(h, axis=-1, keepdims=True)
    d = h - mu
    var = jnp.mean(d * d, axis=-1, keepdims=True)
    return d * lax.rsqrt(var + LN_EPS) * g + b


def _ret_log_g(h):
    return math.log1p(-(2.0 ** (-5.0 - h)))


def _ret_chunk(rows, zq_ref, zk_ref, zg_ref, zv_ref, cos_ref, sin_ref, ya_ref,
               state_ref, dmat_ref, qdec_ref, kdec_ref):
    heads = range(RET_HEADS)
    cols = [slice(h * RET_DK, (h + 1) * RET_DK) for h in heads]
    cs = cos_ref[rows, :]
    sn = sin_ref[rows, :]
    qr, kr = [], []
    for h in heads:
        q = zq_ref[rows, cols[h]]
        k = zk_ref[rows, cols[h]]
        qr.append(q * cs + pltpu.roll(q, RET_DK // 2, 1) * sn)
        kr.append((k * cs + pltpu.roll(k, RET_DK // 2, 1) * sn) * (RET_DK ** -0.5))
    scores = [_dot_nt(qr[h].astype(BF16), kr[h].astype(BF16)) * dmat_ref[h] for h in heads]
    cross = [_dot((qr[h] * qdec_ref[h]).astype(BF16), state_ref[h].astype(BF16)) for h in heads]
    kd_t = [jnp.transpose(kr[h] * kdec_ref[h]).astype(BF16) for h in heads]
    for h in heads:
        v = zv_ref[rows, cols[h]]
        y = _dot(scores[h].astype(BF16), v) + cross[h]
        state_ref[h] = math.exp(_ret_log_g(h) * BLOCK) * state_ref[h] + _dot(kd_t[h], v)
        mu = jnp.mean(y, axis=-1, keepdims=True)
        d = y - mu
        var = jnp.mean(d * d, axis=-1, keepdims=True)
        yn = d * lax.rsqrt(var + LN_EPS)
        ya_ref[rows, cols[h]] = (jax.nn.silu(zg_ref[rows, cols[h]]) * yn).astype(ya_ref.dtype)


def _proj_kernel(x_ref, w_ref, b_ref, cos_ref, sin_ref, cw_ref, cb_ref, wa_ref, ba_ref, wx_ref, bx_ref,
                 lam_ref, zb_ref, zf_ref, ya_ref, yb_ref,
                 zq_ref, zk_ref, zg_ref, zv_ref, state_ref, dmat_ref, qdec_ref, kdec_ref,
                 zlx_ref, zlg_ref, tail_ref, h_ref):
    @pl.when(pl.program_id(1) == 0)
    def _():
        tail_ref[...] = jnp.zeros_like(tail_ref)
        h_ref[...] = jnp.zeros_like(h_ref)
        state_ref[...] = jnp.zeros_like(state_ref)
        row = lax.broadcasted_iota(jnp.int32, (BLOCK, BLOCK), 0).astype(F32)
        col = lax.broadcasted_iota(jnp.int32, (BLOCK, BLOCK), 1).astype(F32)
        diff = row - col
        for h in range(RET_HEADS):
            lg = _ret_log_g(h)
            dmat_ref[h] = jnp.where(diff >= 0, jnp.exp(lg * jnp.maximum(diff, 0.0)), 0.0)
            qdec_ref[h] = jnp.exp(lg * (row + 1.0))
            kdec_ref[h] = jnp.exp(lg * (BLOCK - 1.0 - row))

    xb = x_ref[...].astype(BF16)

    half = BRANCH_W // 2

    def project(part, dst_ref, row0, col0, scale=None):
        def piece(hf):
            cols = slice(part * BRANCH_W + hf * half, part * BRANCH_W + (hf + 1) * half)
            acc = _dot(xb, w_ref[:, cols]) + b_ref[:, cols]
            if scale is not None:
                acc = acc * scale
            dst_ref[row0:row0 + PROJ_TM, col0 + hf * half:col0 + (hf + 1) * half] = acc.astype(dst_ref.dtype)
        return [functools.partial(piece, 0), functools.partial(piece, 1)]

    def ret(c):
        _ret_chunk(slice(c * BLOCK, (c + 1) * BLOCK), zq_ref, zk_ref, zg_ref, zv_ref, cos_ref, sin_ref,
                   ya_ref, state_ref, dmat_ref, qdec_ref, kdec_ref)

    def lru(blk, hf):
        _lru_piece(blk * LRU_ROWS, hf, zlx_ref, cw_ref, cb_ref, wa_ref, ba_ref, wx_ref, bx_ref, lam_ref,
                   zlg_ref, tail_ref, h_ref, yb_ref)

    for piece in project(W_LRU_X, zlx_ref, 0, 0) + project(W_LRU_G, zlg_ref, 0, 0):
        piece()
    matmuls = (project(W_RET_Q, zq_ref, 0, 0) + project(W_RET_K, zk_ref, 0, 0)
               + project(W_RET_G, zg_ref, 0, 0) + project(W_RET_V, zv_ref, 0, 0)
               + project(W_SG_U, zf_ref, 0, ZF_SG_U * BRANCH_W) + project(W_SG_V, zf_ref, 0, ZF_SG_V * BRANCH_W)
               + project(W_SB_Q, zb_ref, 0, ZB_SB_Q * BRANCH_W, SB_HD ** -0.5)
               + project(W_SB_K, zb_ref, 0, ZB_SB_K * BRANCH_W) + project(W_SB_V, zb_ref, 0, ZB_SB_V * BRANCH_W))
    nblk = PROJ_TM // LRU_ROWS
    lru_pieces = [functools.partial(lru, blk, hf) for blk in range(nblk)
                  for hf in range(BRANCH_W // LRU_LANES)]
    ret_pieces = [functools.partial(ret, c) for c in range(PROJ_TM // BLOCK)]
    late = len(ret_pieces)
    branch = lru_pieces[:-late]
    for c, piece in enumerate(ret_pieces):
        branch += [piece] + lru_pieces[len(lru_pieces) - late + c:len(lru_pieces) - late + c + 1]
    ret_needs = 8
    issued = 0
    for n, piece in enumerate(branch):
        if piece in ret_pieces and issued < ret_needs:
            for m in matmuls[issued:ret_needs]:
                m()
            issued = ret_needs
        piece()
        upto = max(issued, min(len(matmuls), -(-len(matmuls) * (n + 1) // len(branch))))
        for m in matmuls[issued:upto]:
            m()
        issued = upto
    for m in matmuls[issued:]:
        m()


def _project(x2, w_layers, layer, b, cos_t, sin_t, conv_w, conv_b, wa_bd, ba, wx_bd, bx, lam, batch, seq):
    t, d = x2.shape
    nblk = seq // PROJ_TM
    row_map = lambda bi, n: (bi * nblk + n, 0)
    whole = functools.partial(pl.BlockSpec, pipeline_mode=pl.Buffered(1))
    tile = lambda width: pl.BlockSpec((PROJ_TM, width), row_map)
    vec = pl.BlockSpec((1, BRANCH_W), lambda bi, n: (0, 0))
    mat = pl.BlockSpec((BRANCH_W // LRU_LANES, LRU_LANES, LRU_LANES), lambda bi, n: (0, 0, 0))
    tile_f32 = pltpu.VMEM((PROJ_TM, BRANCH_W), F32)
    return pl.pallas_call(
        _proj_kernel,
        grid=(batch, nblk),
        in_specs=[
            tile(d),
            whole((None, d, W_COLS), lambda bi, n: (layer, 0, 0)),
            pl.BlockSpec((1, W_COLS), lambda bi, n: (0, 0)),
            pl.BlockSpec((PROJ_TM, RET_DK), lambda bi, n: (n, 0)),
            pl.BlockSpec((PROJ_TM, RET_DK), lambda bi, n: (n, 0)),
            pl.BlockSpec((CONV_W, BRANCH_W), lambda bi, n: (0, 0)),
            vec, mat, vec, mat, vec, vec,
        ],
        out_specs=[tile(ZB_COLS), tile(ZF_COLS), tile(BRANCH_W), tile(BRANCH_W)],
        out_shape=[jax.ShapeDtypeStruct((t, ZB_COLS), BF16), jax.ShapeDtypeStruct((t, ZF_COLS), F32),
                   jax.ShapeDtypeStruct((t, BRANCH_W), BF16), jax.ShapeDtypeStruct((t, BRANCH_W), BF16)],
        scratch_shapes=[tile_f32] * 3 + [pltpu.VMEM((PROJ_TM, BRANCH_W), BF16)]
        + [pltpu.VMEM((RET_HEADS, RET_DK, RET_DK), F32)] * 4
        + [tile_f32, tile_f32, pltpu.VMEM((8 * (CONV_W - 1), BRANCH_W), F32), pltpu.VMEM((8, BRANCH_W), F32)],
        compiler_params=_params("arbitrary", "arbitrary"),
        name="in_proj_retention_lru",
    )(x2, w_layers, b, cos_t, sin_t, conv_w, conv_b, wa_bd, ba, wx_bd, bx, lam)


def _lru_piece(r0, hf, zlx_ref, cw_ref, cb_ref, wa_ref, ba_ref, wx_ref, bx_ref, lam_ref, gate_ref,
               tail_ref, h_ref, o_ref):
    rows = LRU_ROWS
    per = rows // 8
    half = LRU_LANES
    lanes = slice(hf * half, (hf + 1) * half)
    xp = pltpu.einshape("(sj)c->(js)c", zlx_ref[r0:r0 + rows, lanes], s=8)
    sub = lax.broadcasted_iota(jnp.int32, (8, half), 0)

    def delayed(k):
        heads = []
        for m in range(per - k, per):
            prev = tail_ref[8 * (m - (per - 3)):8 * (m - (per - 3)) + 8, lanes]
            heads.append(pltpu.roll(jnp.where(sub == 7, prev, xp[8 * m:8 * m + 8, :]), 1, 0))
        return jnp.concatenate(heads + [xp[0:8 * (per - k), :]], axis=0)

    xc = cb_ref[:, lanes] + cw_ref[CONV_W - 1:CONV_W, lanes] * xp
    for k in range(1, CONV_W):
        xc = xc + cw_ref[CONV_W - 1 - k:CONV_W - k, lanes] * delayed(k)
    tail_ref[:, lanes] = xp[8 * (per - 3):8 * per, :]

    xcb = xc.astype(BF16)
    r = jax.nn.sigmoid(_dot(xcb, wa_ref[hf]) + ba_ref[:, lanes])
    i = jax.nn.sigmoid(_dot(xcb, wx_ref[hf]) + bx_ref[:, lanes])
    nlam = -lam_ref[:, lanes]
    softplus = jnp.maximum(nlam, 0.0) + jnp.log1p(jnp.exp(-jnp.abs(nlam)))
    log_a = -LRU_C * r * softplus
    a_all = jnp.exp(log_a)
    u_all = jnp.sqrt(jnp.tanh(-log_a) * (1.0 + a_all * a_all)) * (i * xc)

    sub1 = sub[:, 0:128]
    slabs = []
    for s in range(half // 128):
        sl = slice(s * 128, (s + 1) * 128)
        cols = slice(hf * half + s * 128, hf * half + (s + 1) * 128)
        run_h, run_p = [u_all[0:8, sl]], [a_all[0:8, sl]]
        for j in range(1, per):
            aj = a_all[8 * j:8 * j + 8, sl]
            run_h.append(aj * run_h[-1] + u_all[8 * j:8 * j + 8, sl])
            run_p.append(aj * run_p[-1])
        tot_h, tot_p = run_h[-1], run_p[-1]
        for shift in (1, 2, 4):
            keep = sub1 >= shift
            p_sh = jnp.where(keep, pltpu.roll(tot_p, shift, 0), 1.0)
            h_sh = jnp.where(keep, pltpu.roll(tot_h, shift, 0), 0.0)
            tot_h = tot_p * h_sh + tot_h
            tot_p = tot_p * p_sh
        hb = h_ref[:, cols]
        ends = tot_h + tot_p * hb
        start = jnp.where(sub1 == 0, hb, pltpu.roll(ends, 1, 0))
        h_ref[:, cols] = jnp.broadcast_to(ends[7:8, :], (8, 128))
        slabs.append(jnp.concatenate([run_h[j] + run_p[j] * start for j in range(per)], axis=0))
    h = pltpu.einshape("(js)c->(sj)c", jnp.concatenate(slabs, axis=1), s=8)
    o_ref[r0:r0 + rows, lanes] = (h * _gelu(gate_ref[r0:r0 + rows, lanes])).astype(o_ref.dtype)


def _sb_key_blocks(q_ref, qrows, k_ref, v_ref, acc_ref, carry_ref, tri, offs, diag_mask):
    t = SB_TQ
    heads = range(SB_HEADS)
    blocks = range(len(offs))
    cols = [slice(h * SB_HD, (h + 1) * SB_HD) for h in heads]
    zs = [[_dot_nt(q_ref[qrows, cols[h]], k_ref[pl.ds(offs[b], t), cols[h]]) for b in blocks] for h in heads]
    cums = [[None] * len(offs) for _ in heads]
    for h in heads:
        for b in blocks:
            z = zs[h][b]
            sp = jnp.maximum(z, 0.0) + jnp.log(1.0 + jnp.exp2(jnp.abs(z) * (-LOG2E)))
            if b == 0 and diag_mask is not None:
                sp = jnp.where(diag_mask, sp, 0.0)
            hi = sp.astype(BF16)
            lo = (sp - hi.astype(F32)).astype(BF16)
            cums[h][b] = _dot(hi, tri) + _dot(lo, tri)
    for h in heads:
        carry = None if diag_mask is not None else carry_ref[h]
        pv = None
        for b in blocks:
            e = zs[h][b] - cums[h][b]
            if carry is not None:
                e = e - jnp.concatenate([carry] * (t // 128), axis=1)
            w = jnp.exp(e)
            if b == 0 and diag_mask is not None:
                w = jnp.where(diag_mask, w, 0.0)
            d = _dot(w.astype(BF16), v_ref[pl.ds(offs[b], t), cols[h]])
            pv = d if pv is None else pv + d
            block_sum = cums[h][b][:, 0:1]
            carry = jnp.broadcast_to(block_sum, (t, 128)) if carry is None else carry + block_sum
        carry_ref[h] = carry
        if diag_mask is not None:
            acc_ref[:, cols[h]] = pv
        else:
            acc_ref[:, cols[h]] += pv


def _sb_kernel(q_ref, k_ref, v_ref, o_ref, acc_ref, carry_ref):
    t = SB_TQ
    row = lax.broadcasted_iota(jnp.int32, (t, t), 0)
    col = lax.broadcasted_iota(jnp.int32, (t, t), 1)
    tri = (row >= col).astype(BF16)

    def any_weight_left():
        m = carry_ref[0]
        for h in range(1, SB_HEADS):
            m = jnp.minimum(m, carry_ref[h])
        return jnp.min(m) < -SB_EXP_ZERO

    def query_block(j, _):
        i = pl.program_id(1) * SB_STEP_BLOCKS + j
        qrows = pl.ds(pl.multiple_of(j * t, t), t)
        diag = pl.multiple_of(i * t, t)

        @pl.when(i == 0)
        def _():
            _sb_key_blocks(q_ref, qrows, k_ref, v_ref, acc_ref, carry_ref, tri, [diag], col < row)

        @pl.when(i > 0)
        def _():
            prev = pl.multiple_of((i - 1) * t, t)
            _sb_key_blocks(q_ref, qrows, k_ref, v_ref, acc_ref, carry_ref, tri, [diag, prev], col < row)

        def cond(state):
            step, live = state
            return jnp.logical_and(step < i, live)

        def body(state):
            step, _ = state
            off = pl.multiple_of((i - 1 - step) * t, t)
            _sb_key_blocks(q_ref, qrows, k_ref, v_ref, acc_ref, carry_ref, tri, [off], None)
            return step + 1, any_weight_left()

        lax.while_loop(cond, body, (jnp.int32(1), any_weight_left()))
        o_ref[qrows, :] = acc_ref[...].astype(o_ref.dtype)
        return 0

    lax.fori_loop(0, SB_STEP_BLOCKS, query_block, 0)


def _stick_breaking(zb, batch, seq):
    rows = SB_TQ * SB_STEP_BLOCKS
    zb3 = zb.reshape(batch, seq, ZB_COLS)
    return pl.pallas_call(
        _sb_kernel,
        grid=(batch, seq // rows),
        in_specs=[
            pl.BlockSpec((None, rows, BRANCH_W), lambda b, i: (b, i, ZB_SB_Q)),
            pl.BlockSpec((None, seq, BRANCH_W), lambda b, i: (b, 0, ZB_SB_K)),
            pl.BlockSpec((None, seq, BRANCH_W), lambda b, i: (b, 0, ZB_SB_V)),
        ],
        out_specs=pl.BlockSpec((None, rows, BRANCH_W), lambda b, i: (b, i, 0)),
        out_shape=jax.ShapeDtypeStruct((batch, seq, BRANCH_W), BF16),
        scratch_shapes=[pltpu.VMEM((SB_TQ, BRANCH_W), F32), pltpu.VMEM((SB_HEADS, SB_TQ, 128), F32)],
        compiler_params=_params("arbitrary", "arbitrary"),
        name="stick_breaking",
    )(zb3, zb3, zb3).reshape(batch * seq, BRANCH_W)


def _sg_chunk(rows, u_ref, v_ref, g_ref, b_ref, ws_ref, bs_ref, o_ref):
    u = _gelu(u_ref[rows, :])
    v = _layer_norm(_gelu(v_ref[rows, :]), g_ref[...], b_ref[...]).astype(BF16)
    row = lax.broadcasted_iota(jnp.int32, (BLOCK, BLOCK), 0)
    col = lax.broadcasted_iota(jnp.int32, (BLOCK, BLOCK), 1)
    gd = BRANCH_W // SG_GROUPS
    for g in range(SG_GROUPS):
        w = jnp.where(col <= row, ws_ref[g], 0.0).astype(BF16)
        cols = slice(g * gd, (g + 1) * gd)
        s = _dot(w, v[:, cols]) + bs_ref[g]
        o_ref[rows, cols] = (u[:, cols] * s).astype(o_ref.dtype)


def _merge_kernel(ya_ref, yb_ref, yc_ref, u_ref, v_ref, x_ref, sg_g_ref, sg_b_ref, ws_ref, bs_ref,
                  wg_ref, bg_ref, wb_ref, wo_ref, lg_ref, lb_ref, o_ref, yd_ref):
    xb = x_ref[...].astype(BF16)
    nchunk = MERGE_TM // BLOCK
    third = -(-nchunk // 3)
    sg_after = {0: range(0, third), 1: range(third, 2 * third), 2: range(2 * third, nchunk)}
    merged = None
    for n, y_ref in enumerate((ya_ref, yb_ref, yc_ref, yd_ref)):
        cols = slice(n * D_MODEL, (n + 1) * D_MODEL)
        logits = _dot(xb, wg_ref[0, :, cols]) + bg_ref[:, cols]
        for c in sg_after.get(n, ()):
            _sg_chunk(slice(c * BLOCK, (c + 1) * BLOCK), u_ref, v_ref, sg_g_ref, sg_b_ref, ws_ref, bs_ref,
                      yd_ref)
        term = jax.nn.sigmoid(logits) * _dot(y_ref[...], wb_ref[n])
        merged = term if merged is None else merged + term
    mb = merged.astype(BF16)
    nsub = 4
    sub = MERGE_TM // nsub
    outs = [_dot(mb[r * sub:(r + 1) * sub, :], wo_ref[...]) for r in range(nsub)]
    for r in range(nsub):
        rows = slice(r * sub, (r + 1) * sub)
        o_ref[rows, :] = _layer_norm(ALPHA * x_ref[rows, :] + outs[r], lg_ref[...], lb_ref[...])


def _merge(ys, zf, x2, sg_ln_g, sg_ln_b, ws, bs_b, layer, w_gate, b_gate, w_branch, w_out, ln_g, ln_b):
    t = x2.shape[0]
    ysp = pl.BlockSpec((MERGE_TM, BRANCH_W), lambda i: (i, 0))
    vec = pl.BlockSpec((1, D_MODEL), lambda i: (0, 0))
    hvec = pl.BlockSpec((1, BRANCH_W), lambda i: (0, 0))
    cube = pl.BlockSpec((SG_GROUPS, BLOCK, BLOCK), lambda i: (0, 0, 0))
    whole = functools.partial(pl.BlockSpec, pipeline_mode=pl.Buffered(1))
    return pl.pallas_call(
        _merge_kernel,
        grid=(t // MERGE_TM,),
        in_specs=[ysp] * 3 + [
            pl.BlockSpec((MERGE_TM, BRANCH_W), lambda i: (i, ZF_SG_U)),
            pl.BlockSpec((MERGE_TM, BRANCH_W), lambda i: (i, ZF_SG_V)),
            pl.BlockSpec((MERGE_TM, D_MODEL), lambda i: (i, 0)),
            hvec, hvec, cube, cube,
            whole((pl.Element(1), pl.Element(D_MODEL), pl.Element(N_BRANCH * D_MODEL)),
                  lambda i: (layer, 0, W_COLS)),
            pl.BlockSpec((1, N_BRANCH * D_MODEL), lambda i: (0, 0)),
            whole((None, N_BRANCH, BRANCH_W, D_MODEL), lambda i: (layer, 0, 0, 0)),
            whole((None, D_MODEL, D_MODEL), lambda i: (layer, 0, 0)),
            vec, vec,
        ],
        out_specs=pl.BlockSpec((MERGE_TM, D_MODEL), lambda i: (i, 0)),
        out_shape=jax.ShapeDtypeStruct((t, D_MODEL), F32),
        scratch_shapes=[pltpu.VMEM((MERGE_TM, BRANCH_W), BF16)],
        compiler_params=_params("arbitrary"),
        name="sg_merge_out",
    )(*ys, zf, zf, x2, sg_ln_g, sg_ln_b, ws, bs_b, w_gate, b_gate, w_branch, w_out, ln_g, ln_b)


def _mlp_kernel(x_ref, w1_ref, b1_ref, w2_ref, b2_ref, lg_ref, lb_ref, o_ref):
    def finish(rows, acc):
        o_ref[rows, :] = _layer_norm(ALPHA * x_ref[rows, :] + (acc + b2_ref[...]), lg_ref[...], lb_ref[...])

    done = []
    for r in range(MLP_TM // MLP_SUB):
        rows = slice(r * MLP_SUB, (r + 1) * MLP_SUB)
        xb = x_ref[rows, :].astype(BF16)
        acc = None
        for c in range(D_FF // MLP_FC):
            cols = slice(c * MLP_FC, (c + 1) * MLP_FC)
            hid = jnp.maximum(_dot(xb, w1_ref[:, cols]) + b1_ref[:, cols], 0.0)
            part = _dot((hid * hid).astype(BF16), w2_ref[cols, :])
            acc = part if acc is None else acc + part
        done.append((rows, acc))
    for rows, acc in done:
        finish(rows, acc)


def _mlp(x2, layer, w1, b1, w2, b2, ln_g, ln_b):
    t = x2.shape[0]
    vec = pl.BlockSpec((1, D_MODEL), lambda i: (0, 0))
    return pl.pallas_call(
        _mlp_kernel,
        grid=(t // MLP_TM,),
        in_specs=[
            pl.BlockSpec((MLP_TM, D_MODEL), lambda i: (i, 0)),
            pl.BlockSpec((None, D_MODEL, D_FF), lambda i: (layer, 0, 0), pipeline_mode=pl.Buffered(1)),
            pl.BlockSpec((1, D_FF), lambda i: (0, 0)),
            pl.BlockSpec((None, D_FF, D_MODEL), lambda i: (layer, 0, 0), pipeline_mode=pl.Buffered(1)),
            vec, vec, vec,
        ],
        out_specs=pl.BlockSpec((MLP_TM, D_MODEL), lambda i: (i, 0)),
        out_shape=jax.ShapeDtypeStruct((t, D_MODEL), F32),
        compiler_params=_params("arbitrary"),
        name="mlp",
    )(x2, w1, b1, w2, b2, ln_g, ln_b)


def _block_diag(w):
    per = LRU_LANES // LRU_HD
    groups = LRU_HEADS // per
    w4 = w.reshape(groups, per, LRU_HD, LRU_HD)
    eye = jnp.eye(per, dtype=w.dtype)
    return jnp.einsum('ghij,hk->ghikj', w4, eye).reshape(groups, LRU_LANES, LRU_LANES)


def _rope_tables(seq):
    half = RET_DK // 2
    pos = np.arange(seq, dtype=np.float32)
    inv_freq = (np.float32(ROPE_THETA) ** (-np.arange(half, dtype=np.float32) / np.float32(half))).astype(np.float32)
    ang = pos[:, None] * inv_freq[None, :]
    cos, sin = np.cos(ang), np.sin(ang)
    return (jnp.asarray(np.concatenate([cos, cos], axis=-1), F32),
            jnp.asarray(np.concatenate([-sin, sin], axis=-1), F32))


def _layer(x2, batch, seq, cos_t, sin_t, layer, big, b_in, conv_w, conv_b, lru_wa, lru_ba, lru_wx,
           lru_bx, lru_lambda, sg_ln_g, sg_ln_b, sg_ws, sg_bs, ln1_g, ln1_b, b1, b2, ln2_g, ln2_b):
    w_proj, w_gate, w_branch, w_out, w1, w2 = big
    gl0 = W_COLS
    zb, zf, y_a, y_b = _project(
        x2, w_proj, layer, b_in[None, :gl0], cos_t, sin_t, conv_w, conv_b[None, :],
        _block_diag(lru_wa).astype(BF16), lru_ba[None, :], _block_diag(lru_wx).astype(BF16),
        lru_bx[None, :], lru_lambda[None, :], batch, seq)
    y_c = _stick_breaking(zb, batch, seq)
    bs_b = jnp.broadcast_to(sg_bs[:, :, None], (SG_GROUPS, BLOCK, BRANCH_W // SG_GROUPS))

    x1 = _merge((y_a, y_b, y_c), zf, x2, sg_ln_g[None, :], sg_ln_b[None, :], sg_ws, bs_b, layer,
                w_gate, b_in[None, gl0:], w_branch, w_out, ln1_g[None, :], ln1_b[None, :])
    return _mlp(x1, layer, w1, b1[None, :], w2, b2[None, :], ln2_g[None, :], ln2_b[None, :])


def kernel(x, w_in, b_in, conv_w, conv_b, lru_wa, lru_ba, lru_wx, lru_bx, lru_lambda, sg_ln_g, sg_ln_b, sg_ws, sg_bs, w_branch, w_out, ln1_g, ln1_b, w1, b1, w2, b2, ln2_g, ln2_b):
    batch, seq, d = x.shape
    cos_t, sin_t = _rope_tables(seq)
    x2 = x.reshape(batch * seq, d)
    w_in_b = w_in.astype(BF16)
    big = (w_in_b, w_in_b, w_branch.astype(BF16), w_out.astype(BF16), w1.astype(BF16), w2.astype(BF16))
    per_layer = (b_in, conv_w, conv_b, lru_wa, lru_ba, lru_wx, lru_bx, lru_lambda, sg_ln_g,
                 sg_ln_b, sg_ws, sg_bs, ln1_g, ln1_b, b1, b2, ln2_g, ln2_b)
    for l in range(DEPTH):
        x2 = _layer(x2, batch, seq, cos_t, sin_t, l, big, *(p[l] for p in per_layer))
    return x2.reshape(batch, seq, d)
```

```python
import functools
import math

import jax
import jax.numpy as jnp
import numpy as np
from jax import lax
from jax.experimental import pallas as pl
from jax.experimental.pallas import tpu as pltpu

F32 = jnp.float32
BF16 = jnp.bfloat16

D_MODEL = 1024
DEPTH = 2
BLOCK = 128
BRANCH_W = D_MODEL // 2
N_BRANCH = 4
RET_HEADS = 4
RET_DK = BRANCH_W // RET_HEADS
ROPE_THETA = 10000.0
LRU_HEADS = 8
LRU_HD = BRANCH_W // LRU_HEADS
CONV_W = 4
LRU_C = 8.0
SB_HEADS = 4
SB_HD = BRANCH_W // SB_HEADS
SG_GROUPS = 4
D_FF = 4 * D_MODEL
ALPHA = (2 * DEPTH) ** 0.25
LN_EPS = 1e-5
LOG2E = 1.4426950408889634
SB_EXP_ZERO = -105.0

W_RET_Q, W_RET_K, W_RET_V, W_RET_G, W_LRU_X, W_LRU_G, W_SB_Q, W_SB_K, W_SB_V, W_SG_U, W_SG_V = range(11)
W_COLS = 11 * BRANCH_W
ZF_SG_U, ZF_SG_V = 0, 1
ZF_COLS = 2 * BRANCH_W
ZB_SB_Q, ZB_SB_K, ZB_SB_V = 0, 1, 2
ZB_COLS = 3 * BRANCH_W

VMEM_LIMIT = 60 * 1024 * 1024

PROJ_TM = 1024
LRU_ROWS = 128
LRU_LANES = 256
SB_TQ = 256
SB_STEP_BLOCKS = 4
MERGE_TM = 1024
MLP_TM = 1024
MLP_SUB = 512
MLP_FC = 4096


def _params(*sem):
    return pltpu.CompilerParams(dimension_semantics=sem, vmem_limit_bytes=VMEM_LIMIT)


def _dot(a, b):
    return jnp.dot(a, b, preferred_element_type=F32)


def _dot_nt(a, b):
    return lax.dot_general(a, b, (((1,), (1,)), ((), ())), preferred_element_type=F32)


def _gelu(x):
    k1 = -2.0 * math.sqrt(2.0 / math.pi) * LOG2E
    return x / (1.0 + jnp.exp2(x * (k1 + (k1 * 0.044715) * (x * x))))


def _layer_norm(h, g, b):
    mu = jnp.mean(h, axis=-1, keepdims=True)
    d = h - mu
    var = jnp.mean(d * d, axis=-1, keepdims=True)
    return d * lax.rsqrt(var + LN_EPS) * g + b


def _ret_log_g(h):
    return math.log1p(-(2.0 ** (-5.0 - h)))


def _ret_chunk(rows, zq_ref, zk_ref, zg_ref, zv_ref, cos_ref, sin_ref, ya_ref,
               state_ref, dmat_ref, qdec_ref, kdec_ref):
    heads = range(RET_HEADS)
    cols = [slice(h * RET_DK, (h + 1) * RET_DK) for h in heads]
    cs = cos_ref[rows, :]
    sn = sin_ref[rows, :]
    qr, kr = [], []
    for h in heads:
        q = zq_ref[rows, cols[h]]
        k = zk_ref[rows, cols[h]]
        qr.append(q * cs + pltpu.roll(q, RET_DK // 2, 1) * sn)
        kr.append((k * cs + pltpu.roll(k, RET_DK // 2, 1) * sn) * (RET_DK ** -0.5))
    scores = [_dot_nt(qr[h].astype(BF16), kr[h].astype(BF16)) * dmat_ref[h] for h in heads]
    cross = [_dot((qr[h] * qdec_ref[h]).astype(BF16), state_ref[h].astype(BF16)) for h in heads]
    kd_t = [jnp.transpose(kr[h] * kdec_ref[h]).astype(BF16) for h in heads]
    for h in heads:
        v = zv_ref[rows, cols[h]]
        y = _dot(scores[h].astype(BF16), v) + cross[h]
        state_ref[h] = math.exp(_ret_log_g(h) * BLOCK) * state_ref[h] + _dot(kd_t[h], v)
        mu = jnp.mean(y, axis=-1, keepdims=True)
        d = y - mu
        var = jnp.mean(d * d, axis=-1, keepdims=True)
        yn = d * lax.rsqrt(var + LN_EPS)
        ya_ref[rows, cols[h]] = (jax.nn.silu(zg_ref[rows, cols[h]]) * yn).astype(ya_ref.dtype)


def _proj_kernel(x_ref, w_ref, b_ref, cos_ref, sin_ref, cw_ref, cb_ref, wa_ref, ba_ref, wx_ref, bx_ref,
                 lam_ref, zb_ref, zf_ref, ya_ref, yb_ref,
                 zq_ref, zk_ref, zg_ref, zv_ref, state_ref, dmat_ref, qdec_ref, kdec_ref,
                 zlx_ref, zlg_ref, tail_ref, h_ref):
    @pl.when(pl.program_id(1) == 0)
    def _():
        tail_ref[...] = jnp.zeros_like(tail_ref)
        h_ref[...] = jnp.zeros_like(h_ref)
        state_ref[...] = jnp.zeros_like(state_ref)
        row = lax.broadcasted_iota(jnp.int32, (BLOCK, BLOCK), 0).astype(F32)
        col = lax.broadcasted_iota(jnp.int32, (BLOCK, BLOCK), 1).astype(F32)
        diff = row - col
        for h in range(RET_HEADS):
            lg = _ret_log_g(h)
            dmat_ref[h] = jnp.where(diff >= 0, jnp.exp(lg * jnp.maximum(diff, 0.0)), 0.0)
            qdec_ref[h] = jnp.exp(lg * (row + 1.0))
            kdec_ref[h] = jnp.exp(lg * (BLOCK - 1.0 - row))

    xb = x_ref[...].astype(BF16)

    half = BRANCH_W // 2

    def project(part, dst_ref, row0, col0, scale=None):
        def piece(hf):
            cols = slice(part * BRANCH_W + hf * half, part * BRANCH_W + (hf + 1) * half)
            acc = _dot(xb, w_ref[:, cols]) + b_ref[:, cols]
            if scale is not None:
                acc = acc * scale
            dst_ref[row0:row0 + PROJ_TM, col0 + hf * half:col0 + (hf + 1) * half] = acc.astype(dst_ref.dtype)
        return [functools.partial(piece, 0), functools.partial(piece, 1)]

    def ret(c):
        _ret_chunk(slice(c * BLOCK, (c + 1) * BLOCK), zq_ref, zk_ref, zg_ref, zv_ref, cos_ref, sin_ref,
                   ya_ref, state_ref, dmat_ref, qdec_ref, kdec_ref)

    def lru(blk, hf):
        _lru_piece(blk * LRU_ROWS, hf, zlx_ref, cw_ref, cb_ref, wa_ref, ba_ref, wx_ref, bx_ref, lam_ref,
                   zlg_ref, tail_ref, h_ref, yb_ref)

    for piece in project(W_LRU_X, zlx_ref, 0, 0) + project(W_LRU_G, zlg_ref, 0, 0):
        piece()
    matmuls = (project(W_RET_Q, zq_ref, 0, 0) + project(W_RET_K, zk_ref, 0, 0)
               + project(W_RET_G, zg_ref, 0, 0) + project(W_RET_V, zv_ref, 0, 0)
               + project(W_SG_U, zf_ref, 0, ZF_SG_U * BRANCH_W) + project(W_SG_V, zf_ref, 0, ZF_SG_V * BRANCH_W)
               + project(W_SB_Q, zb_ref, 0, ZB_SB_Q * BRANCH_W, SB_HD ** -0.5)
               + project(W_SB_K, zb_ref, 0, ZB_SB_K * BRANCH_W) + project(W_SB_V, zb_ref, 0, ZB_SB_V * BRANCH_W))
    nblk = PROJ_TM // LRU_ROWS
    lru_pieces = [functools.partial(lru, blk, hf) for blk in range(nblk)
                  for hf in range(BRANCH_W // LRU_LANES)]
    ret_pieces = [functools.partial(ret, c) for c in range(PROJ_TM // BLOCK)]
    late = len(ret_pieces)
    branch = lru_pieces[:-late]
    for c, piece in enumerate(ret_pieces):
        branch += [piece] + lru_pieces[len(lru_pieces) - late + c:len(lru_pieces) - late + c + 1]
    ret_needs = 8
    issued = 0
    for n, piece in enumerate(branch):
        if piece in ret_pieces and issued < ret_needs:
            for m in matmuls[issued:ret_needs]:
                m()
            issued = ret_needs
        piece()
        upto = max(issued, min(len(matmuls), -(-len(matmuls) * (n + 1) // len(branch))))
        for m in matmuls[issued:upto]:
            m()
        issued = upto
    for m in matmuls[issued:]:
        m()


def _project(x2, w_layers, layer, b, cos_t, sin_t, conv_w, conv_b, wa_bd, ba, wx_bd, bx, lam, batch, seq):
    t, d = x2.shape
    nblk = seq // PROJ_TM
    row_map = lambda bi, n: (bi * nblk + n, 0)
    whole = functools.partial(pl.BlockSpec, pipeline_mode=pl.Buffered(1))
    tile = lambda width: pl.BlockSpec((PROJ_TM, width), row_map)
    vec = pl.BlockSpec((1, BRANCH_W), lambda bi, n: (0, 0))
    mat = pl.BlockSpec((BRANCH_W // LRU_LANES, LRU_LANES, LRU_LANES), lambda bi, n: (0, 0, 0))
    tile_f32 = pltpu.VMEM((PROJ_TM, BRANCH_W), F32)
    return pl.pallas_call(
        _proj_kernel,
        grid=(batch, nblk),
        in_specs=[
            tile(d),
            whole((None, d, W_COLS), lambda bi, n: (layer, 0, 0)),
            pl.BlockSpec((1, W_COLS), lambda bi, n: (0, 0)),
            pl.BlockSpec((PROJ_TM, RET_DK), lambda bi, n: (n, 0)),
            pl.BlockSpec((PROJ_TM, RET_DK), lambda bi, n: (n, 0)),
            pl.BlockSpec((CONV_W, BRANCH_W), lambda bi, n: (0, 0)),
            vec, mat, vec, mat, vec, vec,
        ],
        out_specs=[tile(ZB_COLS), tile(ZF_COLS), tile(BRANCH_W), tile(BRANCH_W)],
        out_shape=[jax.ShapeDtypeStruct((t, ZB_COLS), BF16), jax.ShapeDtypeStruct((t, ZF_COLS), F32),
                   jax.ShapeDtypeStruct((t, BRANCH_W), BF16), jax.ShapeDtypeStruct((t, BRANCH_W), BF16)],
        scratch_shapes=[tile_f32] * 3 + [pltpu.VMEM((PROJ_TM, BRANCH_W), BF16)]
        + [pltpu.VMEM((RET_HEADS, RET_DK, RET_DK), F32)] * 4
        + [tile_f32, tile_f32, pltpu.VMEM((8 * (CONV_W - 1), BRANCH_W), F32), pltpu.VMEM((8, BRANCH_W), F32)],
        compiler_params=_params("arbitrary", "arbitrary"),
        name="in_proj_retention_lru",
    )(x2, w_layers, b, cos_t, sin_t, conv_w, conv_b, wa_bd, ba, wx_bd, bx, lam)


def _lru_piece(r0, hf, zlx_ref, cw_ref, cb_ref, wa_ref, ba_ref, wx_ref, bx_ref, lam_ref, gate_ref,
               tail_ref, h_ref, o_ref):
    rows = LRU_ROWS
    per = rows // 8
    half = LRU_LANES
    lanes = slice(hf * half, (hf + 1) * half)
    xp = pltpu.einshape("(sj)c->(js)c", zlx_ref[r0:r0 + rows, lanes], s=8)
    sub = lax.broadcasted_iota(jnp.int32, (8, half), 0)

    def delayed(k):
        heads = []
        for m in range(per - k, per):
            prev = tail_ref[8 * (m - (per - 3)):8 * (m - (per - 3)) + 8, lanes]
            heads.append(pltpu.roll(jnp.where(sub == 7, prev, xp[8 * m:8 * m + 8, :]), 1, 0))
        return jnp.concatenate(heads + [xp[0:8 * (per - k), :]], axis=0)

    xc = cb_ref[:, lanes] + cw_ref[CONV_W - 1:CONV_W, lanes] * xp
    for k in range(1, CONV_W):
        xc = xc + cw_ref[CONV_W - 1 - k:CONV_W - k, lanes] * delayed(k)
    tail_ref[:, lanes] = xp[8 * (per - 3):8 * per, :]

    xcb = xc.astype(BF16)
    r = jax.nn.sigmoid(_dot(xcb, wa_ref[hf]) + ba_ref[:, lanes])
    i = jax.nn.sigmoid(_dot(xcb, wx_ref[hf]) + bx_ref[:, lanes])
    nlam = -lam_ref[:, lanes]
    softplus = jnp.maximum(nlam, 0.0) + jnp.log1p(jnp.exp(-jnp.abs(nlam)))
    log_a = -LRU_C * r * softplus
    a_all = jnp.exp(log_a)
    u_all = jnp.sqrt(jnp.tanh(-log_a) * (1.0 + a_all * a_all)) * (i * xc)

    sub1 = sub[:, 0:128]
    slabs = []
    for s in range(half // 128):
        sl = slice(s * 128, (s + 1) * 128)
        cols = slice(hf * half + s * 128, hf * half + (s + 1) * 128)
        run_h, run_p = [u_all[0:8, sl]], [a_all[0:8, sl]]
        for j in range(1, per):
            aj = a_all[8 * j:8 * j + 8, sl]
            run_h.append(aj * run_h[-1] + u_all[8 * j:8 * j + 8, sl])
            run_p.append(aj * run_p[-1])
        tot_h, tot_p = run_h[-1], run_p[-1]
        for shift in (1, 2, 4):
            keep = sub1 >= shift
            p_sh = jnp.where(keep, pltpu.roll(tot_p, shift, 0), 1.0)
            h_sh = jnp.where(keep, pltpu.roll(tot_h, shift, 0), 0.0)
            tot_h = tot_p * h_sh + tot_h
            tot_p = tot_p * p_sh
        hb = h_ref[:, cols]
        ends = tot_h + tot_p * hb
        start = jnp.where(sub1 == 0, hb, pltpu.roll(ends, 1, 0))
        h_ref[:, cols] = jnp.broadcast_to(ends[7:8, :], (8, 128))
        slabs.append(jnp.concatenate([run_h[j] + run_p[j] * start for j in range(per)], axis=0))
    h = pltpu.einshape("(js)c->(sj)c", jnp.concatenate(slabs, axis=1), s=8)
    o_ref[r0:r0 + rows, lanes] = (h * _gelu(gate_ref[r0:r0 + rows, lanes])).astype(o_ref.dtype)


def _sb_key_blocks(q_ref, qrows, k_ref, v_ref, acc_ref, carry_ref, tri, offs, diag_mask):
    t = SB_TQ
    heads = range(SB_HEADS)
    blocks = range(len(offs))
    cols = [slice(h * SB_HD, (h + 1) * SB_HD) for h in heads]
    zs = [[_dot_nt(q_ref[qrows, cols[h]], k_ref[pl.ds(offs[b], t), cols[h]]) for b in blocks] for h in heads]
    cums = [[None] * len(offs) for _ in heads]
    for h in heads:
        for b in blocks:
            z = zs[h][b]
            sp = jnp.maximum(z, 0.0) + jnp.log(1.0 + jnp.exp2(jnp.abs(z) * (-LOG2E)))
            if b == 0 and diag_mask is not None:
                sp = jnp.where(diag_mask, sp, 0.0)
            hi = sp.astype(BF16)
            lo = (sp - hi.astype(F32)).astype(BF16)
            cums[h][b] = _dot(hi, tri) + _dot(lo, tri)
    for h in heads:
        carry = None if diag_mask is not None else carry_ref[h]
        pv = None
        for b in blocks:
            e = zs[h][b] - cums[h][b]
            if carry is not None:
                e = e - jnp.concatenate([carry] * (t // 128), axis=1)
            w = jnp.exp(e)
            if b == 0 and diag_mask is not None:
                w = jnp.where(diag_mask, w, 0.0)
            d = _dot(w.astype(BF16), v_ref[pl.ds(offs[b], t), cols[h]])
            pv = d if pv is None else pv + d
            block_sum = cums[h][b][:, 0:1]
            carry = jnp.broadcast_to(block_sum, (t, 128)) if carry is None else carry + block_sum
        carry_ref[h] = carry
        if diag_mask is not None:
            acc_ref[:, cols[h]] = pv
        else:
            acc_ref[:, cols[h]] += pv


def _sb_kernel(q_ref, k_ref, v_ref, o_ref, acc_ref, carry_ref):
    t = SB_TQ
    row = lax.broadcasted_iota(jnp.int32, (t, t), 0)
    col = lax.broadcasted_iota(jnp.int32, (t, t), 1)
    tri = (row >= col).astype(BF16)

    def any_weight_left():
        m = carry_ref[0]
        for h in range(1, SB_HEADS):
            m = jnp.minimum(m, carry_ref[h])
        return jnp.min(m) < -SB_EXP_ZERO

    def query_block(j, _):
        i = pl.program_id(1) * SB_STEP_BLOCKS + j
        qrows = pl.ds(pl.multiple_of(j * t, t), t)
        diag = pl.multiple_of(i * t, t)

        @pl.when(i == 0)
        def _():
            _sb_key_blocks(q_ref, qrows, k_ref, v_ref, acc_ref, carry_ref, tri, [diag], col < row)

        @pl.when(i > 0)
        def _():
            prev = pl.multiple_of((i - 1) * t, t)
            _sb_key_blocks(q_ref, qrows, k_ref, v_ref, acc_ref, carry_ref, tri, [diag, prev], col < row)

        def cond(state):
            step, live = state
            return jnp.logical_and(step < i, live)

        def body(state):
            step, _ = state
            off = pl.multiple_of((i - 1 - step) * t, t)
            _sb_key_blocks(q_ref, qrows, k_ref, v_ref, acc_ref, carry_ref, tri, [off], None)
            return step + 1, any_weight_left()

        lax.while_loop(cond, body, (jnp.int32(1), any_weight_left()))
        o_ref[qrows, :] = acc_ref[...].astype(o_ref.dtype)
        return 0

    lax.fori_loop(0, SB_STEP_BLOCKS, query_block, 0)


def _stick_breaking(zb, batch, seq):
    rows = SB_TQ * SB_STEP_BLOCKS
    zb3 = zb.reshape(batch, seq, ZB_COLS)
    return pl.pallas_call(
        _sb_kernel,
        grid=(batch, seq // rows),
        in_specs=[
            pl.BlockSpec((None, rows, BRANCH_W), lambda b, i: (b, i, ZB_SB_Q)),
            pl.BlockSpec((None, seq, BRANCH_W), lambda b, i: (b, 0, ZB_SB_K)),
            pl.BlockSpec((None, seq, BRANCH_W), lambda b, i: (b, 0, ZB_SB_V)),
        ],
        out_specs=pl.BlockSpec((None, rows, BRANCH_W), lambda b, i: (b, i, 0)),
        out_shape=jax.ShapeDtypeStruct((batch, seq, BRANCH_W), BF16),
        scratch_shapes=[pltpu.VMEM((SB_TQ, BRANCH_W), F32), pltpu.VMEM((SB_HEADS, SB_TQ, 128), F32)],
        compiler_params=_params("arbitrary", "arbitrary"),
        name="stick_breaking",
    )(zb3, zb3, zb3).reshape(batch * seq, BRANCH_W)


def _sg_chunk(rows, u_ref, v_ref, g_ref, b_ref, ws_ref, bs_ref, o_ref):
    u = _gelu(u_ref[rows, :])
    v = _layer_norm(_gelu(v_ref[rows, :]), g_ref[...], b_ref[...]).astype(BF16)
    row = lax.broadcasted_iota(jnp.int32, (BLOCK, BLOCK), 0)
    col = lax.broadcasted_iota(jnp.int32, (BLOCK, BLOCK), 1)
    gd = BRANCH_W // SG_GROUPS
    for g in range(SG_GROUPS):
        w = jnp.where(col <= row, ws_ref[g], 0.0).astype(BF16)
        cols = slice(g * gd, (g + 1) * gd)
        s = _dot(w, v[:, cols]) + bs_ref[g]
        o_ref[rows, cols] = (u[:, cols] * s).astype(o_ref.dtype)


def _merge_kernel(ya_ref, yb_ref, yc_ref, u_ref, v_ref, x_ref, sg_g_ref, sg_b_ref, ws_ref, bs_ref,
                  wg_ref, bg_ref, wb_ref, wo_ref, lg_ref, lb_ref, o_ref, yd_ref):
    xb = x_ref[...].astype(BF16)
    nchunk = MERGE_TM // BLOCK
    third = -(-nchunk // 3)
    sg_after = {0: range(0, third), 1: range(third, 2 * third), 2: range(2 * third, nchunk)}
    merged = None
    for n, y_ref in enumerate((ya_ref, yb_ref, yc_ref, yd_ref)):
        cols = slice(n * D_MODEL, (n + 1) * D_MODEL)
        logits = _dot(xb, wg_ref[0, :, cols]) + bg_ref[:, cols]
        for c in sg_after.get(n, ()):
            _sg_chunk(slice(c * BLOCK, (c + 1) * BLOCK), u_ref, v_ref, sg_g_ref, sg_b_ref, ws_ref, bs_ref,
                      yd_ref)
        term = jax.nn.sigmoid(logits) * _dot(y_ref[...], wb_ref[n])
        merged = term if merged is None else merged + term
    mb = merged.astype(BF16)
    nsub = 4
    sub = MERGE_TM // nsub
    outs = [_dot(mb[r * sub:(r + 1) * sub, :], wo_ref[...]) for r in range(nsub)]
    for r in range(nsub):
        rows = slice(r * sub, (r + 1) * sub)
        o_ref[rows, :] = _layer_norm(ALPHA * x_ref[rows, :] + outs[r], lg_ref[...], lb_ref[...])


def _merge(ys, zf, x2, sg_ln_g, sg_ln_b, ws, bs_b, layer, w_gate, b_gate, w_branch, w_out, ln_g, ln_b):
    t = x2.shape[0]
    ysp = pl.BlockSpec((MERGE_TM, BRANCH_W), lambda i: (i, 0))
    vec = pl.BlockSpec((1, D_MODEL), lambda i: (0, 0))
    hvec = pl.BlockSpec((1, BRANCH_W), lambda i: (0, 0))
    cube = pl.BlockSpec((SG_GROUPS, BLOCK, BLOCK), lambda i: (0, 0, 0))
    whole = functools.partial(pl.BlockSpec, pipeline_mode=pl.Buffered(1))
    return pl.pallas_call(
        _merge_kernel,
        grid=(t // MERGE_TM,),
        in_specs=[ysp] * 3 + [
            pl.BlockSpec((MERGE_TM, BRANCH_W), lambda i: (i, ZF_SG_U)),
            pl.BlockSpec((MERGE_TM, BRANCH_W), lambda i: (i, ZF_SG_V)),
            pl.BlockSpec((MERGE_TM, D_MODEL), lambda i: (i, 0)),
            hvec, hvec, cube, cube,
            whole((pl.Element(1), pl.Element(D_MODEL), pl.Element(N_BRANCH * D_MODEL)),
                  lambda i: (layer, 0, W_COLS)),
            pl.BlockSpec((1, N_BRANCH * D_MODEL), lambda i: (0, 0)),
            whole((None, N_BRANCH, BRANCH_W, D_MODEL), lambda i: (layer, 0, 0, 0)),
            whole((None, D_MODEL, D_MODEL), lambda i: (layer, 0, 0)),
            vec, vec,
        ],
        out_specs=pl.BlockSpec((MERGE_TM, D_MODEL), lambda i: (i, 0)),
        out_shape=jax.ShapeDtypeStruct((t, D_MODEL), F32),
        scratch_shapes=[pltpu.VMEM((MERGE_TM, BRANCH_W), BF16)],
        compiler_params=_params("arbitrary"),
        name="sg_merge_out",
    )(*ys, zf, zf, x2, sg_ln_g, sg_ln_b, ws, bs_b, w_gate, b_gate, w_branch, w_out, ln_g, ln_b)


def _mlp_kernel(x_ref, w1_ref, b1_ref, w2_ref, b2_ref, lg_ref, lb_ref, o_ref):
    def finish(rows, acc):
        o_ref[rows, :] = _layer_norm(ALPHA * x_ref[rows, :] + (acc + b2_ref[...]), lg_ref[...], lb_ref[...])

    done = []
    for r in range(MLP_TM // MLP_SUB):
        rows = slice(r * MLP_SUB, (r + 1) * MLP_SUB)
        xb = x_ref[rows, :].astype(BF16)
        acc = None
        for c in range(D_FF // MLP_FC):
            cols = slice(c * MLP_FC, (c + 1) * MLP_FC)
            hid = jnp.maximum(_dot(xb, w1_ref[:, cols]) + b1_ref[:, cols], 0.0)
            part = _dot((hid * hid).astype(BF16), w2_ref[cols, :])
            acc = part if acc is None else acc + part
        done.append((rows, acc))
    for rows, acc in done:
        finish(rows, acc)


def _mlp(x2, layer, w1, b1, w2, b2, ln_g, ln_b):
    t = x2.shape[0]
    vec = pl.BlockSpec((1, D_MODEL), lambda i: (0, 0))
    return pl.pallas_call(
        _mlp_kernel,
        grid=(t // MLP_TM,),
        in_specs=[
            pl.BlockSpec((MLP_TM, D_MODEL), lambda i: (i, 0)),
            pl.BlockSpec((None, D_MODEL, D_FF), lambda i: (layer, 0, 0), pipeline_mode=pl.Buffered(1)),
            pl.BlockSpec((1, D_FF), lambda i: (0, 0)),
            pl.BlockSpec((None, D_FF, D_MODEL), lambda i: (layer, 0, 0), pipeline_mode=pl.Buffered(1)),
            vec, vec, vec,
        ],
        out_specs=pl.BlockSpec((MLP_TM, D_MODEL), lambda i: (i, 0)),
        out_shape=jax.ShapeDtypeStruct((t, D_MODEL), F32),
        compiler_params=_params("arbitrary"),
        name="mlp",
    )(x2, w1, b1, w2, b2, ln_g, ln_b)


def _block_diag(w):
    per = LRU_LANES // LRU_HD
    groups = LRU_HEADS // per
    w4 = w.reshape(groups, per, LRU_HD, LRU_HD)
    eye = jnp.eye(per, dtype=w.dtype)
    return jnp.einsum('ghij,hk->ghikj', w4, eye).reshape(groups, LRU_LANES, LRU_LANES)


def _rope_tables(seq):
    half = RET_DK // 2
    pos = np.arange(seq, dtype=np.float32)
    inv_freq = (np.float32(ROPE_THETA) ** (-np.arange(half, dtype=np.float32) / np.float32(half))).astype(np.float32)
    ang = pos[:, None] * inv_freq[None, :]
    cos, sin = np.cos(ang), np.sin(ang)
    return (jnp.asarray(np.concatenate([cos, cos], axis=-1), F32),
            jnp.asarray(np.concatenate([-sin, sin], axis=-1), F32))


def _layer(x2, batch, seq, cos_t, sin_t, layer, big, b_in, conv_w, conv_b, lru_wa, lru_ba, lru_wx,
           lru_bx, lru_lambda, sg_ln_g, sg_ln_b, sg_ws, sg_bs, ln1_g, ln1_b, b1, b2, ln2_g, ln2_b):
    w_proj, w_gate, w_branch, w_out, w1, w2 = big
    gl0 = W_COLS
    zb, zf, y_a, y_b = _project(
        x2, w_proj, layer, b_in[None, :gl0], cos_t, sin_t, conv_w, conv_b[None, :],
        _block_diag(lru_wa).astype(BF16), lru_ba[None, :], _block_diag(lru_wx).astype(BF16),
        lru_bx[None, :], lru_lambda[None, :], batch, seq)
    y_c = _stick_breaking(zb, batch, seq)
    bs_b = jnp.broadcast_to(sg_bs[:, :, None], (SG_GROUPS, BLOCK, BRANCH_W // SG_GROUPS))

    x1 = _merge((y_a, y_b, y_c), zf, x2, sg_ln_g[None, :], sg_ln_b[None, :], sg_ws, bs_b, layer,
                w_gate, b_in[None, gl0:], w_branch, w_out, ln1_g[None, :], ln1_b[None, :])
    return _mlp(x1, layer, w1, b1[None, :], w2, b2[None, :], ln2_g[None, :], ln2_b[None, :])


def kernel(x, w_in, b_in, conv_w, conv_b, lru_wa, lru_ba, lru_wx, lru_bx, lru_lambda, sg_ln_g, sg_ln_b, sg_ws, sg_bs, w_branch, w_out, ln1_g, ln1_b, w1, b1, w2, b2, ln2_g, ln2_b):
    batch, seq, d = x.shape
    cos_t, sin_t = _rope_tables(seq)
    x2 = x.reshape(batch * seq, d)
    w_in_b = w_in.astype(BF16)
    big = (w_in_b, w_in_b, w_branch.astype(BF16), w_out.astype(BF16), w1.astype(BF16), w2.astype(BF16))
    per_layer = (b_in, conv_w, conv_b, lru_wa, lru_ba, lru_wx, lru_bx, lru_lambda, sg_ln_g,
                 sg_ln_b, sg_ws, sg_bs, ln1_g, ln1_b, b1, b2, ln2_g, ln2_b)
    for l in range(DEPTH):
        x2 = _layer(x2, batch, seq, cos_t, sin_t, l, big, *(p[l] for p in per_layer))
    return x2.reshape(batch, seq, d)
```
